```python
import jax, jax.numpy as jnp
from jax import lax
import numpy as np

D_MODEL = 4096
BATCH = 32
SEQ = 256
DEPTH = 2
DEC_BATCH = 2
DEC_SEQ = 1024
PAST_LEN = 512

GRID_W = 64
A_HEADS = 8
A_DK = 128
A_DV = 128
A_KEY = A_HEADS * A_DK
A_WIDTH = A_HEADS * A_DV
B_HEADS = 4
B_DK = 128
B_DV = 256
B_KEY = B_HEADS * B_DK
B_WIDTH = B_HEADS * B_DV
GLA_RANK = 16
GLA_GATE_NORM = 16.0
C_HEADS = 16
Q_LORA = 1024
KV_LORA = 512
QK_NOPE = 128
QK_ROPE = 64
C_DV = 128
C_WIDTH = C_HEADS * C_DV
QK_DIM = QK_NOPE + QK_ROPE
ROPE_AXIS = QK_ROPE // 2
ROPE_BASE = 10000.0
N_BRANCH = 3
N_MOD = 6
D_FF = 4 * D_MODEL
CHUNK = 32
Q_BLOCK = 128
EPS = 1e-6
IN_SPLITS = (A_KEY, A_KEY, A_KEY, A_WIDTH, A_WIDTH,
             B_KEY, B_KEY, B_WIDTH, B_WIDTH, GLA_RANK, GLA_RANK,
             Q_LORA, KV_LORA + QK_ROPE,
             D_MODEL, D_MODEL, D_MODEL)
IN_COLS = (3 * A_KEY + 2 * A_WIDTH + 2 * B_KEY + 2 * B_WIDTH + 2 * GLA_RANK
           + Q_LORA + KV_LORA + QK_ROPE + N_BRANCH * D_MODEL)

kernel_name = "hybrid_diffusion_hgrn2_gla_mla_step"


def rms_norm(x, gain):
    xf = x.astype(jnp.float32)
    y = xf * lax.rsqrt(jnp.mean(xf * xf, axis=-1, keepdims=True) + EPS)
    return (y * gain.astype(jnp.float32)).astype(x.dtype)


def heads(t, n):
    return t.reshape(t.shape[:-1] + (n, t.shape[-1] // n))


def merge_heads(t):
    return t.reshape(t.shape[:-2] + (t.shape[-2] * t.shape[-1],))


def chunked_gated_scan(q, k, v, log_a, s0):
    b, l, h, _ = q.shape
    n = l // CHUNK

    def to_chunks(t):
        return t.reshape(b, n, CHUNK, h, t.shape[-1]).transpose(1, 0, 3, 2, 4)

    causal = jnp.tril(jnp.ones((CHUNK, CHUNK), dtype=bool))[:, :, None]

    def step(s, inp):
        qi, ki, vi, gi = inp
        cum = jnp.cumsum(gi, axis=-2)
        diff = cum[..., :, None, :] - cum[..., None, :, :]
        decay = jnp.exp(jnp.where(causal, diff, -jnp.inf))
        attn = jnp.einsum("bhic,bhjc,bhijc->bhij", qi, ki, decay)
        o = (jnp.einsum("bhij,bhjv->bhiv", attn, vi)
             + jnp.einsum("bhic,bhcv->bhiv", qi * jnp.exp(cum), s))
        last = cum[..., -1:, :]
        s_new = (jnp.exp(last[..., 0, :])[..., None] * s
                 + jnp.einsum("bhjc,bhjv->bhcv", ki * jnp.exp(last - cum), vi))
        return s_new, o

    s_fin, o = lax.scan(step, s0, (to_chunks(q), to_chunks(k), to_chunks(v), to_chunks(log_a)))
    o = o.transpose(1, 0, 3, 2, 4).reshape(b, l, h, v.shape[-1])
    return o, s_fin


def bidir_scan(q, k_f, k_b, v, g_f, g_b, s0):
    o_f, s_f = chunked_gated_scan(q, k_f, v, g_f, s0[:, 0].astype(jnp.float32))
    flip = lambda t: jnp.flip(t, axis=1)
    o_b, s_b = chunked_gated_scan(flip(q), flip(k_b), flip(v), flip(g_b), s0[:, 1].astype(jnp.float32))
    return o_f + flip(o_b), jnp.stack([s_f, s_b], axis=1)


def hgrn2_branch(z_q, z_ff, z_fb, z_i, z_g, lb, out_gain, s0):
    f32 = jnp.float32

    def forget(z, lbd):
        zf = z.astype(f32)
        log_f = jnp.logaddexp(jnp.log(lbd), jnp.log1p(-lbd) + jax.nn.log_sigmoid(zf))
        one_minus_f = (1.0 - lbd) * jax.nn.sigmoid(-zf)
        return heads(log_f, A_HEADS), heads(one_minus_f, A_HEADS)

    g_f, k_f = forget(z_ff, lb[0])
    g_b, k_b = forget(z_fb, lb[1])
    q = heads(z_q.astype(f32), A_HEADS)
    v = heads(z_i.astype(f32), A_HEADS)
    o, s_fin = bidir_scan(q, k_f, k_b, v, g_f, g_b, s0)
    o = rms_norm(o, out_gain) * jax.nn.silu(heads(z_g.astype(f32), A_HEADS))
    return merge_heads(o).astype(z_q.dtype), s_fin


def gla_branch(z_q, z_k, z_v, z_r, z_lf, z_lb, w_gate_up, b_gate, out_gain, s0):
    f32 = jnp.float32
    q = heads(z_q.astype(f32), B_HEADS) * (B_DK ** -0.5)
    k = heads(z_k.astype(f32), B_HEADS)
    v = heads(z_v.astype(f32), B_HEADS)

    def decay(z, w, bias):
        logit = z.astype(f32) @ w.astype(f32) + bias.astype(f32)
        return heads(jax.nn.log_sigmoid(logit) / GLA_GATE_NORM, B_HEADS)

    g_f = decay(z_lf, w_gate_up[0], b_gate[0])
    g_b = decay(z_lb, w_gate_up[1], b_gate[1])
    o, s_fin = bidir_scan(q, k, k, v, g_f, g_b, s0)
    o = rms_norm(o, out_gain) * jax.nn.silu(heads(z_r.astype(f32), B_HEADS))
    return merge_heads(o).astype(z_q.dtype), s_fin


def axial_rope_tables(n):
    rows = n // GRID_W
    row = jnp.repeat(jnp.arange(rows, dtype=jnp.float32), GRID_W)
    col = jnp.tile(jnp.arange(GRID_W, dtype=jnp.float32), rows)
    inv_freq = ROPE_BASE ** (-jnp.arange(0, ROPE_AXIS, 2, dtype=jnp.float32) / ROPE_AXIS)
    ang = jnp.stack([row[:, None] * inv_freq, col[:, None] * inv_freq], axis=1)
    return jnp.cos(ang), jnp.sin(ang)


def rotate_rope_part(t, rope):
    cos, sin = rope
    cos = cos[None, :, None]
    sin = sin[None, :, None]
    tr = t[..., QK_NOPE:].astype(jnp.float32)
    tr = tr.reshape(tr.shape[:-1] + (2, 2, ROPE_AXIS // 2))
    t1, t2 = tr[..., 0, :], tr[..., 1, :]
    rot = jnp.stack([t1 * cos - t2 * sin, t2 * cos + t1 * sin], axis=-2)
    rot = rot.reshape(t.shape[:-1] + (QK_ROPE,)).astype(t.dtype)
    return jnp.concatenate([t[..., :QK_NOPE], rot], axis=-1)


def mla_keys_values(ckv, krope, w_kv_up, k_gain):
    kv = heads(ckv @ w_kv_up, C_HEADS)
    k_nope, v = kv[..., :QK_NOPE], kv[..., QK_NOPE:]
    k_r = jnp.broadcast_to(krope[..., None, :], k_nope.shape[:-1] + (QK_ROPE,))
    k = rms_norm(jnp.concatenate([k_nope, k_r], axis=-1), k_gain)
    return k, v


def block_attention(q, k, v):
    b, lq, h, d = q.shape
    nb = lq // Q_BLOCK
    kf = k.astype(jnp.float32)
    vf = v.astype(jnp.float32)
    scale = d ** -0.5
    qb = q.astype(jnp.float32).reshape(b, nb, Q_BLOCK, h, d).transpose(1, 0, 2, 3, 4)

    def attend(qi):
        s = jnp.einsum("bqhd,bkhd->bhqk", qi, kf) * scale
        p = jax.nn.softmax(s, axis=-1)
        return jnp.einsum("bhqk,bkhd->bqhd", p, vf)

    o = lax.map(attend, qb)
    return o.transpose(1, 0, 2, 3, 4).reshape(b, lq, h, vf.shape[-1]).astype(v.dtype)


def mla_branch(z_q, z_kv, lp, ctx_cache, rope):
    ckv = rms_norm(z_kv[..., :KV_LORA], lp["mla_kv_a_norm"])
    krope = z_kv[..., KV_LORA:]
    q = heads(rms_norm(z_q, lp["mla_q_a_norm"]) @ lp["mla_w_q_up"], C_HEADS)
    q = rms_norm(q, lp["mla_q_norm"])
    k, v = mla_keys_values(ckv, krope, lp["mla_w_kv_up"], lp["mla_k_norm"])
    if ctx_cache is not None:
        q = rotate_rope_part(q, rope)
        k = rotate_rope_part(k, rope)
        k_c, v_c = mla_keys_values(ctx_cache[..., :KV_LORA], ctx_cache[..., KV_LORA:],
                                   lp["mla_w_kv_up"], lp["mla_k_norm"])
        k = jnp.concatenate([k_c.astype(k.dtype), k], axis=1)
        v = jnp.concatenate([v_c.astype(v.dtype), v], axis=1)
    o = block_attention(q, k, v)
    return merge_heads(o), ckv, krope


def trunk_layer(x, cond, lp, lb, ctx, rope):
    b = x.shape[0]
    mod = (jax.nn.silu(cond) @ lp["w_mod"] + lp["b_mod"]).astype(x.dtype)
    mod = mod.reshape(-1, 1, N_MOD * D_MODEL)
    sh1, sc1, g1, sh2, sc2, g2 = jnp.split(mod, N_MOD, axis=-1)

    h = rms_norm(x, lp["norm1"]) * (1.0 + sc1) + sh1
    z = h @ lp["w_in"]
    points = np.cumsum(IN_SPLITS)[:-1].tolist()
    (a_q, a_ff, a_fb, a_i, a_g, b_q, b_k, b_v, b_r, b_lf, b_lb,
     c_q, c_kv, gt_a, gt_b, gt_c) = jnp.split(z, points, axis=-1)

    if ctx is None:
        s_a0 = jnp.zeros((b, 2, A_HEADS, A_DK, A_DV), jnp.float32)
        s_b0 = jnp.zeros((b, 2, B_HEADS, B_DK, B_DV), jnp.float32)
        mla_ctx = None
    else:
        s_a0, s_b0, mla_ctx = ctx["hgrn"], ctx["gla"], ctx["mla"]

    o_a, s_a = hgrn2_branch(a_q, a_ff, a_fb, a_i, a_g, lb, lp["hgrn_out_norm"], s_a0)
    o_b, s_b = gla_branch(b_q, b_k, b_v, b_r, b_lf, b_lb, lp["gla_w_gate_up"], lp["gla_b_gate"],
                          lp["gla_out_norm"], s_b0)
    o_c, ckv, krope = mla_branch(c_q, c_kv, lp, mla_ctx, rope)

    merged = (jax.nn.sigmoid(gt_a) * (o_a @ lp["w_branch_a"])
              + jax.nn.sigmoid(gt_b) * (o_b @ lp["w_branch_b"])
              + jax.nn.sigmoid(gt_c) * (o_c @ lp["w_branch_c"]))
    x = x + g1 * (merged @ lp["w_out"])

    h2 = rms_norm(x, lp["norm2"]) * (1.0 + sc2) + sh2
    x = x + g2 * (jnp.square(jax.nn.relu(h2 @ lp["w_mlp_in"])) @ lp["w_mlp_out"])
    return x, ckv, krope, s_a, s_b


def setup_inputs(seed: int = 0) -> dict:
    key = jax.random.key(seed)
    ks = iter(jax.random.split(key, 40))
    f32 = jnp.float32

    def normal(shape, scale):
        return scale * jax.random.normal(next(ks), shape, f32)

    def gain(shape):
        return 1.0 + normal(shape, 0.02)

    L = DEPTH
    return {
        "x_prompt": normal((BATCH, SEQ, D_MODEL), 1.0),
        "x_sample": normal((DEC_BATCH, DEC_SEQ, D_MODEL), 1.0),
        "cache_mla": normal((DEC_BATCH, DEPTH, PAST_LEN, KV_LORA + QK_ROPE), 1.0),
        "state_hgrn": normal((DEC_BATCH, DEPTH, 2, A_HEADS, A_DK, A_DV), 0.5),
        "state_gla": normal((DEC_BATCH, DEPTH, 2, B_HEADS, B_DK, B_DV), 0.5),
        "c": normal((DEC_BATCH, D_MODEL), 1.0),
        "c_ctx": normal((D_MODEL,), 1.0),
        "norm1": gain((L, D_MODEL)),
        "w_mod": normal((L, D_MODEL, N_MOD * D_MODEL), 0.5 * D_MODEL ** -0.5),
        "b_mod": normal((L, N_MOD * D_MODEL), 0.01),
        "w_in": normal((L, D_MODEL, IN_COLS), D_MODEL ** -0.5),
        "hgrn_lb_logits": normal((L, 2, A_KEY), 0.1),
        "hgrn_out_norm": gain((L, A_DV)),
        "gla_w_gate_up": normal((L, 2, GLA_RANK, B_KEY), GLA_RANK ** -0.5),
        "gla_b_gate": normal((L, 2, B_KEY), 0.01),
        "gla_out_norm": gain((L, B_DV)),
        "mla_q_a_norm": gain((L, Q_LORA)),
        "mla_w_q_up": normal((L, Q_LORA, C_HEADS * QK_DIM), Q_LORA ** -0.5),
        "mla_kv_a_norm": gain((L, KV_LORA)),
        "mla_w_kv_up": normal((L, KV_LORA, C_HEADS * (QK_NOPE + C_DV)), KV_LORA ** -0.5),
        "mla_q_norm": gain((L, QK_DIM)),
        "mla_k_norm": gain((L, QK_DIM)),
        "w_branch_a": normal((L, A_WIDTH, D_MODEL), A_WIDTH ** -0.5),
        "w_branch_b": normal((L, B_WIDTH, D_MODEL), B_WIDTH ** -0.5),
        "w_branch_c": normal((L, C_WIDTH, D_MODEL), C_WIDTH ** -0.5),
        "w_out": normal((L, D_MODEL, D_MODEL), D_MODEL ** -0.5),
        "norm2": gain((L, D_MODEL)),
        "w_mlp_in": normal((L, D_MODEL, D_FF), D_MODEL ** -0.5),
        "w_mlp_out": normal((L, D_FF, D_MODEL), D_FF ** -0.5),
    }


def reference(x_prompt, x_sample, cache_mla, state_hgrn, state_gla, c, c_ctx,
              norm1, w_mod, b_mod, w_in, hgrn_lb_logits, hgrn_out_norm,
              gla_w_gate_up, gla_b_gate, gla_out_norm,
              mla_q_a_norm, mla_w_q_up, mla_kv_a_norm, mla_w_kv_up, mla_q_norm, mla_k_norm,
              w_branch_a, w_branch_b, w_branch_c, w_out, norm2, w_mlp_in, w_mlp_out):
    lb_cum = jnp.cumsum(jax.nn.softmax(hgrn_lb_logits.astype(jnp.float32), axis=0), axis=0)
    lower_bounds = lb_cum - lb_cum[:1]
    rope = axial_rope_tables(x_sample.shape[1])

    y_prompt = x_prompt
    y_sample = x_sample
    caches, hgrn_states, gla_states = [], [], []
    for l in range(DEPTH):
        lp = {
            "norm1": norm1[l], "w_mod": w_mod[l], "b_mod": b_mod[l], "w_in": w_in[l],
            "hgrn_out_norm": hgrn_out_norm[l],
            "gla_w_gate_up": gla_w_gate_up[l], "gla_b_gate": gla_b_gate[l],
            "gla_out_norm": gla_out_norm[l],
            "mla_q_a_norm": mla_q_a_norm[l], "mla_w_q_up": mla_w_q_up[l],
            "mla_kv_a_norm": mla_kv_a_norm[l], "mla_w_kv_up": mla_w_kv_up[l],
            "mla_q_norm": mla_q_norm[l], "mla_k_norm": mla_k_norm[l],
            "w_branch_a": w_branch_a[l], "w_branch_b": w_branch_b[l], "w_branch_c": w_branch_c[l],
            "w_out": w_out[l], "norm2": norm2[l],
            "w_mlp_in": w_mlp_in[l], "w_mlp_out": w_mlp_out[l],
        }
        y_prompt, ckv, krope, s_a, s_b = trunk_layer(y_prompt, c_ctx, lp, lower_bounds[l], None, None)
        caches.append(jnp.concatenate([ckv, krope], axis=-1).astype(x_prompt.dtype))
        hgrn_states.append(s_a.astype(x_prompt.dtype))
        gla_states.append(s_b.astype(x_prompt.dtype))
        ctx = {"mla": cache_mla[:, l], "hgrn": state_hgrn[:, l], "gla": state_gla[:, l]}
        y_sample, _, _, _, _ = trunk_layer(y_sample, c, lp, lower_bounds[l], ctx, rope)

    new_cache_mla = jnp.stack(caches, axis=1)
    new_state_hgrn = jnp.stack(hgrn_states, axis=1)
    new_state_gla = jnp.stack(gla_states, axis=1)
    return (y_prompt, y_sample, new_cache_mla, new_state_hgrn, new_state_gla)
```

```python
import functools

import numpy as np
import jax
import jax.numpy as jnp
from jax import lax
from jax.experimental import pallas as pl
from jax.experimental.pallas import tpu as pltpu

F32 = jnp.float32
BF16 = jnp.bfloat16

D_MODEL = 4096
BATCH = 32
SEQ = 256
DEPTH = 2
DEC_BATCH = 2
DEC_SEQ = 1024
PAST_LEN = 512
GRID_W = 64
A_HEADS = 8
A_DK = 128
A_DV = 128
B_HEADS = 4
B_DK = 128
B_DV = 256
GLA_RANK = 16
GLA_GATE_NORM = 16.0
C_HEADS = 16
Q_LORA = 1024
KV_LORA = 512
QK_NOPE = 128
QK_ROPE = 64
C_DV = 128
QK_DIM = QK_NOPE + QK_ROPE
ROPE_AXIS = QK_ROPE // 2
ROPE_BASE = 10000.0
N_MOD = 6
D_FF = 4 * D_MODEL
EPS = 1e-6

N_PROMPT = BATCH * SEQ
N_SAMPLE = DEC_BATCH * DEC_SEQ
N_TOK = N_PROMPT + N_SAMPLE
MOD_ROWS = 8

Z_MAIN = 8192
Z_CQ = Z_MAIN
Z_CKV = Z_CQ + Q_LORA
Z_KR = Z_CKV + KV_LORA
Z_LR = Z_KR + 128
Z_COLS = Z_LR + 128
Z_TN = 768

LANES = 128
V7X_VMEM_BYTES = 64 * 1024 * 1024
VMEM_BIG = 56 * 1024 * 1024
VMEM_MID = 40 * 1024 * 1024

CH = 32
SB = 8


def _params(sem, vmem=VMEM_MID):
    return pltpu.CompilerParams(dimension_semantics=sem, vmem_limit_bytes=vmem)


def _mod_row(i, tm):
    return jnp.where(i < N_PROMPT // tm, 0, 1 + (i - N_PROMPT // tm) // (DEC_SEQ // tm))


def _sigmoid(x):
    return 1.0 / (1.0 + jnp.exp(-x))


def _log_sigmoid(x):
    return jnp.minimum(x, 0.0) - jnp.log1p(jnp.exp(-jnp.abs(x)))


def _mod_kernel(cond_ref, w_ref, b_ref, o_ref):
    c = cond_ref[...]
    s = (c * _sigmoid(c)).astype(BF16)
    o_ref[...] = jnp.dot(s, w_ref[...].astype(BF16), preferred_element_type=F32) + b_ref[...]


def _modulation(cond, w_mod, b_mod):
    tn = 512
    n = N_MOD * D_MODEL
    return pl.pallas_call(
        _mod_kernel,
        grid=(DEPTH, n // tn),
        in_specs=[
            pl.BlockSpec((MOD_ROWS, D_MODEL), lambda l, j: (0, 0)),
            pl.BlockSpec((None, D_MODEL, tn), lambda l, j: (l, 0, j)),
            pl.BlockSpec((None, 1, tn), lambda l, j: (l, 0, j)),
        ],
        out_specs=pl.BlockSpec((None, MOD_ROWS, tn), lambda l, j: (l, 0, j)),
        out_shape=jax.ShapeDtypeStruct((DEPTH, MOD_ROWS, n), F32),
        compiler_params=_params(("parallel", "parallel")),
        name="modulation",
    )(cond, w_mod, b_mod.reshape(DEPTH, 1, n))


def _norm_mod_kernel(x_ref, g_ref, sh_ref, sc_ref, o_ref):
    x = x_ref[...]
    ms = jnp.mean(x * x, axis=-1, keepdims=True)
    y = x * lax.rsqrt(ms + EPS) * g_ref[...]
    o_ref[...] = (y * (1.0 + sc_ref[...]) + sh_ref[...]).astype(BF16)


def _norm_mod(x, gain, mod, layer, shift_chunk):
    tm = 512
    base = layer * MOD_ROWS
    return pl.pallas_call(
        _norm_mod_kernel,
        grid=(N_TOK // tm,),
        in_specs=[
            pl.BlockSpec((tm, D_MODEL), lambda i: (i, 0)),
            pl.BlockSpec((1, D_MODEL), lambda i: (0, 0)),
            pl.BlockSpec((None, 1, D_MODEL), lambda i: (base + _mod_row(i, tm), 0, shift_chunk)),
            pl.BlockSpec((None, 1, D_MODEL), lambda i: (base + _mod_row(i, tm), 0, shift_chunk + 1)),
        ],
        out_specs=pl.BlockSpec((tm, D_MODEL), lambda i: (i, 0)),
        out_shape=jax.ShapeDtypeStruct((N_TOK, D_MODEL), BF16),
        compiler_params=_params(("parallel",)),
        name="norm_mod",
    )(x, gain.reshape(1, D_MODEL), mod, mod)


def _mm_kernel(x_ref, w_ref, o_ref, *, relu2):
    acc = jnp.dot(x_ref[...], w_ref[...], preferred_element_type=F32)
    if relu2:
        acc = jnp.square(jnp.maximum(acc, 0.0))
    o_ref[...] = acc.astype(o_ref.dtype)


def _matmul(x, w, out_dtype, tm, tn, relu2=False, name="matmul"):
    m, k = x.shape
    n = w.shape[1]
    return pl.pallas_call(
        functools.partial(_mm_kernel, relu2=relu2),
        grid=(n // tn, m // tm),
        in_specs=[
            pl.BlockSpec((tm, k), lambda j, i: (i, 0)),
            pl.BlockSpec((k, tn), lambda j, i: (0, j)),
        ],
        out_specs=pl.BlockSpec((tm, tn), lambda j, i: (i, j)),
        out_shape=jax.ShapeDtypeStruct((m, n), out_dtype),
        compiler_params=_params(("parallel", "parallel"), VMEM_BIG),
        name=name,
    )(x, w)


def _mm_res_kernel(x_ref, w_ref, res_ref, g_ref, o_ref, acc_ref, *, nk):
    k = pl.program_id(2)
    part = jnp.dot(x_ref[...], w_ref[...], preferred_element_type=F32)

    @pl.when(k == 0)
    def _():
        acc_ref[...] = part

    @pl.when(k > 0)
    def _():
        acc_ref[...] += part

    @pl.when(k == nk - 1)
    def _():
        o_ref[...] = res_ref[...] + g_ref[...] * acc_ref[...]


def _matmul_residual(a, w, res, mod, layer, gate_chunk, name):
    tm, tn, tk = 1024, 1024, 2048
    m, k = a.shape
    n = w.shape[1]
    nk = k // tk
    base = layer * MOD_ROWS
    cpc = D_MODEL // tn
    return pl.pallas_call(
        functools.partial(_mm_res_kernel, nk=nk),
        grid=(n // tn, m // tm, nk),
        in_specs=[
            pl.BlockSpec((tm, tk), lambda j, i, kk: (i, kk)),
            pl.BlockSpec((tk, tn), lambda j, i, kk: (kk, j)),
            pl.BlockSpec((tm, tn), lambda j, i, kk: (i, j)),
            pl.BlockSpec((None, 1, tn),
                         lambda j, i, kk: (base + _mod_row(i, tm), 0, gate_chunk * cpc + j)),
        ],
        out_specs=pl.BlockSpec((tm, tn), lambda j, i, kk: (i, j)),
        out_shape=jax.ShapeDtypeStruct((m, n), F32),
        scratch_shapes=[pltpu.VMEM((tm, tn), F32)],
        compiler_params=_params(("parallel", "parallel", "arbitrary"), VMEM_BIG),
        name=name,
    )(a, w, res, mod)


def _merge_kernel(h_ref, wga_ref, wgb_ref, wgc_ref, oa_ref, wa_ref, ob_ref, wb_ref, oc_ref, wc_ref,
                  o_ref):
    h = h_ref[...]

    def branch(wg_ref, o_r, w_r):
        g = jnp.dot(h, wg_ref[...], preferred_element_type=F32)
        p = jnp.dot(o_r[...], w_r[...], preferred_element_type=F32)
        return _sigmoid(g) * p

    acc = branch(wga_ref, oa_ref, wa_ref)
    acc = acc + branch(wgb_ref, ob_ref, wb_ref)
    acc = acc + branch(wgc_ref, oc_ref, wc_ref)
    o_ref[...] = acc.astype(BF16)


def _merge(h, wg, o_a, w_a, o_b, w_b, o_c, w_c):
    tm, tn = 512, 512
    nj = D_MODEL // tn
    act = lambda width: pl.BlockSpec((tm, width), lambda j, i: (i, 0))
    wcol = lambda rows: pl.BlockSpec((rows, tn), lambda j, i: (0, j))
    gate = lambda b: pl.BlockSpec((D_MODEL, tn), lambda j, i: (0, b * nj + j))
    return pl.pallas_call(
        _merge_kernel,
        grid=(nj, N_TOK // tm),
        in_specs=[act(D_MODEL), gate(0), gate(1), gate(2),
                  act(o_a.shape[1]), wcol(w_a.shape[0]),
                  act(o_b.shape[1]), wcol(w_b.shape[0]),
                  act(o_c.shape[1]), wcol(w_c.shape[0])],
        out_specs=pl.BlockSpec((tm, tn), lambda j, i: (i, j)),
        out_shape=jax.ShapeDtypeStruct((N_TOK, D_MODEL), BF16),
        compiler_params=_params(("parallel", "parallel"), VMEM_BIG),
        name="merge",
    )(h, wg, wg, wg, o_a, w_a, o_b, w_b, o_c, w_c)


def _seg_cumsum(g, rowc, reverse):
    n_rows = g.shape[0]
    x = g
    for sh in (1, 2, 4, 8, 16):
        if reverse:
            x = x + jnp.where(rowc < CH - sh, pltpu.roll(x, n_rows - sh, 0), 0.0)
        else:
            x = x + jnp.where(rowc >= sh, pltpu.roll(x, sh, 0), 0.0)
    return x


def _scan_dir(q, k, v, g, s0t, reverse, o_scr, accumulate, want_state):
    n_rows, dk = q.shape
    dv = v.shape[1]
    n = n_rows // CH
    rowc = lax.broadcasted_iota(jnp.int32, (n_rows, dk), 0) & (CH - 1)
    cum = _seg_cumsum(g, rowc, reverse)

    def chunk_row(r):
        return jnp.concatenate(
            [jnp.broadcast_to(cum[c * CH + r:c * CH + r + 1, :], (CH, dk)) for c in range(n)], axis=0)

    last_row = 0 if reverse else CH - 1
    last_b = chunk_row(last_row)
    qt = (q * jnp.exp(cum)).astype(BF16)
    kh = (k * jnp.exp(last_b - cum)).astype(BF16)
    vb = v.astype(BF16)

    rows = rowc & (SB - 1)
    o_diag = jnp.sum(q * k, axis=-1, keepdims=True) * v
    for d in range(1, SB):
        if reverse:
            sh, mask = n_rows - d, rows < SB - d
        else:
            sh, mask = d, rows >= d
        kd = pltpu.roll(k, sh, 0)
        cd = pltpu.roll(cum, sh, 0)
        vd = pltpu.roll(v, sh, 0)
        e = jnp.exp(jnp.where(mask, cum - cd, -jnp.inf))
        o_diag = o_diag + jnp.sum(q * kd * e, axis=-1, keepdims=True) * vd
    if accumulate:
        o_scr[...] += o_diag
    else:
        o_scr[...] = o_diag

    sub = rowc // SB
    qs, ks = [], []
    for pr in range(SB, CH, SB):
        if reverse:
            piv, qmask, kmask = chunk_row(pr), sub == pr // SB - 1, rowc >= pr
        else:
            piv, qmask, kmask = chunk_row(pr - 1), sub == pr // SB, rowc < pr
        qs.append((q * jnp.exp(jnp.where(qmask, cum - piv, -jnp.inf))).astype(BF16))
        ks.append((k * jnp.exp(jnp.where(kmask, piv - cum, -jnp.inf))).astype(BF16))
    qcat = jnp.concatenate(qs, axis=-1)
    kcat = jnp.concatenate(ks, axis=-1)

    trans_b = (((1,), (1,)), ((), ()))
    trans_a = (((0,), (0,)), ((), ()))
    st = s0t
    order = range(n - 1, -1, -1) if reverse else range(n)
    for idx, c in enumerate(order):
        sl = slice(c * CH, (c + 1) * CH)
        o_c = lax.dot_general(qt[sl], st.astype(BF16), trans_b, preferred_element_type=F32)
        a = lax.dot_general(qcat[sl], kcat[sl], trans_b, preferred_element_type=F32)
        o_c = o_c + jnp.dot(a.astype(BF16), vb[sl], preferred_element_type=F32)
        o_scr[sl, :] += o_c
        if want_state or idx + 1 < n:
            ut = lax.dot_general(vb[sl], kh[sl], trans_a, preferred_element_type=F32)
            r = c * CH + last_row
            st = st * jnp.exp(cum[r:r + 1, :]) + ut
    return st


def _finish_scan(o_scr, gain_ref, zg_ref, o_ref):
    o = o_scr[...]
    ms = jnp.mean(o * o, axis=-1, keepdims=True)
    zg = zg_ref[...]
    o_ref[...] = (o * lax.rsqrt(ms + EPS) * gain_ref[...] * (zg * _sigmoid(zg))).astype(BF16)


def _hgrn_kernel(*refs, layer, has_state, want_state):
    lbl_ref, q_ref, ff_ref, fb_ref, i_ref, zg_ref, gain_ref = refs[:7]
    pos = 7
    s0_ref = None
    if has_state:
        s0_ref = refs[pos]
        pos += 1
    o_ref = refs[pos]
    pos += 1
    st_ref = None
    if want_state:
        st_ref = refs[pos]
        pos += 1
    o_scr = refs[pos]

    logits = [lbl_ref[l] for l in range(DEPTH)]
    mx = functools.reduce(jnp.maximum, logits)
    ex = [jnp.exp(x - mx) for x in logits]
    tot = functools.reduce(lambda a, b: a + b, ex)
    probs = [e / tot for e in ex]
    cum_first = probs[0]
    cum_l = functools.reduce(lambda a, b: a + b, probs[:layer + 1])
    lb = cum_l - cum_first

    q = q_ref[...]
    v = i_ref[...]
    for d, (z_ref, reverse) in enumerate(((ff_ref, False), (fb_ref, True))):
        lbd = lb[d:d + 1, :]
        zf = z_ref[...]
        a = jnp.log(lbd)
        b = jnp.log1p(-lbd) + _log_sigmoid(zf)
        delta = a - b
        log_f = jnp.where(delta != delta, a + b,
                          jnp.maximum(a, b) + jnp.log1p(jnp.exp(-jnp.abs(delta))))
        one_minus_f = (1.0 - lbd) * _sigmoid(-zf)
        if has_state:
            s0t = s0_ref[d].T
        else:
            s0t = jnp.zeros((A_DV, A_DK), F32)
        st = _scan_dir(q, one_minus_f, v, log_f, s0t, reverse, o_scr, d == 1, want_state)
        if want_state:
            st_ref[d] = st.T
    _finish_scan(o_scr, gain_ref, zg_ref, o_ref)


def _hgrn(z, lb_logits, out_gain, state, layer, prompt):
    seq, nb = (SEQ, BATCH) if prompt else (DEC_SEQ, DEC_BATCH)
    row0 = 0 if prompt else N_PROMPT // seq
    zcol = lambda off: pl.BlockSpec((seq, A_DK), lambda b, h: (row0 + b, off + h))
    in_specs = [
        pl.BlockSpec((DEPTH, 2, A_DK), lambda b, h: (0, 0, h)),
        zcol(0), zcol(A_HEADS), zcol(2 * A_HEADS), zcol(3 * A_HEADS), zcol(4 * A_HEADS),
        pl.BlockSpec((1, A_DV), lambda b, h: (0, 0)),
    ]
    args = [lb_logits, z, z, z, z, z, out_gain.reshape(1, A_DV)]
    if not prompt:
        in_specs.append(pl.BlockSpec((None, None, 2, None, A_DK, A_DV),
                                     lambda b, h: (b, layer, 0, h, 0, 0)))
        args.append(state)
    out_specs = [pl.BlockSpec((seq, A_DV), lambda b, h: (b, h))]
    out_shape = [jax.ShapeDtypeStruct((nb * seq, A_HEADS * A_DV), BF16)]
    if prompt:
        out_specs.append(pl.BlockSpec((None, 2, None, A_DK, A_DV), lambda b, h: (b, 0, h, 0, 0)))
        out_shape.append(jax.ShapeDtypeStruct((nb, 2, A_HEADS, A_DK, A_DV), F32))
    return pl.pallas_call(
        functools.partial(_hgrn_kernel, layer=layer, has_state=not prompt, want_state=prompt),
        grid=(nb, A_HEADS),
        in_specs=in_specs,
        out_specs=out_specs,
        out_shape=out_shape,
        scratch_shapes=[pltpu.VMEM((seq, A_DV), F32)],
        compiler_params=_params(("parallel", "parallel")),
        name="hgrn_prompt" if prompt else "hgrn_sample",
    )(*args)


def _gla_kernel(*refs, has_state, want_state):
    q_ref, k_ref, v_ref, zr_ref, lr_ref, wg_ref, bg_ref, gain_ref = refs[:8]
    pos = 8
    s0_ref = None
    if has_state:
        s0_ref = refs[pos]
        pos += 1
    o_ref = refs[pos]
    pos += 1
    st_ref = None
    if want_state:
        st_ref = refs[pos]
        pos += 1
    o_scr = refs[pos]

    q = q_ref[...] * (B_DK ** -0.5)
    k = k_ref[...]
    v = v_ref[...]
    lr = lr_ref[...].astype(BF16)
    for d, reverse in enumerate((False, True)):
        logit = jnp.dot(lr, wg_ref[d], preferred_element_type=F32) + bg_ref[d]
        g = _log_sigmoid(logit) / GLA_GATE_NORM
        if has_state:
            s0t = s0_ref[d].T
        else:
            s0t = jnp.zeros((B_DV, B_DK), F32)
        st = _scan_dir(q, k, v, g, s0t, reverse, o_scr, d == 1, want_state)
        if want_state:
            st_ref[d] = st.T
    _finish_scan(o_scr, gain_ref, zr_ref, o_ref)


def _gla(z, w_gate_ext, b_gate, out_gain, state, layer, prompt):
    seq, nb = (SEQ, BATCH) if prompt else (DEC_SEQ, DEC_BATCH)
    row0 = 0 if prompt else N_PROMPT // seq
    zcol = lambda width, col: pl.BlockSpec((seq, width),
                                           lambda b, h: (row0 + b, col // width + h))
    in_specs = [
        zcol(B_DK, 5120), zcol(B_DK, 5632), zcol(B_DV, 6144), zcol(B_DV, 7168),
        pl.BlockSpec((seq, LANES), lambda b, h: (row0 + b, Z_LR // LANES)),
        pl.BlockSpec((2, LANES, B_DK), lambda b, h: (0, 0, h)),
        pl.BlockSpec((2, 1, B_DK), lambda b, h: (0, 0, h)),
        pl.BlockSpec((1, B_DV), lambda b, h: (0, 0)),
    ]
    args = [z, z, z, z, z, w_gate_ext, b_gate.reshape(2, 1, B_HEADS * B_DK),
            out_gain.reshape(1, B_DV)]
    if not prompt:
        in_specs.append(pl.BlockSpec((None, None, 2, None, B_DK, B_DV),
                                     lambda b, h: (b, layer, 0, h, 0, 0)))
        args.append(state)
    out_specs = [pl.BlockSpec((seq, B_DV), lambda b, h: (b, h))]
    out_shape = [jax.ShapeDtypeStruct((nb * seq, B_HEADS * B_DV), BF16)]
    if prompt:
        out_specs.append(pl.BlockSpec((None, 2, None, B_DK, B_DV), lambda b, h: (b, 0, h, 0, 0)))
        out_shape.append(jax.ShapeDtypeStruct((nb, 2, B_HEADS, B_DK, B_DV), F32))
    return pl.pallas_call(
        functools.partial(_gla_kernel, has_state=not prompt, want_state=prompt),
        grid=(nb, B_HEADS),
        in_specs=in_specs,
        out_specs=out_specs,
        out_shape=out_shape,
        scratch_shapes=[pltpu.VMEM((seq, B_DV), F32)],
        compiler_params=_params(("parallel", "parallel")),
        name="gla_prompt" if prompt else "gla_sample",
    )(*args)


def _mla_prep_kernel(cq_ref, ckv_ref, kr_ref, gq_ref, gkv_ref, qlat_ref, ckvk_ref):
    cq = cq_ref[...]
    qlat_ref[...] = (cq * lax.rsqrt(jnp.mean(cq * cq, axis=-1, keepdims=True) + EPS)
                     * gq_ref[...]).astype(BF16)
    ckv = ckv_ref[...]
    ckvk_ref[:, :KV_LORA] = (ckv * lax.rsqrt(jnp.mean(ckv * ckv, axis=-1, keepdims=True) + EPS)
                             * gkv_ref[...])
    ckvk_ref[:, KV_LORA:] = kr_ref[...]


def _mla_prep(z, q_gain, kv_gain):
    tm = 512
    return pl.pallas_call(
        _mla_prep_kernel,
        grid=(N_TOK // tm,),
        in_specs=[
            pl.BlockSpec((tm, Q_LORA), lambda i: (i, Z_CQ // Q_LORA)),
            pl.BlockSpec((tm, KV_LORA), lambda i: (i, Z_CKV // KV_LORA)),
            pl.BlockSpec((tm, LANES), lambda i: (i, Z_KR // LANES)),
            pl.BlockSpec((1, Q_LORA), lambda i: (0, 0)),
            pl.BlockSpec((1, KV_LORA), lambda i: (0, 0)),
        ],
        out_specs=[pl.BlockSpec((tm, Q_LORA), lambda i: (i, 0)),
                   pl.BlockSpec((tm, KV_LORA + LANES), lambda i: (i, 0))],
        out_shape=[jax.ShapeDtypeStruct((N_TOK, Q_LORA), BF16),
                   jax.ShapeDtypeStruct((N_TOK, KV_LORA + LANES), F32)],
        compiler_params=_params(("parallel",)),
        name="mla_prep",
    )(z, z, z, q_gain.reshape(1, Q_LORA), kv_gain.reshape(1, KV_LORA))


def _swap_halves(t, lane):
    width = t.shape[-1]
    return jnp.where((lane & 31) < 16, pltpu.roll(t, width - 16, 1), pltpu.roll(t, 16, 1))


def _qup_kernel(x_ref, w_ref, gn_ref, gr_ref, c_ref, s_ref, qn_ref, qr_ref):
    acc = jnp.dot(x_ref[...], w_ref[...], preferred_element_type=F32)
    nope_w = C_HEADS * QK_NOPE
    tm = acc.shape[0]
    lane = lax.broadcasted_iota(jnp.int32, (tm, LANES), 1)
    low = lane < QK_ROPE
    gn = gn_ref[...]
    gr = gr_ref[...]
    cos = c_ref[...]
    sin = s_ref[...]
    for p in range(C_HEADS // 2):
        r2 = acc[:, nope_w + LANES * p: nope_w + LANES * (p + 1)]
        sq = r2 * r2
        ss_rope = (jnp.sum(jnp.where(low, sq, 0.0), axis=-1, keepdims=True),
                   jnp.sum(jnp.where(low, 0.0, sq), axis=-1, keepdims=True))
        inv = []
        for hh in range(2):
            h = 2 * p + hh
            nh = acc[:, QK_NOPE * h: QK_NOPE * (h + 1)]
            ssn = jnp.sum(nh * nh, axis=-1, keepdims=True)
            r = lax.rsqrt((ssn + ss_rope[hh]) / QK_DIM + EPS)
            qn_ref[:, QK_NOPE * h: QK_NOPE * (h + 1)] = (nh * r * gn).astype(BF16)
            inv.append(r)
        t = r2 * jnp.where(low, inv[0], inv[1]) * gr
        qr_ref[:, LANES * p: LANES * (p + 1)] = (t * cos + _swap_halves(t, lane) * sin).astype(BF16)


def _qup(qlat, w, gn, gr, cos, sin):
    tm = 512
    n_rows = qlat.shape[0]
    row = lambda width: pl.BlockSpec((tm, width), lambda i: (i, 0))
    full = lambda a: pl.BlockSpec(a.shape, lambda i: (0, 0))
    return pl.pallas_call(
        _qup_kernel,
        grid=(n_rows // tm,),
        in_specs=[row(Q_LORA), full(w), full(gn), full(gr), row(LANES), row(LANES)],
        out_specs=[row(C_HEADS * QK_NOPE), row(C_HEADS * QK_ROPE)],
        out_shape=[jax.ShapeDtypeStruct((n_rows, C_HEADS * QK_NOPE), BF16),
                   jax.ShapeDtypeStruct((n_rows, C_HEADS * QK_ROPE), BF16)],
        compiler_params=_params(("parallel",)),
        name="mla_q_up",
    )(qlat, w, gn, gr, cos, sin)


def _kvup_kernel(x_ref, w_ref, gn_ref, gr_ref, c_ref, s_ref, kn_ref, kr_ref, v_ref):
    x = x_ref[...]
    acc = jnp.dot(x[:, :KV_LORA].astype(BF16), w_ref[...], preferred_element_type=F32)
    nope_w = C_HEADS * QK_NOPE
    v_ref[...] = acc[:, nope_w:].astype(BF16)
    tm = acc.shape[0]
    lane = lax.broadcasted_iota(jnp.int32, (tm, LANES), 1)
    low = lane < QK_ROPE
    kr2 = x[:, KV_LORA:]
    ss_rope = jnp.sum(jnp.where(low, kr2 * kr2, 0.0), axis=-1, keepdims=True)
    t = kr2 * gr_ref[...]
    rot = t * c_ref[...] + _swap_halves(t, lane) * s_ref[...]
    gn = gn_ref[...]
    for p in range(C_HEADS // 2):
        inv = []
        for hh in range(2):
            h = 2 * p + hh
            nh = acc[:, QK_NOPE * h: QK_NOPE * (h + 1)]
            ssn = jnp.sum(nh * nh, axis=-1, keepdims=True)
            r = lax.rsqrt((ssn + ss_rope) / QK_DIM + EPS)
            kn_ref[:, QK_NOPE * h: QK_NOPE * (h + 1)] = (nh * r * gn).astype(BF16)
            inv.append(r)
        kr_ref[:, LANES * p: LANES * (p + 1)] = (rot * jnp.where(low, inv[0], inv[1])).astype(BF16)


def _kvup(x, w, gn, gr, cos, sin):
    tm = 512
    n_rows = x.shape[0]
    row = lambda width: pl.BlockSpec((tm, width), lambda i: (i, 0))
    full = lambda a: pl.BlockSpec(a.shape, lambda i: (0, 0))
    return pl.pallas_call(
        _kvup_kernel,
        grid=(n_rows // tm,),
        in_specs=[row(KV_LORA + LANES), full(w), full(gn), full(gr), row(LANES), row(LANES)],
        out_specs=[row(C_HEADS * QK_NOPE), row(C_HEADS * QK_ROPE), row(C_HEADS * C_DV)],
        out_shape=[jax.ShapeDtypeStruct((n_rows, C_HEADS * QK_NOPE), BF16),
                   jax.ShapeDtypeStruct((n_rows, C_HEADS * QK_ROPE), BF16),
                   jax.ShapeDtypeStruct((n_rows, C_HEADS * C_DV), BF16)],
        compiler_params=_params(("parallel",)),
        name="mla_kv_up",
    )(x, w, gn, gr, cos, sin)


def _attn_kernel(qn_ref, qr_ref, kn_ref, kr_ref, v_ref, o_ref):
    tq = qr_ref.shape[0]
    lane = lax.broadcasted_iota(jnp.int32, (tq, LANES), 1)
    qr = qr_ref[...]
    kr = kr_ref[...]
    trans_b = (((1,), (1,)), ((), ()))
    for hh in range(2):
        cols = slice(QK_NOPE * hh, QK_NOPE * (hh + 1))
        own = (lane < QK_ROPE) if hh == 0 else (lane >= QK_ROPE)
        qrm = jnp.where(own, qr, jnp.zeros_like(qr))
        s = lax.dot_general(qn_ref[:, cols], kn_ref[:, cols], trans_b, preferred_element_type=F32)
        s = s + lax.dot_general(qrm, kr, trans_b, preferred_element_type=F32)
        s = s * (QK_DIM ** -0.5)
        e = jnp.exp(s - jnp.max(s, axis=-1, keepdims=True))
        p = (e / jnp.sum(e, axis=-1, keepdims=True)).astype(BF16)
        o_ref[:, cols] = jnp.dot(p, v_ref[:, cols], preferred_element_type=F32).astype(BF16)


def _attention(qn, qr, kn, kr, v, q_row0, lq, lk, nb, tq):
    nq = lq // tq
    qspec = lambda width: pl.BlockSpec((tq, width), lambda b, p, i: (q_row0 // tq + b * nq + i, p))
    kspec = lambda width: pl.BlockSpec((lk, width), lambda b, p, i: (b, p))
    return pl.pallas_call(
        _attn_kernel,
        grid=(nb, C_HEADS // 2, nq),
        in_specs=[qspec(2 * QK_NOPE), qspec(LANES), kspec(2 * QK_NOPE), kspec(LANES),
                  kspec(2 * C_DV)],
        out_specs=pl.BlockSpec((tq, 2 * C_DV), lambda b, p, i: (b * nq + i, p)),
        out_shape=jax.ShapeDtypeStruct((nb * lq, C_HEADS * C_DV), BF16),
        compiler_params=_params(("parallel", "parallel", "parallel")),
        name="mla_attention",
    )(qn, qr, kn, kr, v)


def _rope_tables():
    rows = DEC_SEQ // GRID_W
    row = jnp.repeat(jnp.arange(rows, dtype=F32), GRID_W)
    col = jnp.tile(jnp.arange(GRID_W, dtype=F32), rows)
    inv_freq = ROPE_BASE ** (-jnp.arange(0, ROPE_AXIS, 2, dtype=F32) / ROPE_AXIS)
    ar = row[:, None] * inv_freq
    ac = col[:, None] * inv_freq
    cos = jnp.concatenate([jnp.cos(ar), jnp.cos(ar), jnp.cos(ac), jnp.cos(ac)], axis=-1)
    sin = jnp.concatenate([-jnp.sin(ar), jnp.sin(ar), -jnp.sin(ac), jnp.sin(ac)], axis=-1)
    return jnp.tile(cos, (1, 2)), jnp.tile(sin, (1, 2))


def _layer_weights(l, w_in, gla_w_gate_up, mla_w_q_up, mla_w_kv_up, mla_q_norm, mla_k_norm):
    w = w_in[l]
    gate0 = 8224 + Q_LORA + KV_LORA + QK_ROPE
    kr = w[:, 9760:9824]
    wz = jnp.concatenate(
        [w[:, :Z_MAIN], w[:, 8224:9248], w[:, 9248:9760], kr, kr, w[:, 8192:8224],
         jnp.zeros((D_MODEL, LANES - 2 * GLA_RANK), F32)], axis=1).astype(BF16)
    wg = w[:, gate0:].astype(BF16)
    ext = jnp.zeros((2, LANES, B_HEADS * B_DK), F32)
    ext = ext.at[0, :GLA_RANK].set(gla_w_gate_up[l, 0])
    ext = ext.at[1, GLA_RANK:2 * GLA_RANK].set(gla_w_gate_up[l, 1])
    wq = mla_w_q_up[l].reshape(Q_LORA, C_HEADS, QK_DIM)
    wq = jnp.concatenate([wq[:, :, :QK_NOPE].reshape(Q_LORA, -1),
                          wq[:, :, QK_NOPE:].reshape(Q_LORA, -1)], axis=1).astype(BF16)
    wkv = mla_w_kv_up[l].reshape(KV_LORA, C_HEADS, QK_NOPE + C_DV)
    wkv = jnp.concatenate([wkv[:, :, :QK_NOPE].reshape(KV_LORA, -1),
                           wkv[:, :, QK_NOPE:].reshape(KV_LORA, -1)], axis=1).astype(BF16)
    split = lambda g: (g[:QK_NOPE].reshape(1, QK_NOPE), jnp.tile(g[QK_NOPE:], 2).reshape(1, LANES))
    return wz, wg, ext.astype(BF16), wq, wkv, split(mla_q_norm[l]), split(mla_k_norm[l])


def kernel(x_prompt, x_sample, cache_mla, state_hgrn, state_gla, c, c_ctx, norm1, w_mod, b_mod, w_in,
           hgrn_lb_logits, hgrn_out_norm, gla_w_gate_up, gla_b_gate, gla_out_norm, mla_q_a_norm,
           mla_w_q_up, mla_kv_a_norm, mla_w_kv_up, mla_q_norm, mla_k_norm, w_branch_a, w_branch_b,
           w_branch_c, w_out, norm2, w_mlp_in, w_mlp_out):
    x = jnp.concatenate([x_prompt.reshape(N_PROMPT, D_MODEL), x_sample.reshape(N_SAMPLE, D_MODEL)])
    cond = jnp.concatenate([c_ctx[None], c, jnp.zeros((MOD_ROWS - 1 - DEC_BATCH, D_MODEL), F32)])
    mod = _modulation(cond, w_mod, b_mod).reshape(DEPTH * MOD_ROWS, 1, N_MOD * D_MODEL)

    cos_s, sin_s = _rope_tables()
    ones = lambda n: jnp.ones((n, LANES), F32)
    zeros = lambda n: jnp.zeros((n, LANES), F32)
    cos_q = jnp.concatenate([ones(N_PROMPT), jnp.tile(cos_s, (DEC_BATCH, 1))])
    sin_q = jnp.concatenate([zeros(N_PROMPT), jnp.tile(sin_s, (DEC_BATCH, 1))])
    cos_ks = jnp.tile(jnp.concatenate([ones(PAST_LEN), cos_s]), (DEC_BATCH, 1))
    sin_ks = jnp.tile(jnp.concatenate([zeros(PAST_LEN), sin_s]), (DEC_BATCH, 1))

    caches, hgrn_states, gla_states = [], [], []
    for l in range(DEPTH):
        wz, wg, wlr, wq, wkv, (qgn, qgr), (kgn, kgr) = _layer_weights(
            l, w_in, gla_w_gate_up, mla_w_q_up, mla_w_kv_up, mla_q_norm, mla_k_norm)

        h = _norm_mod(x, norm1[l], mod, l, 0)
        z = _matmul(h, wz, F32, 1024, Z_TN, name="in_proj")

        oa_p, sa = _hgrn(z, hgrn_lb_logits, hgrn_out_norm[l], None, l, True)
        (oa_s,) = _hgrn(z, hgrn_lb_logits, hgrn_out_norm[l], state_hgrn, l, False)
        ob_p, sb = _gla(z, wlr, gla_b_gate[l], gla_out_norm[l], None, l, True)
        (ob_s,) = _gla(z, wlr, gla_b_gate[l], gla_out_norm[l], state_gla, l, False)
        o_a = jnp.concatenate([oa_p, oa_s])
        o_b = jnp.concatenate([ob_p, ob_s])
        hgrn_states.append(sa)
        gla_states.append(sb)

        qlat, ckvk = _mla_prep(z, mla_q_a_norm[l], mla_kv_a_norm[l])
        caches.append(ckvk[:N_PROMPT, :KV_LORA + QK_ROPE].reshape(BATCH, SEQ, KV_LORA + QK_ROPE))
        qn, qr = _qup(qlat, wq, qgn, qgr, cos_q, sin_q)
        kn_p, kr_p, v_p = _kvup(ckvk[:N_PROMPT], wkv, kgn, kgr, ones(N_PROMPT), zeros(N_PROMPT))
        ctx = cache_mla[:, l]
        ctx = jnp.concatenate([ctx, ctx[..., KV_LORA:]], axis=-1)
        kv_in = jnp.concatenate([ctx, ckvk[N_PROMPT:].reshape(DEC_BATCH, DEC_SEQ, -1)], axis=1)
        lk_s = PAST_LEN + DEC_SEQ
        kn_s, kr_s, v_s = _kvup(kv_in.reshape(DEC_BATCH * lk_s, -1), wkv, kgn, kgr, cos_ks, sin_ks)
        oc_p = _attention(qn, qr, kn_p, kr_p, v_p, 0, SEQ, SEQ, BATCH, SEQ)
        oc_s = _attention(qn, qr, kn_s, kr_s, v_s, N_PROMPT, DEC_SEQ, lk_s, DEC_BATCH, 256)
        o_c = jnp.concatenate([oc_p, oc_s])

        merged = _merge(h, wg, o_a, w_branch_a[l].astype(BF16), o_b, w_branch_b[l].astype(BF16),
                        o_c, w_branch_c[l].astype(BF16))
        x = _matmul_residual(merged, w_out[l].astype(BF16), x, mod, l, 2, "out_proj")

        h2 = _norm_mod(x, norm2[l], mod, l, 3)
        u = _matmul(h2, w_mlp_in[l].astype(BF16), BF16, 1024, 1024, relu2=True, name="mlp_in")
        x = _matmul_residual(u, w_mlp_out[l].astype(BF16), x, mod, l, 5, "mlp_out")

    y_prompt = x[:N_PROMPT].reshape(BATCH, SEQ, D_MODEL)
    y_sample = x[N_PROMPT:].reshape(DEC_BATCH, DEC_SEQ, D_MODEL)
    return (y_prompt, y_sample, jnp.stack(caches, axis=1), jnp.stack(hgrn_states, axis=1),
            jnp.stack(gla_states, axis=1))
```

```python
import functools

import numpy as np
import jax
import jax.numpy as jnp
from jax import lax
from jax.experimental import pallas as pl
from jax.experimental.pallas import tpu as pltpu

F32 = jnp.float32
BF16 = jnp.bfloat16

D_MODEL = 4096
BATCH = 32
SEQ = 256
DEPTH = 2
DEC_BATCH = 2
DEC_SEQ = 1024
PAST_LEN = 512
GRID_W = 64
A_HEADS = 8
A_DK = 128
A_DV = 128
B_HEADS = 4
B_DK = 128
B_DV = 256
GLA_RANK = 16
GLA_GATE_NORM = 16.0
C_HEADS = 16
Q_LORA = 1024
KV_LORA = 512
QK_NOPE = 128
QK_ROPE = 64
C_DV = 128
QK_DIM = QK_NOPE + QK_ROPE
ROPE_AXIS = QK_ROPE // 2
ROPE_BASE = 10000.0
N_MOD = 6
D_FF = 4 * D_MODEL
EPS = 1e-6

N_PROMPT = BATCH * SEQ
N_SAMPLE = DEC_BATCH * DEC_SEQ
N_TOK = N_PROMPT + N_SAMPLE
MOD_ROWS = 8

Z_MAIN = 8192
Z_CQ = Z_MAIN
Z_CKV = Z_CQ + Q_LORA
Z_KR = Z_CKV + KV_LORA
Z_LR = Z_KR + 128
Z_COLS = Z_LR + 128
Z_TN = 768

LANES = 128
MXU_COLS = 256
V7X_VMEM_BYTES = 64 * 1024 * 1024
VMEM_BIG = 56 * 1024 * 1024
VMEM_MID = 40 * 1024 * 1024

CH = 32
SB = 8


def _params(sem, vmem=VMEM_MID):
    return pltpu.CompilerParams(dimension_semantics=sem, vmem_limit_bytes=vmem)


def _mod_row(i, tm):
    return jnp.where(i < N_PROMPT // tm, 0, 1 + (i - N_PROMPT // tm) // (DEC_SEQ // tm))


def _sigmoid(x):
    return 1.0 / (1.0 + jnp.exp(-x))


def _log_sigmoid(x):
    return jnp.minimum(x, 0.0) - jnp.log1p(jnp.exp(-jnp.abs(x)))


def _mod_kernel(cond_ref, w_ref, b_ref, o_ref):
    c = cond_ref[...]
    s = (c * _sigmoid(c)).astype(BF16)
    o_ref[...] = jnp.dot(s, w_ref[...].astype(BF16), preferred_element_type=F32) + b_ref[...]


def _modulation(cond, w_mod, b_mod):
    tn = 512
    n = N_MOD * D_MODEL
    return pl.pallas_call(
        _mod_kernel,
        grid=(DEPTH, n // tn),
        in_specs=[
            pl.BlockSpec((MOD_ROWS, D_MODEL), lambda l, j: (0, 0)),
            pl.BlockSpec((None, D_MODEL, tn), lambda l, j: (l, 0, j)),
            pl.BlockSpec((None, 1, tn), lambda l, j: (l, 0, j)),
        ],
        out_specs=pl.BlockSpec((None, MOD_ROWS, tn), lambda l, j: (l, 0, j)),
        out_shape=jax.ShapeDtypeStruct((DEPTH, MOD_ROWS, n), F32),
        compiler_params=_params(("parallel", "parallel")),
        name="modulation",
    )(cond, w_mod, b_mod.reshape(DEPTH, 1, n))


def _norm_mod_kernel(x_ref, g_ref, sh_ref, sc_ref, o_ref):
    x = x_ref[...]
    ms = jnp.mean(x * x, axis=-1, keepdims=True)
    y = x * lax.rsqrt(ms + EPS) * g_ref[...]
    o_ref[...] = (y * (1.0 + sc_ref[...]) + sh_ref[...]).astype(BF16)


def _norm_mod(x, gain, mod, layer, shift_chunk):
    tm = 512
    base = layer * MOD_ROWS
    return pl.pallas_call(
        _norm_mod_kernel,
        grid=(N_TOK // tm,),
        in_specs=[
            pl.BlockSpec((tm, D_MODEL), lambda i: (i, 0)),
            pl.BlockSpec((1, D_MODEL), lambda i: (0, 0)),
            pl.BlockSpec((None, 1, D_MODEL), lambda i: (base + _mod_row(i, tm), 0, shift_chunk)),
            pl.BlockSpec((None, 1, D_MODEL), lambda i: (base + _mod_row(i, tm), 0, shift_chunk + 1)),
        ],
        out_specs=pl.BlockSpec((tm, D_MODEL), lambda i: (i, 0)),
        out_shape=jax.ShapeDtypeStruct((N_TOK, D_MODEL), BF16),
        compiler_params=_params(("parallel",)),
        name="norm_mod",
    )(x, gain.reshape(1, D_MODEL), mod, mod)


def _mm_kernel(x_ref, w_ref, o_ref, *, relu2):
    acc = jnp.dot(x_ref[...], w_ref[...], preferred_element_type=F32)
    if relu2:
        acc = jnp.square(jnp.maximum(acc, 0.0))
    o_ref[...] = acc.astype(o_ref.dtype)


def _matmul(x, w, out_dtype, tm, tn, relu2=False, name="matmul"):
    m, k = x.shape
    n = w.shape[1]
    return pl.pallas_call(
        functools.partial(_mm_kernel, relu2=relu2),
        grid=(n // tn, m // tm),
        in_specs=[
            pl.BlockSpec((tm, k), lambda j, i: (i, 0)),
            pl.BlockSpec((k, tn), lambda j, i: (0, j)),
        ],
        out_specs=pl.BlockSpec((tm, tn), lambda j, i: (i, j)),
        out_shape=jax.ShapeDtypeStruct((m, n), out_dtype),
        compiler_params=_params(("parallel", "parallel"), VMEM_BIG),
        name=name,
    )(x, w)


def _mm_res_kernel(x_ref, w_ref, res_ref, g_ref, o_ref, *, nk):
    if nk > 1:
        @pl.when(pl.program_id(2) == 0)
        def _():
            o_ref[...] = res_ref[...]

    x = x_ref[...]
    for c0 in range(0, o_ref.shape[1], MXU_COLS):
        cols = slice(c0, c0 + MXU_COLS)
        part = g_ref[:, cols] * jnp.dot(x, w_ref[:, cols], preferred_element_type=F32)
        if nk == 1:
            o_ref[:, cols] = res_ref[:, cols] + part
        else:
            o_ref[:, cols] += part


def _matmul_residual(a, w, res, mod, layer, gate_chunk, tn, tk, name):
    tm = 1024
    m, k = a.shape
    n = w.shape[1]
    nk = k // tk
    base = layer * MOD_ROWS
    cpc = D_MODEL // tn
    return pl.pallas_call(
        functools.partial(_mm_res_kernel, nk=nk),
        grid=(n // tn, m // tm, nk),
        in_specs=[
            pl.BlockSpec((tm, tk), lambda j, i, kk: (i, kk)),
            pl.BlockSpec((tk, tn), lambda j, i, kk: (kk, j)),
            pl.BlockSpec((tm, tn), lambda j, i, kk: (i, j)),
            pl.BlockSpec((None, 1, tn),
                         lambda j, i, kk: (base + _mod_row(i, tm), 0, gate_chunk * cpc + j)),
        ],
        out_specs=pl.BlockSpec((tm, tn), lambda j, i, kk: (i, j)),
        out_shape=jax.ShapeDtypeStruct((m, n), F32),
        compiler_params=_params(("parallel", "parallel", "arbitrary"), VMEM_BIG),
        name=name,
    )(a, w, res, mod)


def _merge_kernel(h_ref, wga_ref, wgb_ref, wgc_ref, oa_ref, wa_ref, ob_ref, wb_ref, oc_ref, wc_ref,
                  o_ref):
    h = h_ref[...]

    def branch(wg_ref, o_r, w_r):
        g = jnp.dot(h, wg_ref[...], preferred_element_type=F32)
        p = jnp.dot(o_r[...], w_r[...], preferred_element_type=F32)
        return _sigmoid(g) * p

    acc = branch(wga_ref, oa_ref, wa_ref)
    acc = acc + branch(wgb_ref, ob_ref, wb_ref)
    acc = acc + branch(wgc_ref, oc_ref, wc_ref)
    o_ref[...] = acc.astype(BF16)


def _merge(h, wg, o_a, w_a, o_b, w_b, o_c, w_c):
    tm, tn = 512, 512
    nj = D_MODEL // tn
    act = lambda width: pl.BlockSpec((tm, width), lambda j, i: (i, 0))
    wcol = lambda rows: pl.BlockSpec((rows, tn), lambda j, i: (0, j))
    gate = lambda b: pl.BlockSpec((D_MODEL, tn), lambda j, i: (0, b * nj + j))
    return pl.pallas_call(
        _merge_kernel,
        grid=(nj, N_TOK // tm),
        in_specs=[act(D_MODEL), gate(0), gate(1), gate(2),
                  act(o_a.shape[1]), wcol(w_a.shape[0]),
                  act(o_b.shape[1]), wcol(w_b.shape[0]),
                  act(o_c.shape[1]), wcol(w_c.shape[0])],
        out_specs=pl.BlockSpec((tm, tn), lambda j, i: (i, j)),
        out_shape=jax.ShapeDtypeStruct((N_TOK, D_MODEL), BF16),
        compiler_params=_params(("parallel", "parallel"), VMEM_BIG),
        name="merge",
    )(h, wg, wg, wg, o_a, w_a, o_b, w_b, o_c, w_c)


PAD = 8


def _shifted(ref, n_rows, d, reverse):
    return ref[pl.ds(PAD + d if reverse else PAD - d, n_rows), :]


def _seg_cumsum(g, rowc, reverse, c_scr):
    n_rows = g.shape[0]
    x = g
    for sh in (1, 2, 4, 8, 16):
        mask = rowc < CH - sh if reverse else rowc >= sh
        if sh < PAD:
            c_scr[pl.ds(PAD, n_rows), :] = x
            moved = _shifted(c_scr, n_rows, sh, reverse)
        else:
            moved = pltpu.roll(x, n_rows - sh if reverse else sh, 0)
        x = x + jnp.where(mask, moved, 0.0)
    return x


def _scan_dir(q, k, v, g, s0t, reverse, scratch, accumulate, want_state):
    o_scr, k_scr, c_scr, v_scr = scratch
    n_rows, dk = q.shape
    n = n_rows // CH
    rowc = lax.broadcasted_iota(jnp.int32, (n_rows, dk), 0) & (CH - 1)
    cum = _seg_cumsum(g, rowc, reverse, c_scr)
    c_scr[pl.ds(PAD, n_rows), :] = cum
    k_scr[pl.ds(PAD, n_rows), :] = k

    def rows_of(r, count):
        return jnp.broadcast_to(cum[r:r + 1, :], (count, dk))

    def chunk_row(r):
        return jnp.concatenate([rows_of(c * CH + r, CH) for c in range(n)], axis=0)

    last_row = 0 if reverse else CH - 1
    last_b = chunk_row(last_row)
    qt = (q * jnp.exp(cum)).astype(BF16)
    kh = (k * jnp.exp(last_b - cum)).astype(BF16)
    vb = v.astype(BF16)

    rows = rowc & (SB - 1)
    o_diag = jnp.sum(q * k, axis=-1, keepdims=True) * v
    for d in range(1, SB):
        mask = rows < SB - d if reverse else rows >= d
        kd = _shifted(k_scr, n_rows, d, reverse)
        cd = _shifted(c_scr, n_rows, d, reverse)
        vd = _shifted(v_scr, n_rows, d, reverse)
        e = jnp.exp(jnp.where(mask, cum - cd, -jnp.inf))
        o_diag = o_diag + jnp.sum(q * kd * e, axis=-1, keepdims=True) * vd
    if accumulate:
        o_scr[...] += o_diag
    else:
        o_scr[...] = o_diag

    sub = rowc // SB
    nsb = CH // SB
    if reverse:
        own = [rows_of(c * CH + SB * min(i + 1, nsb - 1), SB) for c in range(n) for i in range(nsb)]
        has_piv = sub < nsb - 1
    else:
        own = [rows_of(c * CH + SB * max(i, 1) - 1, SB) for c in range(n) for i in range(nsb)]
        has_piv = sub >= 1
    qe = q * jnp.exp(jnp.where(has_piv, cum - jnp.concatenate(own, axis=0), -jnp.inf))
    qs, ks = [], []
    for pr in range(SB, CH, SB):
        if reverse:
            piv, qmask, kmask = chunk_row(pr), sub == pr // SB - 1, rowc >= pr
        else:
            piv, qmask, kmask = chunk_row(pr - 1), sub == pr // SB, rowc < pr
        qs.append(jnp.where(qmask, qe, 0.0).astype(BF16))
        ks.append((k * jnp.exp(jnp.where(kmask, piv - cum, -jnp.inf))).astype(BF16))
    qcat = jnp.concatenate(qs, axis=-1)
    kcat = jnp.concatenate(ks, axis=-1)

    trans_b = (((1,), (1,)), ((), ()))
    trans_a = (((0,), (0,)), ((), ()))
    st = s0t
    order = range(n - 1, -1, -1) if reverse else range(n)
    for idx, c in enumerate(order):
        sl = slice(c * CH, (c + 1) * CH)
        o_c = lax.dot_general(qt[sl], st.astype(BF16), trans_b, preferred_element_type=F32)
        a = lax.dot_general(qcat[sl], kcat[sl], trans_b, preferred_element_type=F32)
        o_c = o_c + jnp.dot(a.astype(BF16), vb[sl], preferred_element_type=F32)
        o_scr[sl, :] += o_c
        if want_state or idx + 1 < n:
            ut = lax.dot_general(vb[sl], kh[sl], trans_a, preferred_element_type=F32)
            r = c * CH + last_row
            st = st * jnp.exp(cum[r:r + 1, :]) + ut
    return st


def _finish_scan(o_scr, gain_ref, zg_ref, o_ref):
    o = o_scr[...]
    ms = jnp.mean(o * o, axis=-1, keepdims=True)
    zg = zg_ref[...]
    o_ref[...] = (o * lax.rsqrt(ms + EPS) * gain_ref[...] * (zg * _sigmoid(zg))).astype(BF16)


def _split_scan_refs(refs, n_in, sample):
    pos = n_in
    s0_ref = None
    if sample:
        s0_ref = refs[pos]
        pos += 2
    o_ref = refs[pos]
    st_ref = None if sample else refs[pos + 1]
    scratch = refs[pos + (1 if sample else 2):]
    return refs[:n_in], s0_ref, o_ref, st_ref, scratch


def _prep_scan_scratch(scratch, v):
    _, k_scr, c_scr, v_scr = scratch
    n_rows = v.shape[0]
    for ref in (k_scr, c_scr, v_scr):
        zero = jnp.zeros((PAD, ref.shape[1]), F32)
        ref[pl.ds(0, PAD), :] = zero
        ref[pl.ds(PAD + n_rows, PAD), :] = zero
    v_scr[pl.ds(PAD, n_rows), :] = v


def _scan_scratch(seq, dk, dv):
    return [pltpu.VMEM((seq, dv), F32), pltpu.VMEM((seq + 2 * PAD, dk), F32),
            pltpu.VMEM((seq + 2 * PAD, dk), F32), pltpu.VMEM((seq + 2 * PAD, dv), F32)]


def _hgrn_kernel(*refs, layer, sample):
    ins, s0_ref, o_ref, st_ref, scratch = _split_scan_refs(refs, 7, sample)
    lbl_ref, q_ref, ff_ref, fb_ref, i_ref, zg_ref, gain_ref = ins
    has_state, want_state = sample, not sample

    logits = [lbl_ref[l] for l in range(DEPTH)]
    mx = functools.reduce(jnp.maximum, logits)
    ex = [jnp.exp(x - mx) for x in logits]
    tot = functools.reduce(lambda a, b: a + b, ex)
    probs = [e / tot for e in ex]
    cum_first = probs[0]
    cum_l = functools.reduce(lambda a, b: a + b, probs[:layer + 1])
    lb = cum_l - cum_first

    q = q_ref[...]
    v = i_ref[...]
    _prep_scan_scratch(scratch, v)
    for d, (z_ref, reverse) in enumerate(((ff_ref, False), (fb_ref, True))):
        lbd = lb[d:d + 1, :]
        zf = z_ref[...]
        a = jnp.log(lbd)
        b = jnp.log1p(-lbd) + _log_sigmoid(zf)
        delta = a - b
        log_f = jnp.where(delta != delta, a + b,
                          jnp.maximum(a, b) + jnp.log1p(jnp.exp(-jnp.abs(delta))))
        one_minus_f = (1.0 - lbd) * _sigmoid(-zf)
        if has_state:
            s0t = s0_ref[d].T
        else:
            s0t = jnp.zeros((A_DV, A_DK), F32)
        st = _scan_dir(q, one_minus_f, v, log_f, s0t, reverse, scratch, d == 1, want_state)
        if want_state:
            st_ref[d] = st.T
    _finish_scan(scratch[0], gain_ref, zg_ref, o_ref)


def _scan_call(body, name, z_specs, args, heads, dk, dv, state, layer, o_prev):
    sample = o_prev is not None
    seq, nb = (DEC_SEQ, DEC_BATCH) if sample else (SEQ, BATCH)
    row0 = N_PROMPT // seq if sample else 0
    in_specs = [spec(seq, row0) for spec in z_specs]
    args = list(args)
    aliases = {}
    if sample:
        in_specs += [pl.BlockSpec((None, None, 2, None, dk, dv), lambda b, h: (b, layer, 0, h, 0, 0)),
                     pl.BlockSpec(memory_space=pl.ANY)]
        aliases = {len(args) + 1: 0}
        args += [state, o_prev]
    out_specs = [pl.BlockSpec((seq, dv), lambda b, h: (row0 + b, h))]
    out_shape = [jax.ShapeDtypeStruct((N_TOK, heads * dv), BF16)]
    if not sample:
        out_specs.append(pl.BlockSpec((None, 2, None, dk, dv), lambda b, h: (b, 0, h, 0, 0)))
        out_shape.append(jax.ShapeDtypeStruct((nb, 2, heads, dk, dv), F32))
    return pl.pallas_call(
        functools.partial(body, sample=sample),
        grid=(nb, heads),
        in_specs=in_specs,
        out_specs=out_specs,
        out_shape=out_shape,
        input_output_aliases=aliases,
        scratch_shapes=_scan_scratch(seq, dk, dv),
        compiler_params=_params(("parallel", "parallel")),
        name=name + ("_sample" if sample else "_prompt"),
    )(*args)


def _hgrn(z, lb_logits, out_gain, state, layer, o_prev=None):
    zcol = lambda off: (lambda seq, row0: pl.BlockSpec((seq, A_DK), lambda b, h: (row0 + b, off + h)))
    fixed = lambda spec: (lambda seq, row0: spec)
    z_specs = [
        fixed(pl.BlockSpec((DEPTH, 2, A_DK), lambda b, h: (0, 0, h))),
        zcol(0), zcol(A_HEADS), zcol(2 * A_HEADS), zcol(3 * A_HEADS), zcol(4 * A_HEADS),
        fixed(pl.BlockSpec((1, A_DV), lambda b, h: (0, 0))),
    ]
    args = [lb_logits, z, z, z, z, z, out_gain.reshape(1, A_DV)]
    return _scan_call(functools.partial(_hgrn_kernel, layer=layer), "hgrn", z_specs, args,
                      A_HEADS, A_DK, A_DV, state, layer, o_prev)


def _gla_kernel(*refs, sample):
    ins, s0_ref, o_ref, st_ref, scratch = _split_scan_refs(refs, 8, sample)
    q_ref, k_ref, v_ref, zr_ref, lr_ref, wg_ref, bg_ref, gain_ref = ins
    has_state, want_state = sample, not sample

    q = q_ref[...] * (B_DK ** -0.5)
    k = k_ref[...]
    v = v_ref[...]
    _prep_scan_scratch(scratch, v)
    lr = lr_ref[...].astype(BF16)
    for d, reverse in enumerate((False, True)):
        logit = jnp.dot(lr, wg_ref[d], preferred_element_type=F32) + bg_ref[d]
        g = _log_sigmoid(logit) / GLA_GATE_NORM
        if has_state:
            s0t = s0_ref[d].T
        else:
            s0t = jnp.zeros((B_DV, B_DK), F32)
        st = _scan_dir(q, k, v, g, s0t, reverse, scratch, d == 1, want_state)
        if want_state:
            st_ref[d] = st.T
    _finish_scan(scratch[0], gain_ref, zr_ref, o_ref)


def _gla(z, w_gate_ext, b_gate, out_gain, state, layer, o_prev=None):
    zcol = lambda width, col: (lambda seq, row0: pl.BlockSpec(
        (seq, width), lambda b, h: (row0 + b, col // width + h)))
    fixed = lambda spec: (lambda seq, row0: spec)
    z_specs = [
        zcol(B_DK, 5120), zcol(B_DK, 5632), zcol(B_DV, 6144), zcol(B_DV, 7168),
        lambda seq, row0: pl.BlockSpec((seq, LANES), lambda b, h: (row0 + b, Z_LR // LANES)),
        fixed(pl.BlockSpec((2, LANES, B_DK), lambda b, h: (0, 0, h))),
        fixed(pl.BlockSpec((2, 1, B_DK), lambda b, h: (0, 0, h))),
        fixed(pl.BlockSpec((1, B_DV), lambda b, h: (0, 0))),
    ]
    args = [z, z, z, z, z, w_gate_ext, b_gate.reshape(2, 1, B_HEADS * B_DK),
            out_gain.reshape(1, B_DV)]
    return _scan_call(_gla_kernel, "gla", z_specs, args, B_HEADS, B_DK, B_DV, state, layer, o_prev)


def _mla_prep_kernel(cq_ref, ckv_ref, kr_ref, gq_ref, gkv_ref, qlat_ref, ckvk_ref):
    cq = cq_ref[...]
    qlat_ref[...] = (cq * lax.rsqrt(jnp.mean(cq * cq, axis=-1, keepdims=True) + EPS)
                     * gq_ref[...]).astype(BF16)
    ckv = ckv_ref[...]
    ckvk_ref[:, :KV_LORA] = (ckv * lax.rsqrt(jnp.mean(ckv * ckv, axis=-1, keepdims=True) + EPS)
                             * gkv_ref[...])
    ckvk_ref[:, KV_LORA:] = kr_ref[:, :QK_ROPE]


def _mla_prep(z, q_gain, kv_gain):
    tm = 512
    return pl.pallas_call(
        _mla_prep_kernel,
        grid=(N_TOK // tm,),
        in_specs=[
            pl.BlockSpec((tm, Q_LORA), lambda i: (i, Z_CQ // Q_LORA)),
            pl.BlockSpec((tm, KV_LORA), lambda i: (i, Z_CKV // KV_LORA)),
            pl.BlockSpec((tm, LANES), lambda i: (i, Z_KR // LANES)),
            pl.BlockSpec((1, Q_LORA), lambda i: (0, 0)),
            pl.BlockSpec((1, KV_LORA), lambda i: (0, 0)),
        ],
        out_specs=[pl.BlockSpec((tm, Q_LORA), lambda i: (i, 0)),
                   pl.BlockSpec((tm, KV_LORA + QK_ROPE), lambda i: (i, 0))],
        out_shape=[jax.ShapeDtypeStruct((N_TOK, Q_LORA), BF16),
                   jax.ShapeDtypeStruct((N_TOK, KV_LORA + QK_ROPE), F32)],
        compiler_params=_params(("parallel",)),
        name="mla_prep",
    )(z, z, z, q_gain.reshape(1, Q_LORA), kv_gain.reshape(1, KV_LORA))


def _swap_halves(t, lane):
    width = t.shape[-1]
    return jnp.where((lane & 31) < 16, pltpu.roll(t, width - 16, 1), pltpu.roll(t, 16, 1))


def _qup_kernel(x_ref, w_ref, gn_ref, gr_ref, c_ref, s_ref, qn_ref, qr_ref):
    acc = jnp.dot(x_ref[...], w_ref[...], preferred_element_type=F32)
    nope_w = C_HEADS * QK_NOPE
    tm = acc.shape[0]
    lane = lax.broadcasted_iota(jnp.int32, (tm, LANES), 1)
    low = lane < QK_ROPE
    gn = gn_ref[...]
    gr = gr_ref[...]
    cos = c_ref[...]
    sin = s_ref[...]
    for p in range(C_HEADS // 2):
        r2 = acc[:, nope_w + LANES * p: nope_w + LANES * (p + 1)]
        sq = r2 * r2
        ss_rope = (jnp.sum(jnp.where(low, sq, 0.0), axis=-1, keepdims=True),
                   jnp.sum(jnp.where(low, 0.0, sq), axis=-1, keepdims=True))
        inv = []
        for hh in range(2):
            h = 2 * p + hh
            nh = acc[:, QK_NOPE * h: QK_NOPE * (h + 1)]
            ssn = jnp.sum(nh * nh, axis=-1, keepdims=True)
            r = lax.rsqrt((ssn + ss_rope[hh]) / QK_DIM + EPS)
            qn_ref[:, QK_NOPE * h: QK_NOPE * (h + 1)] = (nh * r * gn).astype(BF16)
            inv.append(r)
        t = r2 * jnp.where(low, inv[0], inv[1]) * gr
        qr_ref[:, LANES * p: LANES * (p + 1)] = (t * cos + _swap_halves(t, lane) * sin).astype(BF16)


def _qup(qlat, w, gn, gr, cos, sin):
    tm = 512
    n_rows = qlat.shape[0]
    row = lambda width: pl.BlockSpec((tm, width), lambda i: (i, 0))
    full = lambda a: pl.BlockSpec(a.shape, lambda i: (0, 0))
    return pl.pallas_call(
        _qup_kernel,
        grid=(n_rows // tm,),
        in_specs=[row(Q_LORA), full(w), full(gn), full(gr), row(LANES), row(LANES)],
        out_specs=[row(C_HEADS * QK_NOPE), row(C_HEADS * QK_ROPE)],
        out_shape=[jax.ShapeDtypeStruct((n_rows, C_HEADS * QK_NOPE), BF16),
                   jax.ShapeDtypeStruct((n_rows, C_HEADS * QK_ROPE), BF16)],
        compiler_params=_params(("parallel",)),
        name="mla_q_up",
    )(qlat, w, gn, gr, cos, sin)


def _kvup_kernel(*refs, sample):
    if sample:
        cache_ref, x_ref, w_ref, gn_ref, gr_ref, c_ref, s_ref, kn_ref, kr_ref, v_ref = refs
        from_cache = pl.program_id(1) == 0
        x = jnp.where(from_cache, cache_ref[...], x_ref[...])
    else:
        x_ref, w_ref, gn_ref, gr_ref, kn_ref, kr_ref, v_ref = refs
        x = x_ref[...]
    acc = jnp.dot(x[:, :KV_LORA].astype(BF16), w_ref[...], preferred_element_type=F32)
    nope_w = C_HEADS * QK_NOPE
    v_ref[...] = acc[:, nope_w:].astype(BF16)
    tm = acc.shape[0]
    lane = lax.broadcasted_iota(jnp.int32, (tm, LANES), 1)
    low = lane < QK_ROPE
    kr = x[:, KV_LORA:]
    ss_rope = jnp.sum(kr * kr, axis=-1, keepdims=True)
    rot = jnp.concatenate([kr, kr], axis=-1) * gr_ref[...]
    if sample:
        turned = rot * c_ref[...] + _swap_halves(rot, lane) * s_ref[...]
        rot = jnp.where(from_cache, rot, turned)
    gn = gn_ref[...]
    for p in range(C_HEADS // 2):
        inv = []
        for hh in range(2):
            h = 2 * p + hh
            nh = acc[:, QK_NOPE * h: QK_NOPE * (h + 1)]
            ssn = jnp.sum(nh * nh, axis=-1, keepdims=True)
            r = lax.rsqrt((ssn + ss_rope) / QK_DIM + EPS)
            kn_ref[:, QK_NOPE * h: QK_NOPE * (h + 1)] = (nh * r * gn).astype(BF16)
            inv.append(r)
        kr_ref[:, LANES * p: LANES * (p + 1)] = (rot * jnp.where(low, inv[0], inv[1])).astype(BF16)


def _kvup(ckvk, w, gn, gr, cache=None, layer=None, cos=None, sin=None):
    tm = PAST_LEN
    sample = cache is not None
    width = KV_LORA + QK_ROPE
    if sample:
        per_seq = 1 + DEC_SEQ // tm
        grid = (DEC_BATCH, per_seq)
        row_blk = lambda b, t: b * per_seq + t
        new_blk = lambda b, t: N_PROMPT // tm + b * (per_seq - 1) + jnp.maximum(t - 1, 0)
        full = lambda a: pl.BlockSpec(a.shape, lambda b, t: (0, 0))
        table = pl.BlockSpec((tm, LANES), lambda b, t: (jnp.maximum(t - 1, 0), 0))
        in_specs = [pl.BlockSpec((None, None, tm, width), lambda b, t: (b, layer, 0, 0)),
                    pl.BlockSpec((tm, width), lambda b, t: (new_blk(b, t), 0)),
                    full(w), full(gn), full(gr), table, table]
        args = (cache, ckvk, w, gn, gr, cos, sin)
        row = lambda wd: pl.BlockSpec((tm, wd), lambda b, t: (row_blk(b, t), 0))
        n_rows = DEC_BATCH * per_seq * tm
    else:
        grid = (N_PROMPT // tm,)
        full = lambda a: pl.BlockSpec(a.shape, lambda i: (0, 0))
        in_specs = [pl.BlockSpec((tm, width), lambda i: (i, 0)), full(w), full(gn), full(gr)]
        args = (ckvk, w, gn, gr)
        row = lambda wd: pl.BlockSpec((tm, wd), lambda i: (i, 0))
        n_rows = N_PROMPT
    return pl.pallas_call(
        functools.partial(_kvup_kernel, sample=sample),
        grid=grid,
        in_specs=in_specs,
        out_specs=[row(C_HEADS * QK_NOPE), row(C_HEADS * QK_ROPE), row(C_HEADS * C_DV)],
        out_shape=[jax.ShapeDtypeStruct((n_rows, C_HEADS * QK_NOPE), BF16),
                   jax.ShapeDtypeStruct((n_rows, C_HEADS * QK_ROPE), BF16),
                   jax.ShapeDtypeStruct((n_rows, C_HEADS * C_DV), BF16)],
        compiler_params=_params(("parallel",) * len(grid)),
        name="mla_kv_up_sample" if sample else "mla_kv_up_prompt",
    )(*args)


def _attn_kernel(qn_ref, qr_ref, kn_ref, kr_ref, v_ref, o_ref):
    tq = qr_ref.shape[0]
    lane = lax.broadcasted_iota(jnp.int32, (tq, LANES), 1)
    qr = qr_ref[...]
    kr = kr_ref[...]
    trans_b = (((1,), (1,)), ((), ()))
    for hh in range(2):
        cols = slice(QK_NOPE * hh, QK_NOPE * (hh + 1))
        own = (lane < QK_ROPE) if hh == 0 else (lane >= QK_ROPE)
        qrm = jnp.where(own, qr, jnp.zeros_like(qr))
        s = lax.dot_general(qn_ref[:, cols], kn_ref[:, cols], trans_b, preferred_element_type=F32)
        s = s + lax.dot_general(qrm, kr, trans_b, preferred_element_type=F32)
        s = s * (QK_DIM ** -0.5)
        e = jnp.exp(s - jnp.max(s, axis=-1, keepdims=True))
        p = (e / jnp.sum(e, axis=-1, keepdims=True)).astype(BF16)
        o_ref[:, cols] = jnp.dot(p, v_ref[:, cols], preferred_element_type=F32).astype(BF16)


def _attn_kernel_inplace(qn_ref, qr_ref, kn_ref, kr_ref, v_ref, prev_ref, o_ref):
    del prev_ref
    _attn_kernel(qn_ref, qr_ref, kn_ref, kr_ref, v_ref, o_ref)


def _attention(qn, qr, kn, kr, v, o_prev=None):
    sample = o_prev is not None
    tq = SEQ
    q_row0, lq, lk, nb = ((N_PROMPT, DEC_SEQ, PAST_LEN + DEC_SEQ, DEC_BATCH) if sample
                          else (0, SEQ, SEQ, BATCH))
    nq = lq // tq
    qspec = lambda width: pl.BlockSpec((tq, width), lambda b, p, i: (q_row0 // tq + b * nq + i, p))
    kspec = lambda width: pl.BlockSpec((lk, width), lambda b, p, i: (b, p))
    in_specs = [qspec(2 * QK_NOPE), qspec(LANES), kspec(2 * QK_NOPE), kspec(LANES), kspec(2 * C_DV)]
    args = [qn, qr, kn, kr, v]
    if sample:
        in_specs.append(pl.BlockSpec(memory_space=pl.ANY))
        args.append(o_prev)
    return pl.pallas_call(
        _attn_kernel_inplace if sample else _attn_kernel,
        grid=(nb, C_HEADS // 2, nq),
        in_specs=in_specs,
        out_specs=qspec(2 * C_DV),
        out_shape=jax.ShapeDtypeStruct((N_TOK, C_HEADS * C_DV), BF16),
        input_output_aliases={5: 0} if sample else {},
        compiler_params=_params(("parallel", "parallel", "parallel")),
        name="mla_attention_sample" if sample else "mla_attention_prompt",
    )(*args)


def _rope_tables():
    rows = DEC_SEQ // GRID_W
    row = jnp.repeat(jnp.arange(rows, dtype=F32), GRID_W)
    col = jnp.tile(jnp.arange(GRID_W, dtype=F32), rows)
    inv_freq = ROPE_BASE ** (-jnp.arange(0, ROPE_AXIS, 2, dtype=F32) / ROPE_AXIS)
    ar = row[:, None] * inv_freq
    ac = col[:, None] * inv_freq
    cos = jnp.concatenate([jnp.cos(ar), jnp.cos(ar), jnp.cos(ac), jnp.cos(ac)], axis=-1)
    sin = jnp.concatenate([-jnp.sin(ar), jnp.sin(ar), -jnp.sin(ac), jnp.sin(ac)], axis=-1)
    return jnp.tile(cos, (1, 2)), jnp.tile(sin, (1, 2))


def _layer_weights(l, w_in, gla_w_gate_up, mla_w_q_up, mla_w_kv_up, mla_q_norm, mla_k_norm):
    w = w_in[l]
    gate0 = 8224 + Q_LORA + KV_LORA + QK_ROPE
    kr = w[:, 9760:9824]
    wz = jnp.concatenate(
        [w[:, :Z_MAIN], w[:, 8224:9248], w[:, 9248:9760], kr, kr, w[:, 8192:8224],
         jnp.zeros((D_MODEL, LANES - 2 * GLA_RANK), F32)], axis=1).astype(BF16)
    wg = w[:, gate0:].astype(BF16)
    ext = jnp.zeros((2, LANES, B_HEADS * B_DK), F32)
    ext = ext.at[0, :GLA_RANK].set(gla_w_gate_up[l, 0])
    ext = ext.at[1, GLA_RANK:2 * GLA_RANK].set(gla_w_gate_up[l, 1])
    wq = mla_w_q_up[l].reshape(Q_LORA, C_HEADS, QK_DIM)
    wq = jnp.concatenate([wq[:, :, :QK_NOPE].reshape(Q_LORA, -1),
                          wq[:, :, QK_NOPE:].reshape(Q_LORA, -1)], axis=1).astype(BF16)
    wkv = mla_w_kv_up[l].reshape(KV_LORA, C_HEADS, QK_NOPE + C_DV)
    wkv = jnp.concatenate([wkv[:, :, :QK_NOPE].reshape(KV_LORA, -1),
                           wkv[:, :, QK_NOPE:].reshape(KV_LORA, -1)], axis=1).astype(BF16)
    split = lambda g: (g[:QK_NOPE].reshape(1, QK_NOPE), jnp.tile(g[QK_NOPE:], 2).reshape(1, LANES))
    return wz, wg, ext.astype(BF16), wq, wkv, split(mla_q_norm[l]), split(mla_k_norm[l])


def kernel(x_prompt, x_sample, cache_mla, state_hgrn, state_gla, c, c_ctx, norm1, w_mod, b_mod, w_in,
           hgrn_lb_logits, hgrn_out_norm, gla_w_gate_up, gla_b_gate, gla_out_norm, mla_q_a_norm,
           mla_w_q_up, mla_kv_a_norm, mla_w_kv_up, mla_q_norm, mla_k_norm, w_branch_a, w_branch_b,
           w_branch_c, w_out, norm2, w_mlp_in, w_mlp_out):
    x = jnp.concatenate([x_prompt.reshape(N_PROMPT, D_MODEL), x_sample.reshape(N_SAMPLE, D_MODEL)])
    cond = jnp.concatenate([c_ctx[None], c, jnp.zeros((MOD_ROWS - 1 - DEC_BATCH, D_MODEL), F32)])
    mod = _modulation(cond, w_mod, b_mod).reshape(DEPTH * MOD_ROWS, 1, N_MOD * D_MODEL)

    cos_s, sin_s = _rope_tables()
    cos_q = jnp.concatenate([jnp.ones((N_PROMPT, LANES), F32), jnp.tile(cos_s, (DEC_BATCH, 1))])
    sin_q = jnp.concatenate([jnp.zeros((N_PROMPT, LANES), F32), jnp.tile(sin_s, (DEC_BATCH, 1))])

    caches, hgrn_states, gla_states = [], [], []
    for l in range(DEPTH):
        wz, wg, wlr, wq, wkv, (qgn, qgr), (kgn, kgr) = _layer_weights(
            l, w_in, gla_w_gate_up, mla_w_q_up, mla_w_kv_up, mla_q_norm, mla_k_norm)

        h = _norm_mod(x, norm1[l], mod, l, 0)
        z = _matmul(h, wz, F32, 1024, Z_TN, name="in_proj")

        o_a, sa = _hgrn(z, hgrn_lb_logits, hgrn_out_norm[l], None, l)
        (o_a,) = _hgrn(z, hgrn_lb_logits, hgrn_out_norm[l], state_hgrn, l, o_prev=o_a)
        o_b, sb = _gla(z, wlr, gla_b_gate[l], gla_out_norm[l], None, l)
        (o_b,) = _gla(z, wlr, gla_b_gate[l], gla_out_norm[l], state_gla, l, o_prev=o_b)
        hgrn_states.append(sa)
        gla_states.append(sb)

        qlat, ckvk = _mla_prep(z, mla_q_a_norm[l], mla_kv_a_norm[l])
        caches.append(ckvk[:N_PROMPT].reshape(BATCH, SEQ, KV_LORA + QK_ROPE))
        qn, qr = _qup(qlat, wq, qgn, qgr, cos_q, sin_q)
        kn_p, kr_p, v_p = _kvup(ckvk, wkv, kgn, kgr)
        kn_s, kr_s, v_s = _kvup(ckvk, wkv, kgn, kgr, cache_mla, l, cos_s, sin_s)
        o_c = _attention(qn, qr, kn_p, kr_p, v_p)
        o_c = _attention(qn, qr, kn_s, kr_s, v_s, o_prev=o_c)

        merged = _merge(h, wg, o_a, w_branch_a[l].astype(BF16), o_b, w_branch_b[l].astype(BF16),
                        o_c, w_branch_c[l].astype(BF16))
        x = _matmul_residual(merged, w_out[l].astype(BF16), x, mod, l, 2, 512, D_MODEL, "out_proj")

        h2 = _norm_mod(x, norm2[l], mod, l, 3)
        u = _matmul(h2, w_mlp_in[l].astype(BF16), BF16, 1024, 1024, relu2=True, name="mlp_in")
        x = _matmul_residual(u, w_mlp_out[l].astype(BF16), x, mod, l, 5, 1024, 2048, "mlp_out")

    y_prompt = x[:N_PROMPT].reshape(BATCH, SEQ, D_MODEL)
    y_sample = x[N_PROMPT:].reshape(DEC_BATCH, DEC_SEQ, D_MODEL)
    return (y_prompt, y_sample, jnp.stack(caches, axis=1), jnp.stack(hgrn_states, axis=1),
            jnp.stack(gla_states, axis=1))
```

```python
import functools

import numpy as np
import jax
import jax.numpy as jnp
from jax import lax
from jax.experimental import pallas as pl
from jax.experimental.pallas import tpu as pltpu

F32 = jnp.float32
BF16 = jnp.bfloat16

D_MODEL = 4096
BATCH = 32
SEQ = 256
DEPTH = 2
DEC_BATCH = 2
DEC_SEQ = 1024
PAST_LEN = 512
GRID_W = 64
A_HEADS = 8
A_DK = 128
A_DV = 128
B_HEADS = 4
B_DK = 128
B_DV = 256
GLA_RANK = 16
GLA_GATE_NORM = 16.0
C_HEADS = 16
Q_LORA = 1024
KV_LORA = 512
QK_NOPE = 128
QK_ROPE = 64
C_DV = 128
QK_DIM = QK_NOPE + QK_ROPE
ROPE_AXIS = QK_ROPE // 2
ROPE_BASE = 10000.0
N_MOD = 6
D_FF = 4 * D_MODEL
EPS = 1e-6

N_PROMPT = BATCH * SEQ
N_SAMPLE = DEC_BATCH * DEC_SEQ
N_TOK = N_PROMPT + N_SAMPLE
MOD_ROWS = 8

Z_MAIN = 8192
ZS_CQ = 0
ZS_CKV = ZS_CQ + Q_LORA
ZS_KR = ZS_CKV + KV_LORA
ZS_LR = ZS_KR + 128
ZS_COLS = ZS_LR + 128
GATE_PAD = 2048

LANES = 128
MXU_COLS = 256
V7X_VMEM_BYTES = 64 * 1024 * 1024
VMEM_BIG = 60 * 1024 * 1024
VMEM_MID = 40 * 1024 * 1024

CH = 32
SB = 8


def _params(sem, vmem=VMEM_MID):
    return pltpu.CompilerParams(dimension_semantics=sem, vmem_limit_bytes=vmem)


def _mod_row(i, tm):
    return jnp.where(i < N_PROMPT // tm, 0, 1 + (i - N_PROMPT // tm) // (DEC_SEQ // tm))


def _sigmoid(x):
    return 1.0 / (1.0 + jnp.exp(-x))


def _log_sigmoid(x):
    return jnp.minimum(x, 0.0) - jnp.log1p(jnp.exp(-jnp.abs(x)))


def _mod_kernel(cond_ref, w_ref, b_ref, o_ref):
    c = cond_ref[...]
    s = (c * _sigmoid(c)).astype(BF16)
    o_ref[...] = jnp.dot(s, w_ref[...].astype(BF16), preferred_element_type=F32) + b_ref[...]


def _modulation(cond, w_mod, b_mod):
    tn = 512
    n = N_MOD * D_MODEL
    return pl.pallas_call(
        _mod_kernel,
        grid=(DEPTH, n // tn),
        in_specs=[
            pl.BlockSpec((MOD_ROWS, D_MODEL), lambda l, j: (0, 0)),
            pl.BlockSpec((None, D_MODEL, tn), lambda l, j: (l, 0, j)),
            pl.BlockSpec((None, 1, tn), lambda l, j: (l, 0, j)),
        ],
        out_specs=pl.BlockSpec((None, MOD_ROWS, tn), lambda l, j: (l, 0, j)),
        out_shape=jax.ShapeDtypeStruct((DEPTH, MOD_ROWS, n), F32),
        compiler_params=_params(("parallel", "parallel")),
        name="modulation",
    )(cond, w_mod, b_mod.reshape(DEPTH, 1, n))


def _group_specs(tm, tn, row_of, col_of):
    npb = N_PROMPT // tm
    last = npb - 1
    return [pl.BlockSpec((tm, tn), lambda *g: (jnp.minimum(row_of(*g), last), col_of(*g))),
            pl.BlockSpec((tm, tn), lambda *g: (jnp.maximum(row_of(*g) - npb, 0), col_of(*g)))]


def _norm_mod_kernel(*refs):
    *x_refs, g_ref, sh_ref, sc_ref, o_ref = refs

    def run(x_ref):
        x = x_ref[...]
        ms = jnp.mean(x * x, axis=-1, keepdims=True)
        y = x * lax.rsqrt(ms + EPS) * g_ref[...]
        o_ref[...] = (y * (1.0 + sc_ref[...]) + sh_ref[...]).astype(BF16)

    if len(x_refs) == 2:
        in_prompt = pl.program_id(0) < N_PROMPT // o_ref.shape[0]
        pl.when(in_prompt)(lambda: run(x_refs[0]))
        pl.when(jnp.logical_not(in_prompt))(lambda: run(x_refs[1]))
    else:
        run(x_refs[0])


def _norm_mod(x, gain, mod, layer, shift_chunk):
    tm = 256
    base = layer * MOD_ROWS
    if isinstance(x, tuple):
        x_specs = _group_specs(tm, D_MODEL, lambda i: i, lambda i: 0)
    else:
        x_specs, x = [pl.BlockSpec((tm, D_MODEL), lambda i: (i, 0))], (x,)
    return pl.pallas_call(
        _norm_mod_kernel,
        grid=(N_TOK // tm,),
        in_specs=x_specs + [
            pl.BlockSpec((1, D_MODEL), lambda i: (0, 0)),
            pl.BlockSpec((None, 1, D_MODEL), lambda i: (base + _mod_row(i, tm), 0, shift_chunk)),
            pl.BlockSpec((None, 1, D_MODEL), lambda i: (base + _mod_row(i, tm), 0, shift_chunk + 1)),
        ],
        out_specs=pl.BlockSpec((tm, D_MODEL), lambda i: (i, 0)),
        out_shape=jax.ShapeDtypeStruct((N_TOK, D_MODEL), BF16),
        compiler_params=_params(("parallel",)),
        name="norm_mod",
    )(*x, gain.reshape(1, D_MODEL), mod, mod)


def _cast_weight(w_ref, wb_ref):
    @pl.when(pl.program_id(1) == 0)
    def _():
        wb_ref[...] = w_ref[...].astype(BF16)
    return wb_ref


def _mm_kernel(x_ref, w_ref, o_ref, *scratch, relu2):
    if scratch:
        w_ref = _cast_weight(w_ref, scratch[0])
    acc = jnp.dot(x_ref[...], w_ref[...], preferred_element_type=F32)
    if relu2:
        acc = jnp.square(jnp.maximum(acc, 0.0))
    o_ref[...] = acc.astype(o_ref.dtype)


def _matmul(x, w, out_dtype, tm, tn, relu2=False, layer=None, n_cols=None, name="matmul"):
    m, k = x.shape
    stacked = layer is not None
    n = n_cols if stacked else w.shape[1]
    if stacked:
        w_spec = pl.BlockSpec((None, k, tn), lambda j, i: (layer, 0, j))
    else:
        w_spec = pl.BlockSpec((k, tn), lambda j, i: (0, j))
    return pl.pallas_call(
        functools.partial(_mm_kernel, relu2=relu2),
        grid=(n // tn, m // tm),
        in_specs=[pl.BlockSpec((tm, k), lambda j, i: (i, 0)), w_spec],
        out_specs=pl.BlockSpec((tm, tn), lambda j, i: (i, j)),
        out_shape=jax.ShapeDtypeStruct((m, n), out_dtype),
        scratch_shapes=[pltpu.VMEM((k, tn), BF16)] if stacked else [],
        compiler_params=_params(("parallel", "arbitrary"), VMEM_BIG),
        name=name,
    )(x, w)


def _mm_res_kernel(*refs, n_res, n_out):
    x_ref, w_ref = refs[:2]
    res_refs = refs[2:2 + n_res]
    g_ref = refs[2 + n_res]
    o_refs = refs[3 + n_res:3 + n_res + n_out]
    w_ref = _cast_weight(w_ref, refs[3 + n_res + n_out])
    tm, tn = o_refs[0].shape
    in_prompt = pl.program_id(1) < N_PROMPT // tm
    acc_ref = refs[-1] if n_out == 2 else o_refs[0]
    x = x_ref[...]
    for c0 in range(0, tn, MXU_COLS):
        cols = slice(c0, c0 + MXU_COLS)
        part = g_ref[:, cols] * jnp.dot(x, w_ref[:, cols], preferred_element_type=F32)
        if n_res == 2:
            res = jnp.where(in_prompt, res_refs[0][:, cols], res_refs[1][:, cols])
        else:
            res = res_refs[0][:, cols]
        acc_ref[:, cols] = res + part
    if n_out == 2:
        @pl.when(in_prompt)
        def _():
            o_refs[0][...] = acc_ref[...]

        @pl.when(jnp.logical_not(in_prompt))
        def _():
            o_refs[1][...] = acc_ref[...]


def _matmul_residual(a, w, res, mod, layer, gate_chunk, k_block, name, split_out=False):
    tm, tn, tk = (512 if split_out else 1024), 512, D_MODEL
    m = a.shape[0]
    n = w.shape[2]
    base = layer * MOD_ROWS
    cpc = D_MODEL // tn
    whole = pl.BlockSpec((tm, tn), lambda j, i: (i, j))
    pair = _group_specs(tm, tn, lambda j, i: i, lambda j, i: j)
    res = res if isinstance(res, tuple) else (res,)
    if split_out:
        out_specs = pair
        out_shape = [jax.ShapeDtypeStruct((N_PROMPT, n), F32), jax.ShapeDtypeStruct((N_SAMPLE, n), F32)]
    else:
        out_specs, out_shape = whole, jax.ShapeDtypeStruct((m, n), F32)
    return pl.pallas_call(
        functools.partial(_mm_res_kernel, n_res=len(res), n_out=2 if split_out else 1),
        grid=(n // tn, m // tm),
        in_specs=[
            pl.BlockSpec((tm, tk), lambda j, i: (i, k_block)),
            pl.BlockSpec((None, tk, tn), lambda j, i: (layer, k_block, j)),
            *(pair if len(res) == 2 else [whole]),
            pl.BlockSpec((None, 1, tn),
                         lambda j, i: (base + _mod_row(i, tm), 0, gate_chunk * cpc + j)),
        ],
        out_specs=out_specs,
        out_shape=out_shape,
        scratch_shapes=[pltpu.VMEM((tk, tn), BF16)] + ([pltpu.VMEM((tm, tn), F32)] if split_out else []),
        compiler_params=_params(("parallel", "arbitrary"), VMEM_BIG),
        name=name,
    )(a, w, *res, mod)


def _merge_kernel(h_ref, wga_ref, wgb_ref, wgc_ref, oa_ref, wa_ref, ob_ref, wb_ref, oc_ref, wc_ref,
                  o_ref):
    h = h_ref[...]

    def branch(wg_ref, o_r, w_r):
        g = jnp.dot(h, wg_ref[...], preferred_element_type=F32)
        p = jnp.dot(o_r[...], w_r[...], preferred_element_type=F32)
        return _sigmoid(g) * p

    acc = branch(wga_ref, oa_ref, wa_ref)
    acc = acc + branch(wgb_ref, ob_ref, wb_ref)
    acc = acc + branch(wgc_ref, oc_ref, wc_ref)
    o_ref[...] = acc.astype(BF16)


def _merge(h, wg, o_a, w_a, o_b, w_b, o_c, w_c, layer):
    tm, tn = 512, 512
    nj = D_MODEL // tn
    act = lambda width: pl.BlockSpec((tm, width), lambda j, i: (i, 0))
    wcol = lambda rows: pl.BlockSpec((None, rows, tn), lambda j, i: (layer, 0, j))
    gate = lambda b: pl.BlockSpec((D_MODEL, tn), lambda j, i: (0, GATE_PAD // tn + b * nj + j))
    return pl.pallas_call(
        _merge_kernel,
        grid=(nj, N_TOK // tm),
        in_specs=[act(D_MODEL), gate(0), gate(1), gate(2),
                  act(o_a.shape[1]), wcol(w_a.shape[1]),
                  act(o_b.shape[1]), wcol(w_b.shape[1]),
                  act(o_c.shape[1]), wcol(w_c.shape[1])],
        out_specs=pl.BlockSpec((tm, tn), lambda j, i: (i, j)),
        out_shape=jax.ShapeDtypeStruct((N_TOK, D_MODEL), BF16),
        compiler_params=_params(("parallel", "parallel"), VMEM_BIG),
        name="merge",
    )(h, wg, wg, wg, o_a, w_a, o_b, w_b, o_c, w_c)


PAD = 8


def _shifted(ref, n_rows, d, reverse):
    return ref[pl.ds(PAD + d if reverse else PAD - d, n_rows), :]


def _seg_cumsum(g, rowc, reverse, c_scr):
    n_rows = g.shape[0]
    x = g
    for sh in (1, 2, 4, 8, 16):
        mask = rowc < CH - sh if reverse else rowc >= sh
        if sh < PAD:
            c_scr[pl.ds(PAD, n_rows), :] = x
            moved = _shifted(c_scr, n_rows, sh, reverse)
        else:
            moved = pltpu.roll(x, n_rows - sh if reverse else sh, 0)
        x = x + jnp.where(mask, moved, 0.0)
    return x


def _scan_dir(q, k, v, g, s0t, reverse, scratch, accumulate, want_state):
    o_scr, k_scr, c_scr, v_scr = scratch
    n_rows, dk = q.shape
    n = n_rows // CH
    rowc = lax.broadcasted_iota(jnp.int32, (n_rows, dk), 0) & (CH - 1)
    cum = _seg_cumsum(g, rowc, reverse, c_scr)
    c_scr[pl.ds(PAD, n_rows), :] = cum
    k_scr[pl.ds(PAD, n_rows), :] = k

    def rows_of(r, count):
        return jnp.broadcast_to(cum[r:r + 1, :], (count, dk))

    def chunk_row(r):
        return jnp.concatenate([rows_of(c * CH + r, CH) for c in range(n)], axis=0)

    last_row = 0 if reverse else CH - 1
    last_b = chunk_row(last_row)
    qt = (q * jnp.exp(cum)).astype(BF16)
    kh = (k * jnp.exp(last_b - cum)).astype(BF16)
    vb = v.astype(BF16)

    rows = rowc & (SB - 1)
    o_diag = jnp.sum(q * k, axis=-1, keepdims=True) * v
    for d in range(1, SB):
        mask = rows < SB - d if reverse else rows >= d
        kd = _shifted(k_scr, n_rows, d, reverse)
        cd = _shifted(c_scr, n_rows, d, reverse)
        vd = _shifted(v_scr, n_rows, d, reverse)
        e = jnp.exp(jnp.where(mask, cum - cd, -jnp.inf))
        o_diag = o_diag + jnp.sum(q * kd * e, axis=-1, keepdims=True) * vd
    if accumulate:
        o_scr[...] += o_diag
    else:
        o_scr[...] = o_diag

    sub = rowc // SB
    nsb = CH // SB
    if reverse:
        own = [rows_of(c * CH + SB * min(i + 1, nsb - 1), SB) for c in range(n) for i in range(nsb)]
        has_piv = sub < nsb - 1
    else:
        own = [rows_of(c * CH + SB * max(i, 1) - 1, SB) for c in range(n) for i in range(nsb)]
        has_piv = sub >= 1
    qe = q * jnp.exp(jnp.where(has_piv, cum - jnp.concatenate(own, axis=0), -jnp.inf))
    qs, ks = [], []
    for pr in range(SB, CH, SB):
        if reverse:
            piv, qmask, kmask = chunk_row(pr), sub == pr // SB - 1, rowc >= pr
        else:
            piv, qmask, kmask = chunk_row(pr - 1), sub == pr // SB, rowc < pr
        qs.append(jnp.where(qmask, qe, 0.0).astype(BF16))
        ks.append((k * jnp.exp(jnp.where(kmask, piv - cum, -jnp.inf))).astype(BF16))
    qcat = jnp.concatenate(qs, axis=-1)
    kcat = jnp.concatenate(ks, axis=-1)

    trans_b = (((1,), (1,)), ((), ()))
    trans_a = (((0,), (0,)), ((), ()))
    st = s0t
    order = range(n - 1, -1, -1) if reverse else range(n)
    for idx, c in enumerate(order):
        sl = slice(c * CH, (c + 1) * CH)
        o_c = lax.dot_general(qt[sl], st.astype(BF16), trans_b, preferred_element_type=F32)
        a = lax.dot_general(qcat[sl], kcat[sl], trans_b, preferred_element_type=F32)
        o_c = o_c + jnp.dot(a.astype(BF16), vb[sl], preferred_element_type=F32)
        o_scr[sl, :] += o_c
        if want_state or idx + 1 < n:
            ut = lax.dot_general(vb[sl], kh[sl], trans_a, preferred_element_type=F32)
            r = c * CH + last_row
            st = st * jnp.exp(cum[r:r + 1, :]) + ut
    return st


def _finish_scan(o_scr, gain_ref, zg_ref, o_ref):
    o = o_scr[...]
    ms = jnp.mean(o * o, axis=-1, keepdims=True)
    zg = zg_ref[...]
    o_ref[...] = (o * lax.rsqrt(ms + EPS) * gain_ref[...] * (zg * _sigmoid(zg))).astype(BF16)


def _split_scan_refs(refs, n_in, sample):
    pos = n_in
    s0_ref = None
    if sample:
        s0_ref = refs[pos]
        pos += 2
    o_ref = refs[pos]
    st_ref = None if sample else refs[pos + 1]
    scratch = refs[pos + (1 if sample else 2):]
    return refs[:n_in], s0_ref, o_ref, st_ref, scratch


def _prep_scan_scratch(scratch, v):
    _, k_scr, c_scr, v_scr = scratch
    n_rows = v.shape[0]
    for ref in (k_scr, c_scr, v_scr):
        zero = jnp.zeros((PAD, ref.shape[1]), F32)
        ref[pl.ds(0, PAD), :] = zero
        ref[pl.ds(PAD + n_rows, PAD), :] = zero
    v_scr[pl.ds(PAD, n_rows), :] = v


def _scan_scratch(seq, dk, dv):
    return [pltpu.VMEM((seq, dv), F32), pltpu.VMEM((seq + 2 * PAD, dk), F32),
            pltpu.VMEM((seq + 2 * PAD, dk), F32), pltpu.VMEM((seq + 2 * PAD, dv), F32)]


def _hgrn_kernel(*refs, layer, sample):
    ins, s0_ref, o_ref, st_ref, scratch = _split_scan_refs(refs, 7, sample)
    lbl_ref, q_ref, ff_ref, fb_ref, i_ref, zg_ref, gain_ref = ins
    has_state, want_state = sample, not sample

    logits = [lbl_ref[l] for l in range(DEPTH)]
    mx = functools.reduce(jnp.maximum, logits)
    ex = [jnp.exp(x - mx) for x in logits]
    tot = functools.reduce(lambda a, b: a + b, ex)
    probs = [e / tot for e in ex]
    cum_first = probs[0]
    cum_l = functools.reduce(lambda a, b: a + b, probs[:layer + 1])
    lb = cum_l - cum_first

    q = q_ref[...]
    v = i_ref[...]
    _prep_scan_scratch(scratch, v)
    for d, (z_ref, reverse) in enumerate(((ff_ref, False), (fb_ref, True))):
        lbd = lb[d:d + 1, :]
        zf = z_ref[...]
        a = jnp.log(lbd)
        b = jnp.log1p(-lbd) + _log_sigmoid(zf)
        delta = a - b
        log_f = jnp.where(delta != delta, a + b,
                          jnp.maximum(a, b) + jnp.log1p(jnp.exp(-jnp.abs(delta))))
        one_minus_f = (1.0 - lbd) * _sigmoid(-zf)
        if has_state:
            s0t = s0_ref[d].T
        else:
            s0t = jnp.zeros((A_DV, A_DK), F32)
        st = _scan_dir(q, one_minus_f, v, log_f, s0t, reverse, scratch, d == 1, want_state)
        if want_state:
            st_ref[d] = st.T
    _finish_scan(scratch[0], gain_ref, zg_ref, o_ref)


def _scan_call(body, name, z_specs, args, heads, dk, dv, state, layer, o_prev):
    sample = o_prev is not None
    seq, nb = (DEC_SEQ, DEC_BATCH) if sample else (SEQ, BATCH)
    row0 = N_PROMPT // seq if sample else 0
    in_specs = [spec(seq, row0) for spec in z_specs]
    args = list(args)
    aliases = {}
    if sample:
        in_specs += [pl.BlockSpec((None, None, 2, None, dk, dv), lambda b, h: (b, layer, 0, h, 0, 0)),
                     pl.BlockSpec(memory_space=pl.ANY)]
        aliases = {len(args) + 1: 0}
        args += [state, o_prev]
    out_specs = [pl.BlockSpec((seq, dv), lambda b, h: (row0 + b, h))]
    out_shape = [jax.ShapeDtypeStruct((N_TOK, heads * dv), BF16)]
    if not sample:
        out_specs.append(pl.BlockSpec((None, 2, None, dk, dv), lambda b, h: (b, 0, h, 0, 0)))
        out_shape.append(jax.ShapeDtypeStruct((nb, 2, heads, dk, dv), F32))
    return pl.pallas_call(
        functools.partial(body, sample=sample),
        grid=(nb, heads),
        in_specs=in_specs,
        out_specs=out_specs,
        out_shape=out_shape,
        input_output_aliases=aliases,
        scratch_shapes=_scan_scratch(seq, dk, dv),
        compiler_params=_params(("parallel", "parallel")),
        name=name + ("_sample" if sample else "_prompt"),
    )(*args)


def _hgrn(z, lb_logits, out_gain, state, layer, o_prev=None):
    zcol = lambda off: (lambda seq, row0: pl.BlockSpec((seq, A_DK), lambda b, h: (row0 + b, off + h)))
    fixed = lambda spec: (lambda seq, row0: spec)
    z_specs = [
        fixed(pl.BlockSpec((DEPTH, 2, A_DK), lambda b, h: (0, 0, h))),
        zcol(0), zcol(A_HEADS), zcol(2 * A_HEADS), zcol(3 * A_HEADS), zcol(4 * A_HEADS),
        fixed(pl.BlockSpec((1, A_DV), lambda b, h: (0, 0))),
    ]
    args = [lb_logits, z, z, z, z, z, out_gain.reshape(1, A_DV)]
    return _scan_call(functools.partial(_hgrn_kernel, layer=layer), "hgrn", z_specs, args,
                      A_HEADS, A_DK, A_DV, state, layer, o_prev)


def _gla_kernel(*refs, sample):
    ins, s0_ref, o_ref, st_ref, scratch = _split_scan_refs(refs, 8, sample)
    q_ref, k_ref, v_ref, zr_ref, lr_ref, wg_ref, bg_ref, gain_ref = ins
    has_state, want_state = sample, not sample

    q = q_ref[...] * (B_DK ** -0.5)
    k = k_ref[...]
    v = v_ref[...]
    _prep_scan_scratch(scratch, v)
    lr = lr_ref[...].astype(BF16)
    for d, reverse in enumerate((False, True)):
        logit = jnp.dot(lr, wg_ref[d], preferred_element_type=F32) + bg_ref[d]
        g = _log_sigmoid(logit) / GLA_GATE_NORM
        if has_state:
            s0t = s0_ref[d].T
        else:
            s0t = jnp.zeros((B_DV, B_DK), F32)
        st = _scan_dir(q, k, v, g, s0t, reverse, scratch, d == 1, want_state)
        if want_state:
            st_ref[d] = st.T
    _finish_scan(scratch[0], gain_ref, zr_ref, o_ref)


def _gla(z, zs, w_gate_ext, b_gate, out_gain, state, layer, o_prev=None):
    zcol = lambda width, col: (lambda seq, row0: pl.BlockSpec(
        (seq, width), lambda b, h: (row0 + b, col // width + h)))
    fixed = lambda spec: (lambda seq, row0: spec)
    z_specs = [
        zcol(B_DK, 5120), zcol(B_DK, 5632), zcol(B_DV, 6144), zcol(B_DV, 7168),
        lambda seq, row0: pl.BlockSpec((seq, LANES), lambda b, h: (row0 + b, ZS_LR // LANES)),
        fixed(pl.BlockSpec((2, LANES, B_DK), lambda b, h: (0, 0, h))),
        fixed(pl.BlockSpec((2, 1, B_DK), lambda b, h: (0, 0, h))),
        fixed(pl.BlockSpec((1, B_DV), lambda b, h: (0, 0))),
    ]
    args = [z, z, z, z, zs, w_gate_ext, b_gate.reshape(2, 1, B_HEADS * B_DK),
            out_gain.reshape(1, B_DV)]
    return _scan_call(_gla_kernel, "gla", z_specs, args, B_HEADS, B_DK, B_DV, state, layer, o_prev)


def _mla_prep_kernel(cq_ref, ckv_ref, kr_ref, gq_ref, gkv_ref, qlat_ref, ckvk_ref):
    cq = cq_ref[...]
    qlat_ref[...] = (cq * lax.rsqrt(jnp.mean(cq * cq, axis=-1, keepdims=True) + EPS)
                     * gq_ref[...]).astype(BF16)
    ckv = ckv_ref[...]
    ckvk_ref[:, :KV_LORA] = (ckv * lax.rsqrt(jnp.mean(ckv * ckv, axis=-1, keepdims=True) + EPS)
                             * gkv_ref[...])
    ckvk_ref[:, KV_LORA:] = kr_ref[:, :QK_ROPE]


def _mla_prep(z, q_gain, kv_gain):
    tm = 512
    return pl.pallas_call(
        _mla_prep_kernel,
        grid=(N_TOK // tm,),
        in_specs=[
            pl.BlockSpec((tm, Q_LORA), lambda i: (i, ZS_CQ // Q_LORA)),
            pl.BlockSpec((tm, KV_LORA), lambda i: (i, ZS_CKV // KV_LORA)),
            pl.BlockSpec((tm, LANES), lambda i: (i, ZS_KR // LANES)),
            pl.BlockSpec((1, Q_LORA), lambda i: (0, 0)),
            pl.BlockSpec((1, KV_LORA), lambda i: (0, 0)),
        ],
        out_specs=[pl.BlockSpec((tm, Q_LORA), lambda i: (i, 0)),
                   pl.BlockSpec((tm, KV_LORA + QK_ROPE), lambda i: (i, 0))],
        out_shape=[jax.ShapeDtypeStruct((N_TOK, Q_LORA), BF16),
                   jax.ShapeDtypeStruct((N_TOK, KV_LORA + QK_ROPE), F32)],
        compiler_params=_params(("parallel",)),
        name="mla_prep",
    )(z, z, z, q_gain.reshape(1, Q_LORA), kv_gain.reshape(1, KV_LORA))


def _swap_halves(t, lane):
    width = t.shape[-1]
    return jnp.where((lane & 31) < 16, pltpu.roll(t, width - 16, 1), pltpu.roll(t, 16, 1))


def _qup_kernel(x_ref, w_ref, gn_ref, gr_ref, c_ref, s_ref, qn_ref, qr_ref):
    acc = jnp.dot(x_ref[...], w_ref[...], preferred_element_type=F32)
    nope_w = C_HEADS * QK_NOPE
    tm = acc.shape[0]
    lane = lax.broadcasted_iota(jnp.int32, (tm, LANES), 1)
    low = lane < QK_ROPE
    gn = gn_ref[...]
    gr = gr_ref[...]
    cos = c_ref[...]
    sin = s_ref[...]
    for p in range(C_HEADS // 2):
        r2 = acc[:, nope_w + LANES * p: nope_w + LANES * (p + 1)]
        sq = r2 * r2
        ss_rope = (jnp.sum(jnp.where(low, sq, 0.0), axis=-1, keepdims=True),
                   jnp.sum(jnp.where(low, 0.0, sq), axis=-1, keepdims=True))
        inv = []
        for hh in range(2):
            h = 2 * p + hh
            nh = acc[:, QK_NOPE * h: QK_NOPE * (h + 1)]
            ssn = jnp.sum(nh * nh, axis=-1, keepdims=True)
            r = lax.rsqrt((ssn + ss_rope[hh]) / QK_DIM + EPS)
            qn_ref[:, QK_NOPE * h: QK_NOPE * (h + 1)] = (nh * r * gn).astype(BF16)
            inv.append(r)
        t = r2 * jnp.where(low, inv[0], inv[1]) * gr
        qr_ref[:, LANES * p: LANES * (p + 1)] = (t * cos + _swap_halves(t, lane) * sin).astype(BF16)


def _qup(qlat, w, gn, gr, cos, sin):
    tm = 512
    n_rows = qlat.shape[0]
    row = lambda width: pl.BlockSpec((tm, width), lambda i: (i, 0))
    full = lambda a: pl.BlockSpec(a.shape, lambda i: (0, 0))
    return pl.pallas_call(
        _qup_kernel,
        grid=(n_rows // tm,),
        in_specs=[row(Q_LORA), full(w), full(gn), full(gr), row(LANES), row(LANES)],
        out_specs=[row(C_HEADS * QK_NOPE), row(C_HEADS * QK_ROPE)],
        out_shape=[jax.ShapeDtypeStruct((n_rows, C_HEADS * QK_NOPE), BF16),
                   jax.ShapeDtypeStruct((n_rows, C_HEADS * QK_ROPE), BF16)],
        compiler_params=_params(("parallel",)),
        name="mla_q_up",
    )(qlat, w, gn, gr, cos, sin)


def _kvup_kernel(*refs, sample):
    if sample:
        cache_ref, x_ref, w_ref, gn_ref, gr_ref, c_ref, s_ref, kn_ref, kr_ref, v_ref = refs
        from_cache = pl.program_id(1) == 0
        x = jnp.where(from_cache, cache_ref[...], x_ref[...])
    else:
        x_ref, w_ref, gn_ref, gr_ref, kn_ref, kr_ref, v_ref = refs
        x = x_ref[...]
    acc = jnp.dot(x[:, :KV_LORA].astype(BF16), w_ref[...], preferred_element_type=F32)
    nope_w = C_HEADS * QK_NOPE
    v_ref[...] = acc[:, nope_w:].astype(BF16)
    tm = acc.shape[0]
    lane = lax.broadcasted_iota(jnp.int32, (tm, LANES), 1)
    low = lane < QK_ROPE
    kr = x[:, KV_LORA:]
    ss_rope = jnp.sum(kr * kr, axis=-1, keepdims=True)
    rot = jnp.concatenate([kr, kr], axis=-1) * gr_ref[...]
    if sample:
        turned = rot * c_ref[...] + _swap_halves(rot, lane) * s_ref[...]
        rot = jnp.where(from_cache, rot, turned)
    gn = gn_ref[...]
    for p in range(C_HEADS // 2):
        inv = []
        for hh in range(2):
            h = 2 * p + hh
            nh = acc[:, QK_NOPE * h: QK_NOPE * (h + 1)]
            ssn = jnp.sum(nh * nh, axis=-1, keepdims=True)
            r = lax.rsqrt((ssn + ss_rope) / QK_DIM + EPS)
            kn_ref[:, QK_NOPE * h: QK_NOPE * (h + 1)] = (nh * r * gn).astype(BF16)
            inv.append(r)
        kr_ref[:, LANES * p: LANES * (p + 1)] = (rot * jnp.where(low, inv[0], inv[1])).astype(BF16)


def _kvup(ckvk, w, gn, gr, cache=None, layer=None, cos=None, sin=None):
    tm = PAST_LEN
    sample = cache is not None
    width = KV_LORA + QK_ROPE
    if sample:
        per_seq = 1 + DEC_SEQ // tm
        grid = (DEC_BATCH, per_seq)
        row_blk = lambda b, t: b * per_seq + t
        new_blk = lambda b, t: N_PROMPT // tm + b * (per_seq - 1) + jnp.maximum(t - 1, 0)
        full = lambda a: pl.BlockSpec(a.shape, lambda b, t: (0, 0))
        table = pl.BlockSpec((tm, LANES), lambda b, t: (jnp.maximum(t - 1, 0), 0))
        in_specs = [pl.BlockSpec((None, None, tm, width), lambda b, t: (b, layer, 0, 0)),
                    pl.BlockSpec((tm, width), lambda b, t: (new_blk(b, t), 0)),
                    full(w), full(gn), full(gr), table, table]
        args = (cache, ckvk, w, gn, gr, cos, sin)
        row = lambda wd: pl.BlockSpec((tm, wd), lambda b, t: (row_blk(b, t), 0))
        n_rows = DEC_BATCH * per_seq * tm
    else:
        grid = (N_PROMPT // tm,)
        full = lambda a: pl.BlockSpec(a.shape, lambda i: (0, 0))
        in_specs = [pl.BlockSpec((tm, width), lambda i: (i, 0)), full(w), full(gn), full(gr)]
        args = (ckvk, w, gn, gr)
        row = lambda wd: pl.BlockSpec((tm, wd), lambda i: (i, 0))
        n_rows = N_PROMPT
    return pl.pallas_call(
        functools.partial(_kvup_kernel, sample=sample),
        grid=grid,
        in_specs=in_specs,
        out_specs=[row(C_HEADS * QK_NOPE), row(C_HEADS * QK_ROPE), row(C_HEADS * C_DV)],
        out_shape=[jax.ShapeDtypeStruct((n_rows, C_HEADS * QK_NOPE), BF16),
                   jax.ShapeDtypeStruct((n_rows, C_HEADS * QK_ROPE), BF16),
                   jax.ShapeDtypeStruct((n_rows, C_HEADS * C_DV), BF16)],
        compiler_params=_params(("parallel",) * len(grid)),
        name="mla_kv_up_sample" if sample else "mla_kv_up_prompt",
    )(*args)


def _attn_kernel(qn_ref, qr_ref, kn_ref, kr_ref, v_ref, o_ref):
    tq = qr_ref.shape[0]
    lane = lax.broadcasted_iota(jnp.int32, (tq, LANES), 1)
    qr = qr_ref[...]
    kr = kr_ref[...]
    trans_b = (((1,), (1,)), ((), ()))
    for hh in range(2):
        cols = slice(QK_NOPE * hh, QK_NOPE * (hh + 1))
        own = (lane < QK_ROPE) if hh == 0 else (lane >= QK_ROPE)
        qrm = jnp.where(own, qr, jnp.zeros_like(qr))
        s = lax.dot_general(qn_ref[:, cols], kn_ref[:, cols], trans_b, preferred_element_type=F32)
        s = s + lax.dot_general(qrm, kr, trans_b, preferred_element_type=F32)
        s = s * (QK_DIM ** -0.5)
        e = jnp.exp(s - jnp.max(s, axis=-1, keepdims=True))
        p = (e / jnp.sum(e, axis=-1, keepdims=True)).astype(BF16)
        o_ref[:, cols] = jnp.dot(p, v_ref[:, cols], preferred_element_type=F32).astype(BF16)


def _attn_kernel_inplace(qn_ref, qr_ref, kn_ref, kr_ref, v_ref, prev_ref, o_ref):
    del prev_ref
    _attn_kernel(qn_ref, qr_ref, kn_ref, kr_ref, v_ref, o_ref)


def _attention(qn, qr, kn, kr, v, o_prev=None):
    sample = o_prev is not None
    tq = SEQ
    q_row0, lq, lk, nb = ((N_PROMPT, DEC_SEQ, PAST_LEN + DEC_SEQ, DEC_BATCH) if sample
                          else (0, SEQ, SEQ, BATCH))
    nq = lq // tq
    qspec = lambda width: pl.BlockSpec((tq, width), lambda b, p, i: (q_row0 // tq + b * nq + i, p))
    kspec = lambda width: pl.BlockSpec((lk, width), lambda b, p, i: (b, p))
    in_specs = [qspec(2 * QK_NOPE), qspec(LANES), kspec(2 * QK_NOPE), kspec(LANES), kspec(2 * C_DV)]
    args = [qn, qr, kn, kr, v]
    if sample:
        in_specs.append(pl.BlockSpec(memory_space=pl.ANY))
        args.append(o_prev)
    return pl.pallas_call(
        _attn_kernel_inplace if sample else _attn_kernel,
        grid=(nb, C_HEADS // 2, nq),
        in_specs=in_specs,
        out_specs=qspec(2 * C_DV),
        out_shape=jax.ShapeDtypeStruct((N_TOK, C_HEADS * C_DV), BF16),
        input_output_aliases={5: 0} if sample else {},
        compiler_params=_params(("parallel", "parallel", "parallel")),
        name="mla_attention_sample" if sample else "mla_attention_prompt",
    )(*args)


def _rope_tables():
    rows = DEC_SEQ // GRID_W
    row = jnp.repeat(jnp.arange(rows, dtype=F32), GRID_W)
    col = jnp.tile(jnp.arange(GRID_W, dtype=F32), rows)
    inv_freq = ROPE_BASE ** (-jnp.arange(0, ROPE_AXIS, 2, dtype=F32) / ROPE_AXIS)
    ar = row[:, None] * inv_freq
    ac = col[:, None] * inv_freq
    cos = jnp.concatenate([jnp.cos(ar), jnp.cos(ar), jnp.cos(ac), jnp.cos(ac)], axis=-1)
    sin = jnp.concatenate([-jnp.sin(ar), jnp.sin(ar), -jnp.sin(ac), jnp.sin(ac)], axis=-1)
    return jnp.tile(cos, (1, 2)), jnp.tile(sin, (1, 2))


def _layer_weights(l, w_in, gla_w_gate_up, mla_w_q_up, mla_w_kv_up, mla_q_norm, mla_k_norm):
    gate0 = Z_MAIN + 2 * GLA_RANK + Q_LORA + KV_LORA + QK_ROPE
    wg = w_in[l][:, gate0 - GATE_PAD:].astype(BF16)
    narrow = wg[:, GATE_PAD - (gate0 - Z_MAIN):GATE_PAD]
    zeros = lambda n: jnp.zeros((D_MODEL, n), BF16)
    wz = jnp.concatenate([narrow[:, 2 * GLA_RANK:], zeros(LANES - QK_ROPE),
                          narrow[:, :2 * GLA_RANK], zeros(LANES - 2 * GLA_RANK)], axis=1)
    ext = jnp.zeros((2, LANES, B_HEADS * B_DK), F32)
    ext = ext.at[0, :GLA_RANK].set(gla_w_gate_up[l, 0])
    ext = ext.at[1, GLA_RANK:2 * GLA_RANK].set(gla_w_gate_up[l, 1])
    wq = mla_w_q_up[l].reshape(Q_LORA, C_HEADS, QK_DIM)
    wq = jnp.concatenate([wq[:, :, :QK_NOPE].reshape(Q_LORA, -1),
                          wq[:, :, QK_NOPE:].reshape(Q_LORA, -1)], axis=1).astype(BF16)
    wkv = mla_w_kv_up[l].reshape(KV_LORA, C_HEADS, QK_NOPE + C_DV)
    wkv = jnp.concatenate([wkv[:, :, :QK_NOPE].reshape(KV_LORA, -1),
                           wkv[:, :, QK_NOPE:].reshape(KV_LORA, -1)], axis=1).astype(BF16)
    split = lambda g: (g[:QK_NOPE].reshape(1, QK_NOPE), jnp.tile(g[QK_NOPE:], 2).reshape(1, LANES))
    return wz, wg, ext.astype(BF16), wq, wkv, split(mla_q_norm[l]), split(mla_k_norm[l])


def kernel(x_prompt, x_sample, cache_mla, state_hgrn, state_gla, c, c_ctx, norm1, w_mod, b_mod, w_in,
           hgrn_lb_logits, hgrn_out_norm, gla_w_gate_up, gla_b_gate, gla_out_norm, mla_q_a_norm,
           mla_w_q_up, mla_kv_a_norm, mla_w_kv_up, mla_q_norm, mla_k_norm, w_branch_a, w_branch_b,
           w_branch_c, w_out, norm2, w_mlp_in, w_mlp_out):
    x = (x_prompt.reshape(N_PROMPT, D_MODEL), x_sample.reshape(N_SAMPLE, D_MODEL))
    cond =jnp.concatenate([c_ctx[None], c, jnp.zeros((MOD_ROWS - 1 - DEC_BATCH, D_MODEL), F32)])
    mod = _modulation(cond, w_mod, b_mod).reshape(DEPTH * MOD_ROWS, 1, N_MOD * D_MODEL)

    cos_s, sin_s = _rope_tables()
    cos_q = jnp.concatenate([jnp.ones((N_PROMPT, LANES), F32), jnp.tile(cos_s, (DEC_BATCH, 1))])
    sin_q = jnp.concatenate([jnp.zeros((N_PROMPT, LANES), F32), jnp.tile(sin_s, (DEC_BATCH, 1))])

    wb_a, wb_b, wb_c = w_branch_a.astype(BF16), w_branch_b.astype(BF16), w_branch_c.astype(BF16)
    caches, hgrn_states, gla_states = [], [], []
    for l in range(DEPTH):
        ws, wg, wlr, wq, wkv, (qgn, qgr), (kgn, kgr) = _layer_weights(
            l, w_in, gla_w_gate_up, mla_w_q_up, mla_w_kv_up, mla_q_norm, mla_k_norm)

        h = _norm_mod(x, norm1[l], mod, l, 0)
        z = _matmul(h, w_in, F32, 1024, 512, layer=l, n_cols=Z_MAIN, name="in_proj")
        zs = _matmul(h, ws, F32, 1024, ZS_COLS // 2, name="in_proj_small")

        o_a, sa = _hgrn(z, hgrn_lb_logits, hgrn_out_norm[l], None, l)
        (o_a,) = _hgrn(z, hgrn_lb_logits, hgrn_out_norm[l], state_hgrn, l, o_prev=o_a)
        o_b, sb = _gla(z, zs, wlr, gla_b_gate[l], gla_out_norm[l], None, l)
        (o_b,) = _gla(z, zs, wlr, gla_b_gate[l], gla_out_norm[l], state_gla, l, o_prev=o_b)
        hgrn_states.append(sa)
        gla_states.append(sb)

        qlat, ckvk = _mla_prep(zs, mla_q_a_norm[l], mla_kv_a_norm[l])
        caches.append(ckvk[:N_PROMPT].reshape(BATCH, SEQ, KV_LORA + QK_ROPE))
        qn, qr = _qup(qlat, wq, qgn, qgr, cos_q, sin_q)
        kn_p, kr_p, v_p = _kvup(ckvk, wkv, kgn, kgr)
        kn_s, kr_s, v_s = _kvup(ckvk, wkv, kgn, kgr, cache_mla, l, cos_s, sin_s)
        o_c = _attention(qn, qr, kn_p, kr_p, v_p)
        o_c = _attention(qn, qr, kn_s, kr_s, v_s, o_prev=o_c)

        merged = _merge(h, wg, o_a, wb_a, o_b, wb_b, o_c, wb_c, l)
        x = _matmul_residual(merged, w_out, x, mod, l, 2, 0, "out_proj")

        h2 = _norm_mod(x, norm2[l], mod, l, 3)
        u = _matmul(h2, w_mlp_in, BF16, 1024, 512, relu2=True, layer=l, n_cols=D_FF, name="mlp_in")
        n_kb = D_FF // D_MODEL
        for kb in range(n_kb):
            last = l == DEPTH - 1 and kb == n_kb - 1
            x = _matmul_residual(u, w_mlp_out, x, mod, l, 5, kb, "mlp_out", split_out=last)

    y_prompt = x[0].reshape(BATCH, SEQ, D_MODEL)
    y_sample = x[1].reshape(DEC_BATCH, DEC_SEQ, D_MODEL)
    return (y_prompt, y_sample, jnp.stack(caches, axis=1), jnp.stack(hgrn_states, axis=1),
            jnp.stack(gla_states, axis=1))
```

```python
import functools

import numpy as np
import jax
import jax.numpy as jnp
from jax import lax
from jax.experimental import pallas as pl
from jax.experimental.pallas import tpu as pltpu

F32 = jnp.float32
BF16 = jnp.bfloat16

D_MODEL = 4096
BATCH = 32
SEQ = 256
DEPTH = 2
DEC_BATCH = 2
DEC_SEQ = 1024
PAST_LEN = 512
GRID_W = 64
A_HEADS = 8
A_DK = 128
A_DV = 128
B_HEADS = 4
B_DK = 128
B_DV = 256
GLA_RANK = 16
GLA_GATE_NORM = 16.0
C_HEADS = 16
Q_LORA = 1024
KV_LORA = 512
QK_NOPE = 128
QK_ROPE = 64
C_DV = 128
QK_DIM = QK_NOPE + QK_ROPE
ROPE_AXIS = QK_ROPE // 2
ROPE_BASE = 10000.0
N_MOD = 6
D_FF = 4 * D_MODEL
EPS = 1e-6

N_PROMPT = BATCH * SEQ
N_SAMPLE = DEC_BATCH * DEC_SEQ
N_TOK = N_PROMPT + N_SAMPLE
MOD_ROWS = 8

Z_MAIN = 8192
ZS_CQ = 0
ZS_CKV = ZS_CQ + Q_LORA
ZS_KR = ZS_CKV + KV_LORA
ZS_LR = ZS_KR + 128
ZS_COLS = ZS_LR + 128
ROW_ALIGN = 32

LANES = 128
MXU_COLS = 256
V7X_VMEM_BYTES = 64 * 1024 * 1024
VMEM_BIG = 60 * 1024 * 1024
VMEM_MID = 40 * 1024 * 1024

CH = 32
SB = 8


def _params(sem, vmem=VMEM_MID):
    return pltpu.CompilerParams(dimension_semantics=sem, vmem_limit_bytes=vmem)


def _mod_row(i, tm):
    return jnp.where(i < N_PROMPT // tm, 0, 1 + (i - N_PROMPT // tm) // (DEC_SEQ // tm))


def _sigmoid(x):
    return 1.0 / (1.0 + jnp.exp(-x))


def _log_sigmoid(x):
    return jnp.minimum(x, 0.0) - jnp.log1p(jnp.exp(-jnp.abs(x)))


def _mod_kernel(cond_ref, w_ref, b_ref, o_ref):
    c = cond_ref[...]
    s = (c * _sigmoid(c)).astype(BF16)
    o_ref[...] = jnp.dot(s, w_ref[...].astype(BF16), preferred_element_type=F32) + b_ref[...]


def _modulation(cond, w_mod, b_mod):
    tn = 512
    n = N_MOD * D_MODEL
    return pl.pallas_call(
        _mod_kernel,
        grid=(DEPTH, n // tn),
        in_specs=[
            pl.BlockSpec((MOD_ROWS, D_MODEL), lambda l, j: (0, 0)),
            pl.BlockSpec((None, D_MODEL, tn), lambda l, j: (l, 0, j)),
            pl.BlockSpec((None, 1, tn), lambda l, j: (l, 0, j)),
        ],
        out_specs=pl.BlockSpec((None, MOD_ROWS, tn), lambda l, j: (l, 0, j)),
        out_shape=jax.ShapeDtypeStruct((DEPTH, MOD_ROWS, n), F32),
        compiler_params=_params(("parallel", "parallel")),
        name="modulation",
    )(cond, w_mod, b_mod.reshape(DEPTH, 1, n))


def _group_specs(tm, tn, row_of, col_of):
    npb = N_PROMPT // tm
    last = npb - 1
    return [pl.BlockSpec((tm, tn), lambda *g: (jnp.minimum(row_of(*g), last), col_of(*g))),
            pl.BlockSpec((tm, tn), lambda *g: (jnp.maximum(row_of(*g) - npb, 0), col_of(*g)))]


def _norm_mod_kernel(*refs):
    *x_refs, g_ref, sh_ref, sc_ref, o_ref = refs

    def run(x_ref):
        x = x_ref[...]
        ms = jnp.mean(x * x, axis=-1, keepdims=True)
        y = x * lax.rsqrt(ms + EPS) * g_ref[...]
        o_ref[...] = (y * (1.0 + sc_ref[...]) + sh_ref[...]).astype(BF16)

    if len(x_refs) == 2:
        in_prompt = pl.program_id(0) < N_PROMPT // o_ref.shape[0]
        pl.when(in_prompt)(lambda: run(x_refs[0]))
        pl.when(jnp.logical_not(in_prompt))(lambda: run(x_refs[1]))
    else:
        run(x_refs[0])


def _norm_mod(x, gain, mod, layer, shift_chunk):
    tm = 256
    base = layer * MOD_ROWS
    if isinstance(x, tuple):
        x_specs = _group_specs(tm, D_MODEL, lambda i: i, lambda i: 0)
    else:
        x_specs, x = [pl.BlockSpec((tm, D_MODEL), lambda i: (i, 0))], (x,)
    return pl.pallas_call(
        _norm_mod_kernel,
        grid=(N_TOK // tm,),
        in_specs=x_specs + [
            pl.BlockSpec((1, D_MODEL), lambda i: (0, 0)),
            pl.BlockSpec((None, 1, D_MODEL), lambda i: (base + _mod_row(i, tm), 0, shift_chunk)),
            pl.BlockSpec((None, 1, D_MODEL), lambda i: (base + _mod_row(i, tm), 0, shift_chunk + 1)),
        ],
        out_specs=pl.BlockSpec((tm, D_MODEL), lambda i: (i, 0)),
        out_shape=jax.ShapeDtypeStruct((N_TOK, D_MODEL), BF16),
        compiler_params=_params(("parallel",)),
        name="norm_mod",
    )(*x, gain.reshape(1, D_MODEL), mod, mod)


def _cast_weight(w_ref, wb_ref):
    @pl.when(pl.program_id(1) == 0)
    def _():
        wb_ref[...] = w_ref[...].astype(BF16)
    return wb_ref


TRANS_B = (((1,), (1,)), ((), ()))


def _mm_kernel(x_ref, w_ref, o_ref, wb_ref, *, relu2, transposed):
    w_ref = _cast_weight(w_ref, wb_ref)
    if transposed:
        acc = lax.dot_general(x_ref[...], w_ref[...], TRANS_B, preferred_element_type=F32)
    else:
        acc = jnp.dot(x_ref[...], w_ref[...], preferred_element_type=F32)
    if relu2:
        acc = jnp.square(jnp.maximum(acc, 0.0))
    o_ref[...] = acc.astype(o_ref.dtype)


def _matmul(x, w, out_dtype, layer, n_cols, relu2=False, transposed=False, name="matmul"):
    tm, tn = 1024, 512
    m, k = x.shape
    if transposed:
        w_spec = pl.BlockSpec((None, tn, k), lambda j, i: (layer, j, 0))
    else:
        w_spec = pl.BlockSpec((None, k, tn), lambda j, i: (layer, 0, j))
    return pl.pallas_call(
        functools.partial(_mm_kernel, relu2=relu2, transposed=transposed),
        grid=(n_cols // tn, m // tm),
        in_specs=[pl.BlockSpec((tm, k), lambda j, i: (i, 0)), w_spec],
        out_specs=pl.BlockSpec((tm, tn), lambda j, i: (i, j)),
        out_shape=jax.ShapeDtypeStruct((m, n_cols), out_dtype),
        scratch_shapes=[pltpu.VMEM((tn, k) if transposed else (k, tn), BF16)],
        compiler_params=_params(("parallel", "arbitrary"), VMEM_BIG),
        name=name,
    )(x, w)


def _cast_rows_kernel(w_ref, o_ref):
    o_ref[...] = w_ref[0].astype(BF16)


def _cast_rows(w, layer, row0, n_rows, block_rows, name):
    k = w.shape[2]
    assert row0 % ROW_ALIGN == 0 and block_rows % ROW_ALIGN == 0 and n_rows % block_rows == 0
    start = lambda r: pl.multiple_of(ROW_ALIGN * (row0 // ROW_ALIGN + (block_rows // ROW_ALIGN) * r),
                                     ROW_ALIGN)
    return pl.pallas_call(
        _cast_rows_kernel,
        grid=(n_rows // block_rows,),
        in_specs=[pl.BlockSpec((pl.Element(1), pl.Element(block_rows), pl.Element(k)),
                               lambda r: (layer, start(r), 0))],
        out_specs=pl.BlockSpec((block_rows, k), lambda r: (r, 0)),
        out_shape=jax.ShapeDtypeStruct((n_rows, k), BF16),
        compiler_params=_params(("parallel",)),
        name=name,
    )(w)


def _in_small_kernel(x_ref, w_ref, o_ref):
    x = x_ref[...]
    tm = x.shape[0]
    n_lr, n_kv = 2 * GLA_RANK, KV_LORA + QK_ROPE
    part = lambda r0, rows: lax.dot_general(x, w_ref[r0:r0 + rows, :], TRANS_B,
                                            preferred_element_type=F32)
    o_ref[:, ZS_LR:ZS_LR + n_lr] = part(0, n_lr)
    o_ref[:, ZS_LR + n_lr:] = jnp.zeros((tm, ZS_COLS - ZS_LR - n_lr), F32)
    o_ref[:, ZS_CQ:ZS_CQ + Q_LORA] = part(n_lr, Q_LORA)
    o_ref[:, ZS_CKV:ZS_CKV + n_kv] = part(n_lr + Q_LORA, n_kv)
    o_ref[:, ZS_CKV + n_kv:ZS_LR] = jnp.zeros((tm, ZS_LR - ZS_CKV - n_kv), F32)


def _in_small(h, w_small):
    tm = 512
    return pl.pallas_call(
        _in_small_kernel,
        grid=(N_TOK // tm,),
        in_specs=[pl.BlockSpec((tm, D_MODEL), lambda i: (i, 0)),
                  pl.BlockSpec(w_small.shape, lambda i: (0, 0))],
        out_specs=pl.BlockSpec((tm, ZS_COLS), lambda i: (i, 0)),
        out_shape=jax.ShapeDtypeStruct((N_TOK, ZS_COLS), F32),
        compiler_params=_params(("parallel",), VMEM_BIG),
        name="in_proj_small",
    )(h, w_small)


def _mm_res_kernel(*refs, n_res, n_out):
    x_ref, w_ref = refs[:2]
    res_refs = refs[2:2 + n_res]
    g_ref = refs[2 + n_res]
    o_refs = refs[3 + n_res:3 + n_res + n_out]
    w_ref = _cast_weight(w_ref, refs[3 + n_res + n_out])
    tm, tn = o_refs[0].shape
    in_prompt = pl.program_id(1) < N_PROMPT // tm
    acc_ref = refs[-1] if n_out == 2 else o_refs[0]
    x = x_ref[...]
    for c0 in range(0, tn, MXU_COLS):
        cols = slice(c0, c0 + MXU_COLS)
        part = g_ref[:, cols] * jnp.dot(x, w_ref[:, cols], preferred_element_type=F32)
        if n_res == 2:
            res = jnp.where(in_prompt, res_refs[0][:, cols], res_refs[1][:, cols])
        else:
            res = res_refs[0][:, cols]
        acc_ref[:, cols] = res + part
    if n_out == 2:
        @pl.when(in_prompt)
        def _():
            o_refs[0][...] = acc_ref[...]

        @pl.when(jnp.logical_not(in_prompt))
        def _():
            o_refs[1][...] = acc_ref[...]


def _matmul_residual(a, w, res, mod, layer, gate_chunk, k_block, name, split_out=False):
    tm, tn, tk = (512 if split_out else 1024), 512, D_MODEL
    m = a.shape[0]
    n = w.shape[2]
    base = layer * MOD_ROWS
    cpc = D_MODEL // tn
    whole = pl.BlockSpec((tm, tn), lambda j, i: (i, j))
    pair = _group_specs(tm, tn, lambda j, i: i, lambda j, i: j)
    res = res if isinstance(res, tuple) else (res,)
    if split_out:
        out_specs = pair
        out_shape = [jax.ShapeDtypeStruct((N_PROMPT, n), F32), jax.ShapeDtypeStruct((N_SAMPLE, n), F32)]
    else:
        out_specs, out_shape = whole, jax.ShapeDtypeStruct((m, n), F32)
    return pl.pallas_call(
        functools.partial(_mm_res_kernel, n_res=len(res), n_out=2 if split_out else 1),
        grid=(n // tn, m // tm),
        in_specs=[
            pl.BlockSpec((tm, tk), lambda j, i: (i, k_block)),
            pl.BlockSpec((None, tk, tn), lambda j, i: (layer, k_block, j)),
            *(pair if len(res) == 2 else [whole]),
            pl.BlockSpec((None, 1, tn),
                         lambda j, i: (base + _mod_row(i, tm), 0, gate_chunk * cpc + j)),
        ],
        out_specs=out_specs,
        out_shape=out_shape,
        scratch_shapes=[pltpu.VMEM((tk, tn), BF16)] + ([pltpu.VMEM((tm, tn), F32)] if split_out else []),
        compiler_params=_params(("parallel", "arbitrary"), VMEM_BIG),
        name=name,
    )(a, w, *res, mod)


def _merge_kernel(h_ref, wga_ref, wgb_ref, wgc_ref, oa_ref, wa_ref, ob_ref, wb_ref, oc_ref, wc_ref,
                  o_ref):
    h = h_ref[...]

    def branch(wg_ref, o_r, w_r):
        g = lax.dot_general(h, wg_ref[...], TRANS_B, preferred_element_type=F32)
        p = jnp.dot(o_r[...], w_r[...], preferred_element_type=F32)
        return _sigmoid(g) * p

    acc = branch(wga_ref, oa_ref, wa_ref)
    acc = acc + branch(wgb_ref, ob_ref, wb_ref)
    acc = acc + branch(wgc_ref, oc_ref, wc_ref)
    o_ref[...] = acc.astype(BF16)


def _merge(h, wg, o_a, w_a, o_b, w_b, o_c, w_c, layer):
    tm, tn = 512, 512
    nj = D_MODEL // tn
    act = lambda width: pl.BlockSpec((tm, width), lambda j, i: (i, 0))
    wcol = lambda rows: pl.BlockSpec((None, rows, tn), lambda j, i: (layer, 0, j))
    gate = lambda b: pl.BlockSpec((tn, D_MODEL), lambda j, i: (b * nj + j, 0))
    return pl.pallas_call(
        _merge_kernel,
        grid=(nj, N_TOK // tm),
        in_specs=[act(D_MODEL), gate(0), gate(1), gate(2),
                  act(o_a.shape[1]), wcol(w_a.shape[1]),
                  act(o_b.shape[1]), wcol(w_b.shape[1]),
                  act(o_c.shape[1]), wcol(w_c.shape[1])],
        out_specs=pl.BlockSpec((tm, tn), lambda j, i: (i, j)),
        out_shape=jax.ShapeDtypeStruct((N_TOK, D_MODEL), BF16),
        compiler_params=_params(("parallel", "parallel"), VMEM_BIG),
        name="merge",
    )(h, wg, wg, wg, o_a, w_a, o_b, w_b, o_c, w_c)


PAD = 8


def _shifted(ref, n_rows, d, reverse):
    return ref[pl.ds(PAD + d if reverse else PAD - d, n_rows), :]


def _seg_cumsum(g, rowc, reverse, c_scr):
    n_rows = g.shape[0]
    x = g
    for sh in (1, 2, 4, 8, 16):
        mask = rowc < CH - sh if reverse else rowc >= sh
        if sh < PAD:
            c_scr[pl.ds(PAD, n_rows), :] = x
            moved = _shifted(c_scr, n_rows, sh, reverse)
        else:
            moved = pltpu.roll(x, n_rows - sh if reverse else sh, 0)
        x = x + jnp.where(mask, moved, 0.0)
    return x


def _scan_dir(q, k, v, g, s0t, reverse, scratch, accumulate, want_state):
    o_scr, k_scr, c_scr, v_scr = scratch
    n_rows, dk = q.shape
    n = n_rows // CH
    rowc = lax.broadcasted_iota(jnp.int32, (n_rows, dk), 0) & (CH - 1)
    cum = _seg_cumsum(g, rowc, reverse, c_scr)
    c_scr[pl.ds(PAD, n_rows), :] = cum
    k_scr[pl.ds(PAD, n_rows), :] = k

    def rows_of(r, count):
        return jnp.broadcast_to(cum[r:r + 1, :], (count, dk))

    def chunk_row(r):
        return jnp.concatenate([rows_of(c * CH + r, CH) for c in range(n)], axis=0)

    last_row = 0 if reverse else CH - 1
    last_b = chunk_row(last_row)
    qt = (q * jnp.exp(cum)).astype(BF16)
    kh = (k * jnp.exp(last_b - cum)).astype(BF16)
    vb = v.astype(BF16)

    rows = rowc & (SB - 1)
    o_diag = jnp.sum(q * k, axis=-1, keepdims=True) * v
    for d in range(1, SB):
        mask = rows < SB - d if reverse else rows >= d
        kd = _shifted(k_scr, n_rows, d, reverse)
        cd = _shifted(c_scr, n_rows, d, reverse)
        vd = _shifted(v_scr, n_rows, d, reverse)
        e = jnp.exp(jnp.where(mask, cum - cd, -jnp.inf))
        o_diag = o_diag + jnp.sum(q * kd * e, axis=-1, keepdims=True) * vd
    if accumulate:
        o_scr[...] += o_diag
    else:
        o_scr[...] = o_diag

    sub = rowc // SB
    nsb = CH // SB
    if reverse:
        own = [rows_of(c * CH + SB * min(i + 1, nsb - 1), SB) for c in range(n) for i in range(nsb)]
        has_piv = sub < nsb - 1
    else:
        own = [rows_of(c * CH + SB * max(i, 1) - 1, SB) for c in range(n) for i in range(nsb)]
        has_piv = sub >= 1
    qe = q * jnp.exp(jnp.where(has_piv, cum - jnp.concatenate(own, axis=0), -jnp.inf))
    qs, ks = [], []
    for pr in range(SB, CH, SB):
        if reverse:
            piv, qmask, kmask = chunk_row(pr), sub == pr // SB - 1, rowc >= pr
        else:
            piv, qmask, kmask = chunk_row(pr - 1), sub == pr // SB, rowc < pr
        qs.append(jnp.where(qmask, qe, 0.0).astype(BF16))
        ks.append((k * jnp.exp(jnp.where(kmask, piv - cum, -jnp.inf))).astype(BF16))
    qcat = jnp.concatenate(qs, axis=-1)
    kcat = jnp.concatenate(ks, axis=-1)

    trans_b = (((1,), (1,)), ((), ()))
    trans_a = (((0,), (0,)), ((), ()))
    st = s0t
    order = range(n - 1, -1, -1) if reverse else range(n)
    for idx, c in enumerate(order):
        sl = slice(c * CH, (c + 1) * CH)
        o_c = lax.dot_general(qt[sl], st.astype(BF16), trans_b, preferred_element_type=F32)
        a = lax.dot_general(qcat[sl], kcat[sl], trans_b, preferred_element_type=F32)
        o_c = o_c + jnp.dot(a.astype(BF16), vb[sl], preferred_element_type=F32)
        o_scr[sl, :] += o_c
        if want_state or idx + 1 < n:
            ut = lax.dot_general(vb[sl], kh[sl], trans_a, preferred_element_type=F32)
            r = c * CH + last_row
            st = st * jnp.exp(cum[r:r + 1, :]) + ut
    return st


def _finish_scan(o_scr, gain_ref, zg_ref, o_ref):
    o = o_scr[...]
    ms = jnp.mean(o * o, axis=-1, keepdims=True)
    zg = zg_ref[...]
    o_ref[...] = (o * lax.rsqrt(ms + EPS) * gain_ref[...] * (zg * _sigmoid(zg))).astype(BF16)


def _split_scan_refs(refs, n_in, sample):
    pos = n_in
    s0_ref = None
    if sample:
        s0_ref = refs[pos]
        pos += 2
    o_ref = refs[pos]
    st_ref = None if sample else refs[pos + 1]
    scratch = refs[pos + (1 if sample else 2):]
    return refs[:n_in], s0_ref, o_ref, st_ref, scratch


def _prep_scan_scratch(scratch, v):
    _, k_scr, c_scr, v_scr = scratch
    n_rows = v.shape[0]
    for ref in (k_scr, c_scr, v_scr):
        zero = jnp.zeros((PAD, ref.shape[1]), F32)
        ref[pl.ds(0, PAD), :] = zero
        ref[pl.ds(PAD + n_rows, PAD), :] = zero
    v_scr[pl.ds(PAD, n_rows), :] = v


def _scan_scratch(seq, dk, dv):
    return [pltpu.VMEM((seq, dv), F32), pltpu.VMEM((seq + 2 * PAD, dk), F32),
            pltpu.VMEM((seq + 2 * PAD, dk), F32), pltpu.VMEM((seq + 2 * PAD, dv), F32)]


def _hgrn_kernel(*refs, layer, sample):
    ins, s0_ref, o_ref, st_ref, scratch = _split_scan_refs(refs, 7, sample)
    lbl_ref, q_ref, ff_ref, fb_ref, i_ref, zg_ref, gain_ref = ins
    has_state, want_state = sample, not sample

    logits = [lbl_ref[l] for l in range(DEPTH)]
    mx = functools.reduce(jnp.maximum, logits)
    ex = [jnp.exp(x - mx) for x in logits]
    tot = functools.reduce(lambda a, b: a + b, ex)
    probs = [e / tot for e in ex]
    cum_first = probs[0]
    cum_l = functools.reduce(lambda a, b: a + b, probs[:layer + 1])
    lb = cum_l - cum_first

    q = q_ref[...]
    v = i_ref[...]
    _prep_scan_scratch(scratch, v)
    for d, (z_ref, reverse) in enumerate(((ff_ref, False), (fb_ref, True))):
        lbd = lb[d:d + 1, :]
        zf = z_ref[...]
        a = jnp.log(lbd)
        b = jnp.log1p(-lbd) + _log_sigmoid(zf)
        delta = a - b
        log_f = jnp.where(delta != delta, a + b,
                          jnp.maximum(a, b) + jnp.log1p(jnp.exp(-jnp.abs(delta))))
        one_minus_f = (1.0 - lbd) * _sigmoid(-zf)
        if has_state:
            s0t = s0_ref[d].T
        else:
            s0t = jnp.zeros((A_DV, A_DK), F32)
        st = _scan_dir(q, one_minus_f, v, log_f, s0t, reverse, scratch, d == 1, want_state)
        if want_state:
            st_ref[d] = st.T
    _finish_scan(scratch[0], gain_ref, zg_ref, o_ref)


def _scan_call(body, name, z_specs, args, heads, dk, dv, state, layer, o_prev):
    sample = o_prev is not None
    seq, nb = (DEC_SEQ, DEC_BATCH) if sample else (SEQ, BATCH)
    row0 = N_PROMPT // seq if sample else 0
    in_specs = [spec(seq, row0) for spec in z_specs]
    args = list(args)
    aliases = {}
    if sample:
        in_specs += [pl.BlockSpec((None, None, 2, None, dk, dv), lambda b, h: (b, layer, 0, h, 0, 0)),
                     pl.BlockSpec(memory_space=pl.ANY)]
        aliases = {len(args) + 1: 0}
        args += [state, o_prev]
    out_specs = [pl.BlockSpec((seq, dv), lambda b, h: (row0 + b, h))]
    out_shape = [jax.ShapeDtypeStruct((N_TOK, heads * dv), BF16)]
    if not sample:
        out_specs.append(pl.BlockSpec((None, 2, None, dk, dv), lambda b, h: (b, 0, h, 0, 0)))
        out_shape.append(jax.ShapeDtypeStruct((nb, 2, heads, dk, dv), F32))
    return pl.pallas_call(
        functools.partial(body, sample=sample),
        grid=(nb, heads),
        in_specs=in_specs,
        out_specs=out_specs,
        out_shape=out_shape,
        input_output_aliases=aliases,
        scratch_shapes=_scan_scratch(seq, dk, dv),
        compiler_params=_params(("parallel", "parallel")),
        name=name + ("_sample" if sample else "_prompt"),
    )(*args)


def _hgrn(z, lb_logits, out_gain, state, layer, o_prev=None):
    zcol = lambda off: (lambda seq, row0: pl.BlockSpec((seq, A_DK), lambda b, h: (row0 + b, off + h)))
    fixed = lambda spec: (lambda seq, row0: spec)
    z_specs = [
        fixed(pl.BlockSpec((DEPTH, 2, A_DK), lambda b, h: (0, 0, h))),
        zcol(0), zcol(A_HEADS), zcol(2 * A_HEADS), zcol(3 * A_HEADS), zcol(4 * A_HEADS),
        fixed(pl.BlockSpec((1, A_DV), lambda b, h: (0, 0))),
    ]
    args = [lb_logits, z, z, z, z, z, out_gain.reshape(1, A_DV)]
    return _scan_call(functools.partial(_hgrn_kernel, layer=layer), "hgrn", z_specs, args,
                      A_HEADS, A_DK, A_DV, state, layer, o_prev)


def _gla_kernel(*refs, sample):
    ins, s0_ref, o_ref, st_ref, scratch = _split_scan_refs(refs, 8, sample)
    q_ref, k_ref, v_ref, zr_ref, lr_ref, wg_ref, bg_ref, gain_ref = ins
    has_state, want_state = sample, not sample

    q = q_ref[...] * (B_DK ** -0.5)
    k = k_ref[...]
    v = v_ref[...]
    _prep_scan_scratch(scratch, v)
    lr = lr_ref[...].astype(BF16)
    for d, reverse in enumerate((False, True)):
        logit = jnp.dot(lr, wg_ref[d], preferred_element_type=F32) + bg_ref[d]
        g = _log_sigmoid(logit) / GLA_GATE_NORM
        if has_state:
            s0t = s0_ref[d].T
        else:
            s0t = jnp.zeros((B_DV, B_DK), F32)
        st = _scan_dir(q, k, v, g, s0t, reverse, scratch, d == 1, want_state)
        if want_state:
            st_ref[d] = st.T
    _finish_scan(scratch[0], gain_ref, zr_ref, o_ref)


def _gla(z, zs, w_gate_ext, b_gate, out_gain, state, layer, o_prev=None):
    zcol = lambda width, col: (lambda seq, row0: pl.BlockSpec(
        (seq, width), lambda b, h: (row0 + b, col // width + h)))
    fixed = lambda spec: (lambda seq, row0: spec)
    z_specs = [
        zcol(B_DK, 5120), zcol(B_DK, 5632), zcol(B_DV, 6144), zcol(B_DV, 7168),
        lambda seq, row0: pl.BlockSpec((seq, LANES), lambda b, h: (row0 + b, ZS_LR // LANES)),
        fixed(pl.BlockSpec((2, LANES, B_DK), lambda b, h: (0, 0, h))),
        fixed(pl.BlockSpec((2, 1, B_DK), lambda b, h: (0, 0, h))),
        fixed(pl.BlockSpec((1, B_DV), lambda b, h: (0, 0))),
    ]
    args = [z, z, z, z, zs, w_gate_ext, b_gate.reshape(2, 1, B_HEADS * B_DK),
            out_gain.reshape(1, B_DV)]
    return _scan_call(_gla_kernel, "gla", z_specs, args, B_HEADS, B_DK, B_DV, state, layer, o_prev)


def _mla_prep_kernel(cq_ref, ckv_ref, kr_ref, gq_ref, gkv_ref, qlat_ref, ckvk_ref):
    cq = cq_ref[...]
    qlat_ref[...] = (cq * lax.rsqrt(jnp.mean(cq * cq, axis=-1, keepdims=True) + EPS)
                     * gq_ref[...]).astype(BF16)
    ckv = ckv_ref[...]
    ckvk_ref[:, :KV_LORA] = (ckv * lax.rsqrt(jnp.mean(ckv * ckv, axis=-1, keepdims=True) + EPS)
                             * gkv_ref[...])
    ckvk_ref[:, KV_LORA:] = kr_ref[:, :QK_ROPE]


def _mla_prep(z, q_gain, kv_gain):
    tm = 512
    return pl.pallas_call(
        _mla_prep_kernel,
        grid=(N_TOK // tm,),
        in_specs=[
            pl.BlockSpec((tm, Q_LORA), lambda i: (i, ZS_CQ // Q_LORA)),
            pl.BlockSpec((tm, KV_LORA), lambda i: (i, ZS_CKV // KV_LORA)),
            pl.BlockSpec((tm, LANES), lambda i: (i, ZS_KR // LANES)),
            pl.BlockSpec((1, Q_LORA), lambda i: (0, 0)),
            pl.BlockSpec((1, KV_LORA), lambda i: (0, 0)),
        ],
        out_specs=[pl.BlockSpec((tm, Q_LORA), lambda i: (i, 0)),
                   pl.BlockSpec((tm, KV_LORA + QK_ROPE), lambda i: (i, 0))],
        out_shape=[jax.ShapeDtypeStruct((N_TOK, Q_LORA), BF16),
                   jax.ShapeDtypeStruct((N_TOK, KV_LORA + QK_ROPE), F32)],
        compiler_params=_params(("parallel",)),
        name="mla_prep",
    )(z, z, z, q_gain.reshape(1, Q_LORA), kv_gain.reshape(1, KV_LORA))


def _swap_halves(t, lane):
    width = t.shape[-1]
    return jnp.where((lane & 31) < 16, pltpu.roll(t, width - 16, 1), pltpu.roll(t, 16, 1))


def _qup_kernel(x_ref, w_ref, gn_ref, gr_ref, c_ref, s_ref, qn_ref, qr_ref):
    acc = jnp.dot(x_ref[...], w_ref[...], preferred_element_type=F32)
    nope_w = C_HEADS * QK_NOPE
    tm = acc.shape[0]
    lane = lax.broadcasted_iota(jnp.int32, (tm, LANES), 1)
    low = lane < QK_ROPE
    gn = gn_ref[...]
    gr = gr_ref[...]
    cos = c_ref[...]
    sin = s_ref[...]
    for p in range(C_HEADS // 2):
        r2 = acc[:, nope_w + LANES * p: nope_w + LANES * (p + 1)]
        sq = r2 * r2
        ss_rope = (jnp.sum(jnp.where(low, sq, 0.0), axis=-1, keepdims=True),
                   jnp.sum(jnp.where(low, 0.0, sq), axis=-1, keepdims=True))
        inv = []
        for hh in range(2):
            h = 2 * p + hh
            nh = acc[:, QK_NOPE * h: QK_NOPE * (h + 1)]
            ssn = jnp.sum(nh * nh, axis=-1, keepdims=True)
            r = lax.rsqrt((ssn + ss_rope[hh]) / QK_DIM + EPS)
            qn_ref[:, QK_NOPE * h: QK_NOPE * (h + 1)] = (nh * r * gn).astype(BF16)
            inv.append(r)
        t = r2 * jnp.where(low, inv[0], inv[1]) * gr
        qr_ref[:, LANES * p: LANES * (p + 1)] = (t * cos + _swap_halves(t, lane) * sin).astype(BF16)


def _qup(qlat, w, gn, gr, cos, sin):
    tm = 512
    n_rows = qlat.shape[0]
    row = lambda width: pl.BlockSpec((tm, width), lambda i: (i, 0))
    full = lambda a: pl.BlockSpec(a.shape, lambda i: (0, 0))
    return pl.pallas_call(
        _qup_kernel,
        grid=(n_rows // tm,),
        in_specs=[row(Q_LORA), full(w), full(gn), full(gr), row(LANES), row(LANES)],
        out_specs=[row(C_HEADS * QK_NOPE), row(C_HEADS * QK_ROPE)],
        out_shape=[jax.ShapeDtypeStruct((n_rows, C_HEADS * QK_NOPE), BF16),
                   jax.ShapeDtypeStruct((n_rows, C_HEADS * QK_ROPE), BF16)],
        compiler_params=_params(("parallel",)),
        name="mla_q_up",
    )(qlat, w, gn, gr, cos, sin)


def _kvup_kernel(*refs, sample):
    if sample:
        cache_ref, x_ref, w_ref, gn_ref, gr_ref, c_ref, s_ref, kn_ref, kr_ref, v_ref = refs
        from_cache = pl.program_id(1) == 0
        x = jnp.where(from_cache, cache_ref[...], x_ref[...])
    else:
        x_ref, w_ref, gn_ref, gr_ref, kn_ref, kr_ref, v_ref = refs
        x = x_ref[...]
    acc = jnp.dot(x[:, :KV_LORA].astype(BF16), w_ref[...], preferred_element_type=F32)
    nope_w = C_HEADS * QK_NOPE
    v_ref[...] = acc[:, nope_w:].astype(BF16)
    tm = acc.shape[0]
    lane = lax.broadcasted_iota(jnp.int32, (tm, LANES), 1)
    low = lane < QK_ROPE
    kr = x[:, KV_LORA:]
    ss_rope = jnp.sum(kr * kr, axis=-1, keepdims=True)
    rot = jnp.concatenate([kr, kr], axis=-1) * gr_ref[...]
    if sample:
        turned = rot * c_ref[...] + _swap_halves(rot, lane) * s_ref[...]
        rot = jnp.where(from_cache, rot, turned)
    gn = gn_ref[...]
    for p in range(C_HEADS // 2):
        inv = []
        for hh in range(2):
            h = 2 * p + hh
            nh = acc[:, QK_NOPE * h: QK_NOPE * (h + 1)]
            ssn = jnp.sum(nh * nh, axis=-1, keepdims=True)
            r = lax.rsqrt((ssn + ss_rope) / QK_DIM + EPS)
            kn_ref[:, QK_NOPE * h: QK_NOPE * (h + 1)] = (nh * r * gn).astype(BF16)
            inv.append(r)
        kr_ref[:, LANES * p: LANES * (p + 1)] = (rot * jnp.where(low, inv[0], inv[1])).astype(BF16)


def _kvup(ckvk, w, gn, gr, cache=None, layer=None, cos=None, sin=None):
    tm = PAST_LEN
    sample = cache is not None
    width = KV_LORA + QK_ROPE
    if sample:
        per_seq = 1 + DEC_SEQ // tm
        grid = (DEC_BATCH, per_seq)
        row_blk = lambda b, t: b * per_seq + t
        new_blk = lambda b, t: N_PROMPT // tm + b * (per_seq - 1) + jnp.maximum(t - 1, 0)
        full = lambda a: pl.BlockSpec(a.shape, lambda b, t: (0, 0))
        table = pl.BlockSpec((tm, LANES), lambda b, t: (jnp.maximum(t - 1, 0), 0))
        in_specs = [pl.BlockSpec((None, None, tm, width), lambda b, t: (b, layer, 0, 0)),
                    pl.BlockSpec((tm, width), lambda b, t: (new_blk(b, t), 0)),
                    full(w), full(gn), full(gr), table, table]
        args = (cache, ckvk, w, gn, gr, cos, sin)
        row = lambda wd: pl.BlockSpec((tm, wd), lambda b, t: (row_blk(b, t), 0))
        n_rows = DEC_BATCH * per_seq * tm
    else:
        grid = (N_PROMPT // tm,)
        full = lambda a: pl.BlockSpec(a.shape, lambda i: (0, 0))
        in_specs = [pl.BlockSpec((tm, width), lambda i: (i, 0)), full(w), full(gn), full(gr)]
        args = (ckvk, w, gn, gr)
        row = lambda wd: pl.BlockSpec((tm, wd), lambda i: (i, 0))
        n_rows = N_PROMPT
    return pl.pallas_call(
        functools.partial(_kvup_kernel, sample=sample),
        grid=grid,
        in_specs=in_specs,
        out_specs=[row(C_HEADS * QK_NOPE), row(C_HEADS * QK_ROPE), row(C_HEADS * C_DV)],
        out_shape=[jax.ShapeDtypeStruct((n_rows, C_HEADS * QK_NOPE), BF16),
                   jax.ShapeDtypeStruct((n_rows, C_HEADS * QK_ROPE), BF16),
                   jax.ShapeDtypeStruct((n_rows, C_HEADS * C_DV), BF16)],
        compiler_params=_params(("parallel",) * len(grid)),
        name="mla_kv_up_sample" if sample else "mla_kv_up_prompt",
    )(*args)


def _attn_kernel(qn_ref, qr_ref, kn_ref, kr_ref, v_ref, o_ref):
    tq = qr_ref.shape[0]
    lane = lax.broadcasted_iota(jnp.int32, (tq, LANES), 1)
    qr = qr_ref[...]
    kr = kr_ref[...]
    trans_b = (((1,), (1,)), ((), ()))
    for hh in range(2):
        cols = slice(QK_NOPE * hh, QK_NOPE * (hh + 1))
        own = (lane < QK_ROPE) if hh == 0 else (lane >= QK_ROPE)
        qrm = jnp.where(own, qr, jnp.zeros_like(qr))
        s = lax.dot_general(qn_ref[:, cols], kn_ref[:, cols], trans_b, preferred_element_type=F32)
        s = s + lax.dot_general(qrm, kr, trans_b, preferred_element_type=F32)
        s = s * (QK_DIM ** -0.5)
        e = jnp.exp(s - jnp.max(s, axis=-1, keepdims=True))
        p = (e / jnp.sum(e, axis=-1, keepdims=True)).astype(BF16)
        o_ref[:, cols] = jnp.dot(p, v_ref[:, cols], preferred_element_type=F32).astype(BF16)


def _attn_kernel_inplace(qn_ref, qr_ref, kn_ref, kr_ref, v_ref, prev_ref, o_ref):
    del prev_ref
    _attn_kernel(qn_ref, qr_ref, kn_ref, kr_ref, v_ref, o_ref)


def _attention(qn, qr, kn, kr, v, o_prev=None):
    sample = o_prev is not None
    tq = SEQ
    q_row0, lq, lk, nb = ((N_PROMPT, DEC_SEQ, PAST_LEN + DEC_SEQ, DEC_BATCH) if sample
                          else (0, SEQ, SEQ, BATCH))
    nq = lq // tq
    qspec = lambda width: pl.BlockSpec((tq, width), lambda b, p, i: (q_row0 // tq + b * nq + i, p))
    kspec = lambda width: pl.BlockSpec((lk, width), lambda b, p, i: (b, p))
    in_specs = [qspec(2 * QK_NOPE), qspec(LANES), kspec(2 * QK_NOPE), kspec(LANES), kspec(2 * C_DV)]
    args = [qn, qr, kn, kr, v]
    if sample:
        in_specs.append(pl.BlockSpec(memory_space=pl.ANY))
        args.append(o_prev)
    return pl.pallas_call(
        _attn_kernel_inplace if sample else _attn_kernel,
        grid=(nb, C_HEADS // 2, nq),
        in_specs=in_specs,
        out_specs=qspec(2 * C_DV),
        out_shape=jax.ShapeDtypeStruct((N_TOK, C_HEADS * C_DV), BF16),
        input_output_aliases={5: 0} if sample else {},
        compiler_params=_params(("parallel", "parallel", "parallel")),
        name="mla_attention_sample" if sample else "mla_attention_prompt",
    )(*args)


def _rope_tables():
    rows = DEC_SEQ // GRID_W
    row = jnp.repeat(jnp.arange(rows, dtype=F32), GRID_W)
    col = jnp.tile(jnp.arange(GRID_W, dtype=F32), rows)
    inv_freq = ROPE_BASE ** (-jnp.arange(0, ROPE_AXIS, 2, dtype=F32) / ROPE_AXIS)
    ar = row[:, None] * inv_freq
    ac = col[:, None] * inv_freq
    cos = jnp.concatenate([jnp.cos(ar), jnp.cos(ar), jnp.cos(ac), jnp.cos(ac)], axis=-1)
    sin = jnp.concatenate([-jnp.sin(ar), jnp.sin(ar), -jnp.sin(ac), jnp.sin(ac)], axis=-1)
    return jnp.tile(cos, (1, 2)), jnp.tile(sin, (1, 2))


def _layer_weights(l, gla_w_gate_up, mla_w_q_up, mla_w_kv_up, mla_q_norm, mla_k_norm):
    ext =jnp.zeros((2, LANES, B_HEADS * B_DK), F32)
    ext = ext.at[0, :GLA_RANK].set(gla_w_gate_up[l, 0])
    ext = ext.at[1, GLA_RANK:2 * GLA_RANK].set(gla_w_gate_up[l, 1])
    wq = mla_w_q_up[l].reshape(Q_LORA, C_HEADS, QK_DIM)
    wq = jnp.concatenate([wq[:, :, :QK_NOPE].reshape(Q_LORA, -1),
                          wq[:, :, QK_NOPE:].reshape(Q_LORA, -1)], axis=1).astype(BF16)
    wkv = mla_w_kv_up[l].reshape(KV_LORA, C_HEADS, QK_NOPE + C_DV)
    wkv = jnp.concatenate([wkv[:, :, :QK_NOPE].reshape(KV_LORA, -1),
                           wkv[:, :, QK_NOPE:].reshape(KV_LORA, -1)], axis=1).astype(BF16)
    split = lambda g: (g[:QK_NOPE].reshape(1, QK_NOPE), jnp.tile(g[QK_NOPE:], 2).reshape(1, LANES))
    return ext.astype(BF16), wq, wkv, split(mla_q_norm[l]), split(mla_k_norm[l])


def kernel(x_prompt, x_sample, cache_mla, state_hgrn, state_gla, c, c_ctx, norm1, w_mod, b_mod, w_in,
           hgrn_lb_logits, hgrn_out_norm, gla_w_gate_up, gla_b_gate, gla_out_norm, mla_q_a_norm,
           mla_w_q_up, mla_kv_a_norm, mla_w_kv_up, mla_q_norm, mla_k_norm, w_branch_a, w_branch_b,
           w_branch_c, w_out, norm2, w_mlp_in, w_mlp_out):
    x = (x_prompt.reshape(N_PROMPT, D_MODEL), x_sample.reshape(N_SAMPLE, D_MODEL))
    cond =jnp.concatenate([c_ctx[None], c, jnp.zeros((MOD_ROWS - 1 - DEC_BATCH, D_MODEL), F32)])
    mod = _modulation(cond, w_mod, b_mod).reshape(DEPTH * MOD_ROWS, 1, N_MOD * D_MODEL)

    cos_s, sin_s = _rope_tables()
    cos_q = jnp.concatenate([jnp.ones((N_PROMPT, LANES), F32), jnp.tile(cos_s, (DEC_BATCH, 1))])
    sin_q = jnp.concatenate([jnp.zeros((N_PROMPT, LANES), F32), jnp.tile(sin_s, (DEC_BATCH, 1))])

    wb_a, wb_b, wb_c = w_branch_a.astype(BF16), w_branch_b.astype(BF16), w_branch_c.astype(BF16)
    w_in_t = jnp.swapaxes(w_in, 1, 2)
    narrow0, gate0 = Z_MAIN, Z_MAIN + 2 * GLA_RANK + Q_LORA + KV_LORA + QK_ROPE
    caches, hgrn_states, gla_states = [], [], []
    for l in range(DEPTH):
        wlr, wq, wkv, (qgn, qgr), (kgn, kgr) = _layer_weights(
            l, gla_w_gate_up, mla_w_q_up, mla_w_kv_up, mla_q_norm, mla_k_norm)
        ws = _cast_rows(w_in_t, l, narrow0, gate0 - narrow0, (gate0 - narrow0) // 3, "cast_narrow")
        wg = _cast_rows(w_in_t, l, gate0, 3 * D_MODEL, 512, "cast_gates")

        h = _norm_mod(x, norm1[l], mod, l, 0)
        z = _matmul(h, w_in_t, F32, l, Z_MAIN, transposed=True, name="in_proj")
        zs = _in_small(h, ws)

        o_a, sa = _hgrn(z, hgrn_lb_logits, hgrn_out_norm[l], None, l)
        (o_a,) = _hgrn(z, hgrn_lb_logits, hgrn_out_norm[l], state_hgrn, l, o_prev=o_a)
        o_b, sb = _gla(z, zs, wlr, gla_b_gate[l], gla_out_norm[l], None, l)
        (o_b,) = _gla(z, zs, wlr, gla_b_gate[l], gla_out_norm[l], state_gla, l, o_prev=o_b)
        hgrn_states.append(sa)
        gla_states.append(sb)

        qlat, ckvk = _mla_prep(zs, mla_q_a_norm[l], mla_kv_a_norm[l])
        caches.append(ckvk[:N_PROMPT].reshape(BATCH, SEQ, KV_LORA + QK_ROPE))
        qn, qr = _qup(qlat, wq, qgn, qgr, cos_q, sin_q)
        kn_p, kr_p, v_p = _kvup(ckvk, wkv, kgn, kgr)
        kn_s, kr_s, v_s = _kvup(ckvk, wkv, kgn, kgr, cache_mla, l, cos_s, sin_s)
        o_c = _attention(qn, qr, kn_p, kr_p, v_p)
        o_c = _attention(qn, qr, kn_s, kr_s, v_s, o_prev=o_c)

        merged = _merge(h, wg, o_a, wb_a, o_b, wb_b, o_c, wb_c, l)
        x = _matmul_residual(merged, w_out, x, mod, l, 2, 0, "out_proj")

        h2 = _norm_mod(x, norm2[l], mod, l, 3)
        u = _matmul(h2, w_mlp_in, BF16, l, D_FF, relu2=True, name="mlp_in")
        n_kb = D_FF // D_MODEL
        for kb in range(n_kb):
            last = l == DEPTH - 1 and kb == n_kb - 1
            x = _matmul_residual(u, w_mlp_out, x, mod, l, 5, kb, "mlp_out", split_out=last)

    y_prompt = x[0].reshape(BATCH, SEQ, D_MODEL)
    y_sample = x[1].reshape(DEC_BATCH, DEC_SEQ, D_MODEL)
    return (y_prompt, y_sample, jnp.stack(caches, axis=1), jnp.stack(hgrn_states, axis=1),
            jnp.stack(gla_states, axis=1))
```

```python
import functools

import numpy as np
import jax
import jax.numpy as jnp
from jax import lax
from jax.experimental import pallas as pl
from jax.experimental.pallas import tpu as pltpu

F32 = jnp.float32
BF16 = jnp.bfloat16

D_MODEL = 4096
BATCH = 32
SEQ = 256
DEPTH = 2
DEC_BATCH = 2
DEC_SEQ = 1024
PAST_LEN = 512
GRID_W = 64
A_HEADS = 8
A_DK = 128
A_DV = 128
B_HEADS = 4
B_DK = 128
B_DV = 256
GLA_RANK = 16
GLA_GATE_NORM = 16.0
C_HEADS = 16
Q_LORA = 1024
KV_LORA = 512
QK_NOPE = 128
QK_ROPE = 64
C_DV = 128
QK_DIM = QK_NOPE + QK_ROPE
ROPE_AXIS = QK_ROPE // 2
ROPE_BASE = 10000.0
N_MOD = 6
D_FF = 4 * D_MODEL
EPS = 1e-6

N_PROMPT = BATCH * SEQ
N_SAMPLE = DEC_BATCH * DEC_SEQ
N_TOK = N_PROMPT + N_SAMPLE
MOD_ROWS = 8

Z_MAIN = 8192
ZS_CQ = 0
ZS_CKV = ZS_CQ + Q_LORA
ZS_KR = ZS_CKV + KV_LORA
ZS_LR = ZS_KR + 128
ZS_COLS = ZS_LR + 128
ROW_ALIGN = 32

LANES = 128
MXU_COLS = 256
V7X_VMEM_BYTES = 64 * 1024 * 1024
VMEM_BIG = 60 * 1024 * 1024
VMEM_MID = 40 * 1024 * 1024

LOG2_E = 1.4426950408889634
CH = 32
SB = 8


def _params(sem, vmem=VMEM_MID):
    return pltpu.CompilerParams(dimension_semantics=sem, vmem_limit_bytes=vmem)


def _mod_row(i, tm):
    return jnp.where(i < N_PROMPT // tm, 0, 1 + (i - N_PROMPT // tm) // (DEC_SEQ // tm))


def _sigmoid(x):
    return 1.0 / (1.0 + jnp.exp(-x))


def _log_sigmoid(x):
    return jnp.minimum(x, 0.0) - jnp.log1p(jnp.exp(-jnp.abs(x)))


def _mod_kernel(cond_ref, w_ref, b_ref, o_ref):
    c = cond_ref[...]
    s = (c * _sigmoid(c)).astype(BF16)
    o_ref[...] = jnp.dot(s, w_ref[...].astype(BF16), preferred_element_type=F32) + b_ref[...]


def _modulation(cond, w_mod, b_mod):
    tn = 512
    n = N_MOD * D_MODEL
    return pl.pallas_call(
        _mod_kernel,
        grid=(DEPTH, n // tn),
        in_specs=[
            pl.BlockSpec((MOD_ROWS, D_MODEL), lambda l, j: (0, 0)),
            pl.BlockSpec((None, D_MODEL, tn), lambda l, j: (l, 0, j)),
            pl.BlockSpec((None, 1, tn), lambda l, j: (l, 0, j)),
        ],
        out_specs=pl.BlockSpec((None, MOD_ROWS, tn), lambda l, j: (l, 0, j)),
        out_shape=jax.ShapeDtypeStruct((DEPTH, MOD_ROWS, n), F32),
        compiler_params=_params(("parallel", "parallel")),
        name="modulation",
    )(cond, w_mod, b_mod.reshape(DEPTH, 1, n))


def _group_specs(tm, tn, row_of, col_of):
    npb = N_PROMPT // tm
    last = npb - 1
    return [pl.BlockSpec((tm, tn), lambda *g: (jnp.minimum(row_of(*g), last), col_of(*g))),
            pl.BlockSpec((tm, tn), lambda *g: (jnp.maximum(row_of(*g) - npb, 0), col_of(*g)))]


def _norm_mod_kernel(*refs):
    *x_refs, g_ref, sh_ref, sc_ref, o_ref = refs

    def run(x_ref):
        x = x_ref[...]
        ms = jnp.mean(x * x, axis=-1, keepdims=True)
        y = x * lax.rsqrt(ms + EPS) * g_ref[...]
        o_ref[...] = (y * (1.0 + sc_ref[...]) + sh_ref[...]).astype(BF16)

    if len(x_refs) == 2:
        in_prompt = pl.program_id(0) < N_PROMPT // o_ref.shape[0]
        pl.when(in_prompt)(lambda: run(x_refs[0]))
        pl.when(jnp.logical_not(in_prompt))(lambda: run(x_refs[1]))
    else:
        run(x_refs[0])


def _norm_mod(x, gain, mod, layer, shift_chunk):
    tm = 256
    base = layer * MOD_ROWS
    if isinstance(x, tuple):
        x_specs = _group_specs(tm, D_MODEL, lambda i: i, lambda i: 0)
    else:
        x_specs, x = [pl.BlockSpec((tm, D_MODEL), lambda i: (i, 0))], (x,)
    return pl.pallas_call(
        _norm_mod_kernel,
        grid=(N_TOK // tm,),
        in_specs=x_specs + [
            pl.BlockSpec((1, D_MODEL), lambda i: (0, 0)),
            pl.BlockSpec((None, 1, D_MODEL), lambda i: (base + _mod_row(i, tm), 0, shift_chunk)),
            pl.BlockSpec((None, 1, D_MODEL), lambda i: (base + _mod_row(i, tm), 0, shift_chunk + 1)),
        ],
        out_specs=pl.BlockSpec((tm, D_MODEL), lambda i: (i, 0)),
        out_shape=jax.ShapeDtypeStruct((N_TOK, D_MODEL), BF16),
        compiler_params=_params(("parallel",)),
        name="norm_mod",
    )(*x, gain.reshape(1, D_MODEL), mod, mod)


def _cast_weight(w_ref, wb_ref):
    @pl.when(pl.program_id(1) == 0)
    def _():
        wb_ref[...] = w_ref[...].astype(BF16)
    return wb_ref


TRANS_B = (((1,), (1,)), ((), ()))


def _mm_kernel(x_ref, w_ref, o_ref, wb_ref, *, relu2, transposed):
    w_ref = _cast_weight(w_ref, wb_ref)
    if transposed:
        acc = lax.dot_general(x_ref[...], w_ref[...], TRANS_B, preferred_element_type=F32)
    else:
        acc = jnp.dot(x_ref[...], w_ref[...], preferred_element_type=F32)
    if relu2:
        acc = jnp.square(jnp.maximum(acc, 0.0))
    o_ref[...] = acc.astype(o_ref.dtype)


def _matmul(x, w, out_dtype, layer, n_cols, relu2=False, transposed=False, name="matmul"):
    tm, tn = 512, 1024
    m, k = x.shape
    if transposed:
        w_spec = pl.BlockSpec((None, tn, k), lambda j, i: (layer, j, 0))
    else:
        w_spec = pl.BlockSpec((None, k, tn), lambda j, i: (layer, 0, j))
    return pl.pallas_call(
        functools.partial(_mm_kernel, relu2=relu2, transposed=transposed),
        grid=(n_cols // tn, m // tm),
        in_specs=[pl.BlockSpec((tm, k), lambda j, i: (i, 0)), w_spec],
        out_specs=pl.BlockSpec((tm, tn), lambda j, i: (i, j)),
        out_shape=jax.ShapeDtypeStruct((m, n_cols), out_dtype),
        scratch_shapes=[pltpu.VMEM((tn, k) if transposed else (k, tn), BF16)],
        compiler_params=_params(("parallel", "arbitrary"), VMEM_BIG),
        name=name,
    )(x, w)


def _cast_rows_kernel(w_ref, o_ref):
    o_ref[...] = w_ref[0].astype(BF16)


def _cast_rows(w, layer, row0, n_rows, block_rows, name):
    k = w.shape[2]
    assert row0 % ROW_ALIGN == 0 and block_rows % ROW_ALIGN == 0 and n_rows % block_rows == 0
    start = lambda r: pl.multiple_of(ROW_ALIGN * (row0 // ROW_ALIGN + (block_rows // ROW_ALIGN) * r),
                                     ROW_ALIGN)
    return pl.pallas_call(
        _cast_rows_kernel,
        grid=(n_rows // block_rows,),
        in_specs=[pl.BlockSpec((pl.Element(1), pl.Element(block_rows), pl.Element(k)),
                               lambda r: (layer, start(r), 0))],
        out_specs=pl.BlockSpec((block_rows, k), lambda r: (r, 0)),
        out_shape=jax.ShapeDtypeStruct((n_rows, k), BF16),
        compiler_params=_params(("parallel",)),
        name=name,
    )(w)


def _in_small_kernel(x_ref, w_ref, o_ref):
    x = x_ref[...]
    tm = x.shape[0]
    n_lr, n_kv = 2 * GLA_RANK, KV_LORA + QK_ROPE
    part = lambda r0, rows: lax.dot_general(x, w_ref[r0:r0 + rows, :], TRANS_B,
                                            preferred_element_type=F32)
    o_ref[:, ZS_LR:ZS_LR + n_lr] = part(0, n_lr)
    o_ref[:, ZS_LR + n_lr:] = jnp.zeros((tm, ZS_COLS - ZS_LR - n_lr), F32)
    o_ref[:, ZS_CQ:ZS_CQ + Q_LORA] = part(n_lr, Q_LORA)
    o_ref[:, ZS_CKV:ZS_CKV + n_kv] = part(n_lr + Q_LORA, n_kv)
    o_ref[:, ZS_CKV + n_kv:ZS_LR] = jnp.zeros((tm, ZS_LR - ZS_CKV - n_kv), F32)


def _in_small(h, w_small):
    tm = 512
    return pl.pallas_call(
        _in_small_kernel,
        grid=(N_TOK // tm,),
        in_specs=[pl.BlockSpec((tm, D_MODEL), lambda i: (i, 0)),
                  pl.BlockSpec(w_small.shape, lambda i: (0, 0))],
        out_specs=pl.BlockSpec((tm, ZS_COLS), lambda i: (i, 0)),
        out_shape=jax.ShapeDtypeStruct((N_TOK, ZS_COLS), F32),
        compiler_params=_params(("parallel",), VMEM_BIG),
        name="in_proj_small",
    )(h, w_small)


def _mm_res_kernel(*refs, n_res, n_out):
    x_ref, w_ref = refs[:2]
    res_refs = refs[2:2 + n_res]
    g_ref = refs[2 + n_res]
    o_refs = refs[3 + n_res:3 + n_res + n_out]
    w_ref = _cast_weight(w_ref, refs[3 + n_res + n_out])
    tm, tn = o_refs[0].shape
    in_prompt = pl.program_id(1) < N_PROMPT // tm
    acc_ref = refs[-1] if n_out == 2 else o_refs[0]
    x = x_ref[...]
    for c0 in range(0, tn, MXU_COLS):
        cols = slice(c0, c0 + MXU_COLS)
        part = g_ref[:, cols] * jnp.dot(x, w_ref[:, cols], preferred_element_type=F32)
        if n_res == 2:
            res = jnp.where(in_prompt, res_refs[0][:, cols], res_refs[1][:, cols])
        else:
            res = res_refs[0][:, cols]
        acc_ref[:, cols] = res + part
    if n_out == 2:
        @pl.when(in_prompt)
        def _():
            o_refs[0][...] = acc_ref[...]

        @pl.when(jnp.logical_not(in_prompt))
        def _():
            o_refs[1][...] = acc_ref[...]


def _matmul_residual(a, w, res, mod, layer, gate_chunk, k_block, name, split_out=False):
    res = res if isinstance(res, tuple) else (res,)
    tm, tn, tk = 512, (512 if split_out or len(res) == 2 else 1024), D_MODEL
    m = a.shape[0]
    n = w.shape[2]
    base = layer * MOD_ROWS
    cpc = D_MODEL // tn
    whole = pl.BlockSpec((tm, tn), lambda j, i: (i, j))
    pair = _group_specs(tm, tn, lambda j, i: i, lambda j, i: j)
    if split_out:
        out_specs = pair
        out_shape = [jax.ShapeDtypeStruct((N_PROMPT, n), F32), jax.ShapeDtypeStruct((N_SAMPLE, n), F32)]
    else:
        out_specs, out_shape = whole, jax.ShapeDtypeStruct((m, n), F32)
    return pl.pallas_call(
        functools.partial(_mm_res_kernel, n_res=len(res), n_out=2 if split_out else 1),
        grid=(n // tn, m // tm),
        in_specs=[
            pl.BlockSpec((tm, tk), lambda j, i: (i, k_block)),
            pl.BlockSpec((None, tk, tn), lambda j, i: (layer, k_block, j)),
            *(pair if len(res) == 2 else [whole]),
            pl.BlockSpec((None, 1, tn),
                         lambda j, i: (base + _mod_row(i, tm), 0, gate_chunk * cpc + j)),
        ],
        out_specs=out_specs,
        out_shape=out_shape,
        scratch_shapes=[pltpu.VMEM((tk, tn), BF16)] + ([pltpu.VMEM((tm, tn), F32)] if split_out else []),
        compiler_params=_params(("parallel", "arbitrary"), VMEM_BIG),
        name=name,
    )(a, w, *res, mod)


def _merge_kernel(h_ref, wga_ref, wgb_ref, wgc_ref, oa_ref, wa_ref, ob_ref, wb_ref, oc_ref, wc_ref,
                  o_ref):
    h = h_ref[...]

    def branch(wg_ref, o_r, w_r):
        g = lax.dot_general(h, wg_ref[...], TRANS_B, preferred_element_type=F32)
        p = jnp.dot(o_r[...], w_r[...], preferred_element_type=F32)
        return _sigmoid(g) * p

    acc = branch(wga_ref, oa_ref, wa_ref)
    acc = acc + branch(wgb_ref, ob_ref, wb_ref)
    acc = acc + branch(wgc_ref, oc_ref, wc_ref)
    o_ref[...] = acc.astype(BF16)


def _merge(h, wg, o_a, w_a, o_b, w_b, o_c, w_c, layer):
    tm, tn = 512, 512
    nj = D_MODEL // tn
    act = lambda width: pl.BlockSpec((tm, width), lambda j, i: (i, 0))
    wcol = lambda rows: pl.BlockSpec((None, rows, tn), lambda j, i: (layer, 0, j))
    gate = lambda b: pl.BlockSpec((tn, D_MODEL), lambda j, i: (b * nj + j, 0))
    return pl.pallas_call(
        _merge_kernel,
        grid=(nj, N_TOK // tm),
        in_specs=[act(D_MODEL), gate(0), gate(1), gate(2),
                  act(o_a.shape[1]), wcol(w_a.shape[1]),
                  act(o_b.shape[1]), wcol(w_b.shape[1]),
                  act(o_c.shape[1]), wcol(w_c.shape[1])],
        out_specs=pl.BlockSpec((tm, tn), lambda j, i: (i, j)),
        out_shape=jax.ShapeDtypeStruct((N_TOK, D_MODEL), BF16),
        compiler_params=_params(("parallel", "parallel"), VMEM_BIG),
        name="merge",
    )(h, wg, wg, wg, o_a, w_a, o_b, w_b, o_c, w_c)


PAD = 8


def _shifted(ref, n_rows, d, reverse):
    return ref[pl.ds(PAD + d if reverse else PAD - d, n_rows), :]


def _seg_cumsum(g, rowc, reverse, c_scr):
    n_rows = g.shape[0]
    x = g
    for sh in (1, 2, 4, 8, 16):
        mask = rowc < CH - sh if reverse else rowc >= sh
        if sh < PAD:
            c_scr[pl.ds(PAD, n_rows), :] = x
            moved = _shifted(c_scr, n_rows, sh, reverse)
        else:
            moved = pltpu.roll(x, n_rows - sh if reverse else sh, 0)
        x = x + jnp.where(mask, moved, 0.0)
    return x


def _scan_dir(q, k, v, g, s0t, reverse, scratch, accumulate, want_state):
    o_scr, k_scr, c_scr, v_scr = scratch
    n_rows, dk = q.shape
    n = n_rows // CH
    rowc = lax.broadcasted_iota(jnp.int32, (n_rows, dk), 0) & (CH - 1)
    cum = _seg_cumsum(g * LOG2_E, rowc, reverse, c_scr)
    c_scr[pl.ds(PAD, n_rows), :] = cum
    k_scr[pl.ds(PAD, n_rows), :] = k

    def rows_of(r, count):
        return jnp.broadcast_to(cum[r:r + 1, :], (count, dk))

    def chunk_row(r):
        return jnp.concatenate([rows_of(c * CH + r, CH) for c in range(n)], axis=0)

    last_row = 0 if reverse else CH - 1
    last_b = chunk_row(last_row)
    qt = (q * jnp.exp2(cum)).astype(BF16)
    kh = (k * jnp.exp2(last_b - cum)).astype(BF16)
    vb = v.astype(BF16)

    rows = rowc & (SB - 1)
    o_diag = jnp.sum(q * k, axis=-1, keepdims=True) * v
    for d in range(1, SB):
        mask = rows < SB - d if reverse else rows >= d
        kd = _shifted(k_scr, n_rows, d, reverse)
        cd = _shifted(c_scr, n_rows, d, reverse)
        vd = _shifted(v_scr, n_rows, d, reverse)
        e = jnp.exp2(jnp.where(mask, cum - cd, -jnp.inf))
        o_diag = o_diag + jnp.sum(q * kd * e, axis=-1, keepdims=True) * vd
    if accumulate:
        o_scr[...] += o_diag
    else:
        o_scr[...] = o_diag

    sub = rowc // SB
    nsb = CH // SB
    if reverse:
        own = [rows_of(c * CH + SB * min(i + 1, nsb - 1), SB) for c in range(n) for i in range(nsb)]
        has_piv = sub < nsb - 1
    else:
        own = [rows_of(c * CH + SB * max(i, 1) - 1, SB) for c in range(n) for i in range(nsb)]
        has_piv = sub >= 1
    qe = q * jnp.exp2(jnp.where(has_piv, cum - jnp.concatenate(own, axis=0), -jnp.inf))
    qs, ks = [], []
    for pr in range(SB, CH, SB):
        if reverse:
            piv, qmask, kmask = chunk_row(pr), sub == pr // SB - 1, rowc >= pr
        else:
            piv, qmask, kmask = chunk_row(pr - 1), sub == pr // SB, rowc < pr
        qs.append(jnp.where(qmask, qe, 0.0).astype(BF16))
        ks.append((k * jnp.exp2(jnp.where(kmask, piv - cum, -jnp.inf))).astype(BF16))
    qcat = jnp.concatenate(qs, axis=-1)
    kcat = jnp.concatenate(ks, axis=-1)

    trans_a = (((0,), (0,)), ((), ()))
    order = list(range(n - 1, -1, -1) if reverse else range(n))
    rows = [slice(c * CH, (c + 1) * CH) for c in order]
    n_upd = n if want_state else n - 1
    uts = [lax.dot_general(vb[sl], kh[sl], trans_a, preferred_element_type=F32) for sl in rows[:n_upd]]
    atts = [lax.dot_general(qcat[sl], kcat[sl], TRANS_B, preferred_element_type=F32).astype(BF16)
            for sl in rows]
    o_off = [jnp.dot(a, vb[sl], preferred_element_type=F32) for a, sl in zip(atts, rows)]
    st = s0t
    states = []
    for idx, c in enumerate(order):
        states.append(st.astype(BF16))
        if idx < n_upd:
            r = c * CH + last_row
            st = st * jnp.exp2(cum[r:r + 1, :]) + uts[idx]
    for sl, s_in, off in zip(rows, states, o_off):
        o_scr[sl, :] += lax.dot_general(qt[sl], s_in, TRANS_B, preferred_element_type=F32) + off
    return st


def _finish_scan(o_scr, gain_ref, zg_ref, o_ref):
    o = o_scr[...]
    ms = jnp.mean(o * o, axis=-1, keepdims=True)
    zg = zg_ref[...]
    o_ref[...] = (o * lax.rsqrt(ms + EPS) * gain_ref[...] * (zg * _sigmoid(zg))).astype(BF16)


def _split_scan_refs(refs, n_in, sample):
    pos = n_in
    s0_ref = None
    if sample:
        s0_ref = refs[pos]
        pos += 2
    o_ref = refs[pos]
    st_ref = None if sample else refs[pos + 1]
    scratch = refs[pos + (1 if sample else 2):]
    return refs[:n_in], s0_ref, o_ref, st_ref, scratch


def _prep_scan_scratch(scratch, v):
    _, k_scr, c_scr, v_scr = scratch
    n_rows = v.shape[0]
    for ref in (k_scr, c_scr, v_scr):
        zero = jnp.zeros((PAD, ref.shape[1]), F32)
        ref[pl.ds(0, PAD), :] = zero
        ref[pl.ds(PAD + n_rows, PAD), :] = zero
    v_scr[pl.ds(PAD, n_rows), :] = v


def _scan_scratch(seq, dk, dv):
    return [pltpu.VMEM((seq, dv), F32), pltpu.VMEM((seq + 2 * PAD, dk), F32),
            pltpu.VMEM((seq + 2 * PAD, dk), F32), pltpu.VMEM((seq + 2 * PAD, dv), F32)]


def _hgrn_kernel(*refs, layer, sample):
    ins, s0_ref, o_ref, st_ref, scratch = _split_scan_refs(refs, 7, sample)
    lbl_ref, q_ref, ff_ref, fb_ref, i_ref, zg_ref, gain_ref = ins
    has_state, want_state = sample, not sample

    logits = [lbl_ref[l] for l in range(DEPTH)]
    mx = functools.reduce(jnp.maximum, logits)
    ex = [jnp.exp(x - mx) for x in logits]
    tot = functools.reduce(lambda a, b: a + b, ex)
    probs = [e / tot for e in ex]
    cum_first = probs[0]
    cum_l = functools.reduce(lambda a, b: a + b, probs[:layer + 1])
    lb = cum_l - cum_first

    q = q_ref[...]
    v = i_ref[...]
    _prep_scan_scratch(scratch, v)
    for d, (z_ref, reverse) in enumerate(((ff_ref, False), (fb_ref, True))):
        lbd = lb[d:d + 1, :]
        zf = z_ref[...]
        a = jnp.log(lbd)
        b = jnp.log1p(-lbd) + _log_sigmoid(zf)
        delta = a - b
        log_f = jnp.where(delta != delta, a + b,
                          jnp.maximum(a, b) + jnp.log1p(jnp.exp(-jnp.abs(delta))))
        one_minus_f = (1.0 - lbd) * _sigmoid(-zf)
        if has_state:
            s0t = s0_ref[d].T
        else:
            s0t = jnp.zeros((A_DV, A_DK), F32)
        st = _scan_dir(q, one_minus_f, v, log_f, s0t, reverse, scratch, d == 1, want_state)
        if want_state:
            st_ref[d] = st.T
    _finish_scan(scratch[0], gain_ref, zg_ref, o_ref)


def _scan_call(body, name, z_specs, args, heads, dk, dv, state, layer, o_prev):
    sample = o_prev is not None
    seq, nb = (DEC_SEQ, DEC_BATCH) if sample else (SEQ, BATCH)
    row0 = N_PROMPT // seq if sample else 0
    in_specs = [spec(seq, row0) for spec in z_specs]
    args = list(args)
    aliases = {}
    if sample:
        in_specs += [pl.BlockSpec((None, None, 2, None, dk, dv), lambda b, h: (b, layer, 0, h, 0, 0)),
                     pl.BlockSpec(memory_space=pl.ANY)]
        aliases = {len(args) + 1: 0}
        args += [state, o_prev]
    out_specs = [pl.BlockSpec((seq, dv), lambda b, h: (row0 + b, h))]
    out_shape = [jax.ShapeDtypeStruct((N_TOK, heads * dv), BF16)]
    if not sample:
        out_specs.append(pl.BlockSpec((None, 2, None, dk, dv), lambda b, h: (b, 0, h, 0, 0)))
        out_shape.append(jax.ShapeDtypeStruct((nb, 2, heads, dk, dv), F32))
    return pl.pallas_call(
        functools.partial(body, sample=sample),
        grid=(nb, heads),
        in_specs=in_specs,
        out_specs=out_specs,
        out_shape=out_shape,
        input_output_aliases=aliases,
        scratch_shapes=_scan_scratch(seq, dk, dv),
        compiler_params=_params(("parallel", "parallel")),
        name=name + ("_sample" if sample else "_prompt"),
    )(*args)


def _hgrn(z, lb_logits, out_gain, state, layer, o_prev=None):
    zcol = lambda off: (lambda seq, row0: pl.BlockSpec((seq, A_DK), lambda b, h: (row0 + b, off + h)))
    fixed = lambda spec: (lambda seq, row0: spec)
    z_specs = [
        fixed(pl.BlockSpec((DEPTH, 2, A_DK), lambda b, h: (0, 0, h))),
        zcol(0), zcol(A_HEADS), zcol(2 * A_HEADS), zcol(3 * A_HEADS), zcol(4 * A_HEADS),
        fixed(pl.BlockSpec((1, A_DV), lambda b, h: (0, 0))),
    ]
    args = [lb_logits, z, z, z, z, z, out_gain.reshape(1, A_DV)]
    return _scan_call(functools.partial(_hgrn_kernel, layer=layer), "hgrn", z_specs, args,
                      A_HEADS, A_DK, A_DV, state, layer, o_prev)


def _gla_kernel(*refs, sample):
    ins, s0_ref, o_ref, st_ref, scratch = _split_scan_refs(refs, 8, sample)
    q_ref, k_ref, v_ref, zr_ref, lr_ref, wg_ref, bg_ref, gain_ref = ins
    has_state, want_state = sample, not sample

    q = q_ref[...] * (B_DK ** -0.5)
    k = k_ref[...]
    v = v_ref[...]
    _prep_scan_scratch(scratch, v)
    lr = lr_ref[...].astype(BF16)
    for d, reverse in enumerate((False, True)):
        logit = jnp.dot(lr, wg_ref[d], preferred_element_type=F32) + bg_ref[d]
        g = _log_sigmoid(logit) / GLA_GATE_NORM
        if has_state:
            s0t = s0_ref[d].T
        else:
            s0t = jnp.zeros((B_DV, B_DK), F32)
        st = _scan_dir(q, k, v, g, s0t, reverse, scratch, d == 1, want_state)
        if want_state:
            st_ref[d] = st.T
    _finish_scan(scratch[0], gain_ref, zr_ref, o_ref)


def _gla(z, zs, w_gate_ext, b_gate, out_gain, state, layer, o_prev=None):
    zcol = lambda width, col: (lambda seq, row0: pl.BlockSpec(
        (seq, width), lambda b, h: (row0 + b, col // width + h)))
    fixed = lambda spec: (lambda seq, row0: spec)
    z_specs = [
        zcol(B_DK, 5120), zcol(B_DK, 5632), zcol(B_DV, 6144), zcol(B_DV, 7168),
        lambda seq, row0: pl.BlockSpec((seq, LANES), lambda b, h: (row0 + b, ZS_LR // LANES)),
        fixed(pl.BlockSpec((2, LANES, B_DK), lambda b, h: (0, 0, h))),
        fixed(pl.BlockSpec((2, 1, B_DK), lambda b, h: (0, 0, h))),
        fixed(pl.BlockSpec((1, B_DV), lambda b, h: (0, 0))),
    ]
    args = [z, z, z, z, zs, w_gate_ext, b_gate.reshape(2, 1, B_HEADS * B_DK),
            out_gain.reshape(1, B_DV)]
    return _scan_call(_gla_kernel, "gla", z_specs, args, B_HEADS, B_DK, B_DV, state, layer, o_prev)


def _mla_prep_kernel(cq_ref, ckv_ref, kr_ref, gq_ref, gkv_ref, qlat_ref, ckvk_ref):
    cq = cq_ref[...]
    qlat_ref[...] = (cq * lax.rsqrt(jnp.mean(cq * cq, axis=-1, keepdims=True) + EPS)
                     * gq_ref[...]).astype(BF16)
    ckv = ckv_ref[...]
    ckvk_ref[:, :KV_LORA] = (ckv * lax.rsqrt(jnp.mean(ckv * ckv, axis=-1, keepdims=True) + EPS)
                             * gkv_ref[...])
    ckvk_ref[:, KV_LORA:] = kr_ref[:, :QK_ROPE]


def _mla_prep(z, q_gain, kv_gain):
    tm = 512
    return pl.pallas_call(
        _mla_prep_kernel,
        grid=(N_TOK // tm,),
        in_specs=[
            pl.BlockSpec((tm, Q_LORA), lambda i: (i, ZS_CQ // Q_LORA)),
            pl.BlockSpec((tm, KV_LORA), lambda i: (i, ZS_CKV // KV_LORA)),
            pl.BlockSpec((tm, LANES), lambda i: (i, ZS_KR // LANES)),
            pl.BlockSpec((1, Q_LORA), lambda i: (0, 0)),
            pl.BlockSpec((1, KV_LORA), lambda i: (0, 0)),
        ],
        out_specs=[pl.BlockSpec((tm, Q_LORA), lambda i: (i, 0)),
                   pl.BlockSpec((tm, KV_LORA + QK_ROPE), lambda i: (i, 0))],
        out_shape=[jax.ShapeDtypeStruct((N_TOK, Q_LORA), BF16),
                   jax.ShapeDtypeStruct((N_TOK, KV_LORA + QK_ROPE), F32)],
        compiler_params=_params(("parallel",)),
        name="mla_prep",
    )(z, z, z, q_gain.reshape(1, Q_LORA), kv_gain.reshape(1, KV_LORA))


def _swap_halves(t, lane):
    width = t.shape[-1]
    return jnp.where((lane & 31) < 16, pltpu.roll(t, width - 16, 1), pltpu.roll(t, 16, 1))


def _qup_kernel(x_ref, w_ref, gn_ref, gr_ref, c_ref, s_ref, qn_ref, qr_ref):
    acc = jnp.dot(x_ref[...], w_ref[...], preferred_element_type=F32)
    nope_w = C_HEADS * QK_NOPE
    tm = acc.shape[0]
    lane = lax.broadcasted_iota(jnp.int32, (tm, LANES), 1)
    low = lane < QK_ROPE
    gn = gn_ref[...]
    gr = gr_ref[...]
    cos = c_ref[...]
    sin = s_ref[...]
    for p in range(C_HEADS // 2):
        r2 = acc[:, nope_w + LANES * p: nope_w + LANES * (p + 1)]
        sq = r2 * r2
        ss_rope = (jnp.sum(jnp.where(low, sq, 0.0), axis=-1, keepdims=True),
                   jnp.sum(jnp.where(low, 0.0, sq), axis=-1, keepdims=True))
        inv = []
        for hh in range(2):
            h = 2 * p + hh
            nh = acc[:, QK_NOPE * h: QK_NOPE * (h + 1)]
            ssn = jnp.sum(nh * nh, axis=-1, keepdims=True)
            r = lax.rsqrt((ssn + ss_rope[hh]) / QK_DIM + EPS)
            qn_ref[:, QK_NOPE * h: QK_NOPE * (h + 1)] = (nh * r * gn).astype(BF16)
            inv.append(r)
        t = r2 * jnp.where(low, inv[0], inv[1]) * gr
        qr_ref[:, LANES * p: LANES * (p + 1)] = (t * cos + _swap_halves(t, lane) * sin).astype(BF16)


def _qup(qlat, w, gn, gr, cos, sin):
    tm = 512
    n_rows = qlat.shape[0]
    row = lambda width: pl.BlockSpec((tm, width), lambda i: (i, 0))
    full = lambda a: pl.BlockSpec(a.shape, lambda i: (0, 0))
    return pl.pallas_call(
        _qup_kernel,
        grid=(n_rows // tm,),
        in_specs=[row(Q_LORA), full(w), full(gn), full(gr), row(LANES), row(LANES)],
        out_specs=[row(C_HEADS * QK_NOPE), row(C_HEADS * QK_ROPE)],
        out_shape=[jax.ShapeDtypeStruct((n_rows, C_HEADS * QK_NOPE), BF16),
                   jax.ShapeDtypeStruct((n_rows, C_HEADS * QK_ROPE), BF16)],
        compiler_params=_params(("parallel",)),
        name="mla_q_up",
    )(qlat, w, gn, gr, cos, sin)


def _kvup_kernel(*refs, sample):
    if sample:
        cache_ref, x_ref, w_ref, gn_ref, gr_ref, c_ref, s_ref, kn_ref, kr_ref, v_ref = refs
        from_cache = pl.program_id(1) == 0
        x = jnp.where(from_cache, cache_ref[...], x_ref[...])
    else:
        x_ref, w_ref, gn_ref, gr_ref, kn_ref, kr_ref, v_ref = refs
        x = x_ref[...]
    acc = jnp.dot(x[:, :KV_LORA].astype(BF16), w_ref[...], preferred_element_type=F32)
    nope_w = C_HEADS * QK_NOPE
    v_ref[...] = acc[:, nope_w:].astype(BF16)
    tm = acc.shape[0]
    lane = lax.broadcasted_iota(jnp.int32, (tm, LANES), 1)
    low = lane < QK_ROPE
    kr = x[:, KV_LORA:]
    ss_rope = jnp.sum(kr * kr, axis=-1, keepdims=True)
    rot = jnp.concatenate([kr, kr], axis=-1) * gr_ref[...]
    if sample:
        turned = rot * c_ref[...] + _swap_halves(rot, lane) * s_ref[...]
        rot = jnp.where(from_cache, rot, turned)
    gn = gn_ref[...]
    for p in range(C_HEADS // 2):
        inv = []
        for hh in range(2):
            h = 2 * p + hh
            nh = acc[:, QK_NOPE * h: QK_NOPE * (h + 1)]
            ssn = jnp.sum(nh * nh, axis=-1, keepdims=True)
            r = lax.rsqrt((ssn + ss_rope) / QK_DIM + EPS)
            kn_ref[:, QK_NOPE * h: QK_NOPE * (h + 1)] = (nh * r * gn).astype(BF16)
            inv.append(r)
        kr_ref[:, LANES * p: LANES * (p + 1)] = (rot * jnp.where(low, inv[0], inv[1])).astype(BF16)


def _kvup(ckvk, w, gn, gr, cache=None, layer=None, cos=None, sin=None):
    tm = PAST_LEN
    sample = cache is not None
    width = KV_LORA + QK_ROPE
    if sample:
        per_seq = 1 + DEC_SEQ // tm
        grid = (DEC_BATCH, per_seq)
        row_blk = lambda b, t: b * per_seq + t
        new_blk = lambda b, t: N_PROMPT // tm + b * (per_seq - 1) + jnp.maximum(t - 1, 0)
        full = lambda a: pl.BlockSpec(a.shape, lambda b, t: (0, 0))
        table = pl.BlockSpec((tm, LANES), lambda b, t: (jnp.maximum(t - 1, 0), 0))
        in_specs = [pl.BlockSpec((None, None, tm, width), lambda b, t: (b, layer, 0, 0)),
                    pl.BlockSpec((tm, width), lambda b, t: (new_blk(b, t), 0)),
                    full(w), full(gn), full(gr), table, table]
        args = (cache, ckvk, w, gn, gr, cos, sin)
        row = lambda wd: pl.BlockSpec((tm, wd), lambda b, t: (row_blk(b, t), 0))
        n_rows = DEC_BATCH * per_seq * tm
    else:
        grid = (N_PROMPT // tm,)
        full = lambda a: pl.BlockSpec(a.shape, lambda i: (0, 0))
        in_specs = [pl.BlockSpec((tm, width), lambda i: (i, 0)), full(w), full(gn), full(gr)]
        args = (ckvk, w, gn, gr)
        row = lambda wd: pl.BlockSpec((tm, wd), lambda i: (i, 0))
        n_rows = N_PROMPT
    return pl.pallas_call(
        functools.partial(_kvup_kernel, sample=sample),
        grid=grid,
        in_specs=in_specs,
        out_specs=[row(C_HEADS * QK_NOPE), row(C_HEADS * QK_ROPE), row(C_HEADS * C_DV)],
        out_shape=[jax.ShapeDtypeStruct((n_rows, C_HEADS * QK_NOPE), BF16),
                   jax.ShapeDtypeStruct((n_rows, C_HEADS * QK_ROPE), BF16),
                   jax.ShapeDtypeStruct((n_rows, C_HEADS * C_DV), BF16)],
        compiler_params=_params(("parallel",) * len(grid)),
        name="mla_kv_up_sample" if sample else "mla_kv_up_prompt",
    )(*args)


def _attn_kernel(qn_ref, qr_ref, kn_ref, kr_ref, v_ref, *rest, n_seq):
    o_ref = rest[-1]
    lq = qr_ref.shape[0] // n_seq
    lk = kr_ref.shape[0] // n_seq
    lane = lax.broadcasted_iota(jnp.int32, (lq, LANES), 1)
    blocks = [(slice(s * lq, (s + 1) * lq), slice(s * lk, (s + 1) * lk),
               slice(QK_NOPE * hh, QK_NOPE * (hh + 1)), hh)
              for s in range(n_seq) for hh in range(2)]
    scores = []
    for qrows, krows, cols, hh in blocks:
        qr = qr_ref[qrows, :]
        own = (lane < QK_ROPE) if hh == 0 else (lane >= QK_ROPE)
        qrm = jnp.where(own, qr, jnp.zeros_like(qr))
        s = lax.dot_general(qn_ref[qrows, cols], kn_ref[krows, cols], TRANS_B,
                            preferred_element_type=F32)
        s = s + lax.dot_general(qrm, kr_ref[krows, :], TRANS_B, preferred_element_type=F32)
        scores.append(s * (QK_DIM ** -0.5 * LOG2_E))
    tops = [jnp.max(s, axis=-1, keepdims=True) for s in scores]
    exps = [jnp.exp2(s - m) for s, m in zip(scores, tops)]
    sums = [jnp.sum(e, axis=-1, keepdims=True) for e in exps]
    probs = [(e / t).astype(BF16) for e, t in zip(exps, sums)]
    for p, (qrows, krows, cols, _) in zip(probs, blocks):
        o_ref[qrows, cols] = jnp.dot(p, v_ref[krows, cols], preferred_element_type=F32).astype(BF16)


def _attention(qn, qr, kn, kr, v, o_prev=None):
    sample = o_prev is not None
    if sample:
        q_row0, tq, tk, n_seq = N_PROMPT, SEQ, PAST_LEN + DEC_SEQ, 1
        grid = (DEC_BATCH, C_HEADS // 2, DEC_SEQ // tq)
    else:
        n_seq = 4
        q_row0, tq, tk = 0, n_seq * SEQ, n_seq * SEQ
        grid = (BATCH // n_seq, C_HEADS // 2, 1)
    nq = grid[2]
    qspec = lambda width: pl.BlockSpec((tq, width), lambda b, p, i: (q_row0 // tq + b * nq + i, p))
    kspec = lambda width: pl.BlockSpec((tk, width), lambda b, p, i: (b, p))
    in_specs = [qspec(2 * QK_NOPE), qspec(LANES), kspec(2 * QK_NOPE), kspec(LANES), kspec(2 * C_DV)]
    args = [qn, qr, kn, kr, v]
    if sample:
        in_specs.append(pl.BlockSpec(memory_space=pl.ANY))
        args.append(o_prev)
    return pl.pallas_call(
        functools.partial(_attn_kernel, n_seq=n_seq),
        grid=grid,
        in_specs=in_specs,
        out_specs=qspec(2 * C_DV),
        out_shape=jax.ShapeDtypeStruct((N_TOK, C_HEADS * C_DV), BF16),
        input_output_aliases={5: 0} if sample else {},
        compiler_params=_params(("parallel", "parallel", "parallel")),
        name="mla_attention_sample" if sample else "mla_attention_prompt",
    )(*args)


def _rope_tables():
    rows = DEC_SEQ // GRID_W
    row = jnp.repeat(jnp.arange(rows, dtype=F32), GRID_W)
    col = jnp.tile(jnp.arange(GRID_W, dtype=F32), rows)
    inv_freq = ROPE_BASE ** (-jnp.arange(0, ROPE_AXIS, 2, dtype=F32) / ROPE_AXIS)
    ar = row[:, None] * inv_freq
    ac = col[:, None] * inv_freq
    cos = jnp.concatenate([jnp.cos(ar), jnp.cos(ar), jnp.cos(ac), jnp.cos(ac)], axis=-1)
    sin = jnp.concatenate([-jnp.sin(ar), jnp.sin(ar), -jnp.sin(ac), jnp.sin(ac)], axis=-1)
    return jnp.tile(cos, (1, 2)), jnp.tile(sin, (1, 2))


def _layer_weights(l, gla_w_gate_up, mla_w_q_up, mla_w_kv_up, mla_q_norm, mla_k_norm):
    ext =jnp.zeros((2, LANES, B_HEADS * B_DK), F32)
    ext = ext.at[0, :GLA_RANK].set(gla_w_gate_up[l, 0])
    ext = ext.at[1, GLA_RANK:2 * GLA_RANK].set(gla_w_gate_up[l, 1])
    wq = mla_w_q_up[l].reshape(Q_LORA, C_HEADS, QK_DIM)
    wq = jnp.concatenate([wq[:, :, :QK_NOPE].reshape(Q_LORA, -1),
                          wq[:, :, QK_NOPE:].reshape(Q_LORA, -1)], axis=1).astype(BF16)
    wkv = mla_w_kv_up[l].reshape(KV_LORA, C_HEADS, QK_NOPE + C_DV)
    wkv = jnp.concatenate([wkv[:, :, :QK_NOPE].reshape(KV_LORA, -1),
                           wkv[:, :, QK_NOPE:].reshape(KV_LORA, -1)], axis=1).astype(BF16)
    split = lambda g: (g[:QK_NOPE].reshape(1, QK_NOPE), jnp.tile(g[QK_NOPE:], 2).reshape(1, LANES))
    return ext.astype(BF16), wq, wkv, split(mla_q_norm[l]), split(mla_k_norm[l])


def kernel(x_prompt, x_sample, cache_mla, state_hgrn, state_gla, c, c_ctx, norm1, w_mod, b_mod, w_in,
           hgrn_lb_logits, hgrn_out_norm, gla_w_gate_up, gla_b_gate, gla_out_norm, mla_q_a_norm,
           mla_w_q_up, mla_kv_a_norm, mla_w_kv_up, mla_q_norm, mla_k_norm, w_branch_a, w_branch_b,
           w_branch_c, w_out, norm2, w_mlp_in, w_mlp_out):
    x = (x_prompt.reshape(N_PROMPT, D_MODEL), x_sample.reshape(N_SAMPLE, D_MODEL))
    cond =jnp.concatenate([c_ctx[None], c, jnp.zeros((MOD_ROWS - 1 - DEC_BATCH, D_MODEL), F32)])
    mod = _modulation(cond, w_mod, b_mod).reshape(DEPTH * MOD_ROWS, 1, N_MOD * D_MODEL)

    cos_s, sin_s = _rope_tables()
    cos_q = jnp.concatenate([jnp.ones((N_PROMPT, LANES), F32), jnp.tile(cos_s, (DEC_BATCH, 1))])
    sin_q = jnp.concatenate([jnp.zeros((N_PROMPT, LANES), F32), jnp.tile(sin_s, (DEC_BATCH, 1))])

    wb_a, wb_b, wb_c = w_branch_a.astype(BF16), w_branch_b.astype(BF16), w_branch_c.astype(BF16)
    w_in_t = jnp.swapaxes(w_in, 1, 2)
    narrow0, gate0 = Z_MAIN, Z_MAIN + 2 * GLA_RANK + Q_LORA + KV_LORA + QK_ROPE
    caches, hgrn_states, gla_states = [], [], []
    for l in range(DEPTH):
        wlr, wq, wkv, (qgn, qgr), (kgn, kgr) = _layer_weights(
            l, gla_w_gate_up, mla_w_q_up, mla_w_kv_up, mla_q_norm, mla_k_norm)
        ws = _cast_rows(w_in_t, l, narrow0, gate0 - narrow0, (gate0 - narrow0) // 3, "cast_narrow")
        wg = _cast_rows(w_in_t, l, gate0, 3 * D_MODEL, 512, "cast_gates")

        h = _norm_mod(x, norm1[l], mod, l, 0)
        z = _matmul(h, w_in_t, F32, l, Z_MAIN, transposed=True, name="in_proj")
        zs = _in_small(h, ws)

        o_a, sa = _hgrn(z, hgrn_lb_logits, hgrn_out_norm[l], None, l)
        (o_a,) = _hgrn(z, hgrn_lb_logits, hgrn_out_norm[l], state_hgrn, l, o_prev=o_a)
        o_b, sb = _gla(z, zs, wlr, gla_b_gate[l], gla_out_norm[l], None, l)
        (o_b,) = _gla(z, zs, wlr, gla_b_gate[l], gla_out_norm[l], state_gla, l, o_prev=o_b)
        hgrn_states.append(sa)
        gla_states.append(sb)

        qlat, ckvk = _mla_prep(zs, mla_q_a_norm[l], mla_kv_a_norm[l])
        caches.append(ckvk[:N_PROMPT].reshape(BATCH, SEQ, KV_LORA + QK_ROPE))
        qn, qr = _qup(qlat, wq, qgn, qgr, cos_q, sin_q)
        kn_p, kr_p, v_p = _kvup(ckvk, wkv, kgn, kgr)
        kn_s, kr_s, v_s = _kvup(ckvk, wkv, kgn, kgr, cache_mla, l, cos_s, sin_s)
        o_c = _attention(qn, qr, kn_p, kr_p, v_p)
        o_c = _attention(qn, qr, kn_s, kr_s, v_s, o_prev=o_c)

        merged = _merge(h, wg, o_a, wb_a, o_b, wb_b, o_c, wb_c, l)
        x = _matmul_residual(merged, w_out, x, mod, l, 2, 0, "out_proj")

        h2 = _norm_mod(x, norm2[l], mod, l, 3)
        u = _matmul(h2, w_mlp_in, BF16, l, D_FF, relu2=True, name="mlp_in")
        n_kb = D_FF // D_MODEL
        for kb in range(n_kb):
            last = l == DEPTH - 1 and kb == n_kb - 1
            x = _matmul_residual(u, w_mlp_out, x, mod, l, 5, kb, "mlp_out", split_out=last)

    y_prompt = x[0].reshape(BATCH, SEQ, D_MODEL)
    y_sample = x[1].reshape(DEC_BATCH, DEC_SEQ, D_MODEL)
    return (y_prompt, y_sample, jnp.stack(caches, axis=1), jnp.stack(hgrn_states, axis=1),
            jnp.stack(gla_states, axis=1))
```

```python
import functools

import numpy as np
import jax
import jax.numpy as jnp
from jax import lax
from jax.experimental import pallas as pl
from jax.experimental.pallas import tpu as pltpu

F32 = jnp.float32
BF16 = jnp.bfloat16

D_MODEL = 4096
BATCH = 32
SEQ = 256
DEPTH = 2
DEC_BATCH = 2
DEC_SEQ = 1024
PAST_LEN = 512
GRID_W = 64
A_HEADS = 8
A_DK = 128
A_DV = 128
B_HEADS = 4
B_DK = 128
B_DV = 256
GLA_RANK = 16
GLA_GATE_NORM = 16.0
C_HEADS = 16
Q_LORA = 1024
KV_LORA = 512
QK_NOPE = 128
QK_ROPE = 64
C_DV = 128
QK_DIM = QK_NOPE + QK_ROPE
ROPE_AXIS = QK_ROPE // 2
ROPE_BASE = 10000.0
N_MOD = 6
D_FF = 4 * D_MODEL
EPS = 1e-6

N_PROMPT = BATCH * SEQ
N_SAMPLE = DEC_BATCH * DEC_SEQ
N_TOK = N_PROMPT + N_SAMPLE
MOD_ROWS = 8

Z_MAIN = 8192
ZS_CQ = 0
ZS_CKV = ZS_CQ + Q_LORA
ZS_KR = ZS_CKV + KV_LORA
ZS_LR = ZS_KR + 128
ZS_COLS = ZS_LR + 128
ROW_ALIGN = 32

LANES = 128
MXU_COLS = 256
V7X_VMEM_BYTES = 64 * 1024 * 1024
VMEM_BIG = 60 * 1024 * 1024
VMEM_MID = 40 * 1024 * 1024

LOG2_E = 1.4426950408889634
CH = 32
SB = 8


def _params(sem, vmem=VMEM_MID):
    return pltpu.CompilerParams(dimension_semantics=sem, vmem_limit_bytes=vmem)


def _mod_row(i, tm):
    return jnp.where(i < N_PROMPT // tm, 0, 1 + (i - N_PROMPT // tm) // (DEC_SEQ // tm))


def _sigmoid(x):
    return 1.0 / (1.0 + jnp.exp(-x))


def _log1p_of_exp_neg(t):
    return jnp.log(1.0 + jnp.exp(-t))


def _log_sigmoid(x):
    return jnp.minimum(x, 0.0) - _log1p_of_exp_neg(jnp.abs(x))


def _mod_kernel(cond_ref, w_ref, b_ref, o_ref):
    c = cond_ref[...]
    s = (c * _sigmoid(c)).astype(BF16)
    o_ref[...] = jnp.dot(s, w_ref[...].astype(BF16), preferred_element_type=F32) + b_ref[...]


def _modulation(cond, w_mod, b_mod):
    tn = 512
    n = N_MOD * D_MODEL
    return pl.pallas_call(
        _mod_kernel,
        grid=(DEPTH, n // tn),
        in_specs=[
            pl.BlockSpec((MOD_ROWS, D_MODEL), lambda l, j: (0, 0)),
            pl.BlockSpec((None, D_MODEL, tn), lambda l, j: (l, 0, j)),
            pl.BlockSpec((None, 1, tn), lambda l, j: (l, 0, j)),
        ],
        out_specs=pl.BlockSpec((None, MOD_ROWS, tn), lambda l, j: (l, 0, j)),
        out_shape=jax.ShapeDtypeStruct((DEPTH, MOD_ROWS, n), F32),
        compiler_params=_params(("parallel", "parallel")),
        name="modulation",
    )(cond, w_mod, b_mod.reshape(DEPTH, 1, n))


def _norm_mod_kernel(x_ref, g_ref, sh_ref, sc_ref, *rest):
    o_ref = rest[-1]
    x = x_ref[...]
    ms = jnp.mean(x * x, axis=-1, keepdims=True)
    y = x * lax.rsqrt(ms + EPS) * g_ref[...]
    o_ref[...] = (y * (1.0 + sc_ref[...]) + sh_ref[...]).astype(BF16)


def _norm_mod(x, gain, mod, layer, shift_chunk, row0=0, o_prev=None):
    tm = 512
    base = layer * MOD_ROWS
    blk0 = row0 // tm
    mod_spec = lambda chunk: pl.BlockSpec(
        (None, 1, D_MODEL), lambda i: (base + _mod_row(blk0 + i, tm), 0, chunk))
    in_specs = [pl.BlockSpec((tm, D_MODEL), lambda i: (i, 0)),
                pl.BlockSpec((1, D_MODEL), lambda i: (0, 0)),
                mod_spec(shift_chunk), mod_spec(shift_chunk + 1)]
    args = [x, gain.reshape(1, D_MODEL), mod, mod]
    if o_prev is not None:
        in_specs.append(pl.BlockSpec(memory_space=pl.ANY))
        args.append(o_prev)
    return pl.pallas_call(
        _norm_mod_kernel,
        grid=(x.shape[0] // tm,),
        in_specs=in_specs,
        out_specs=pl.BlockSpec((tm, D_MODEL), lambda i: (blk0 + i, 0)),
        out_shape=jax.ShapeDtypeStruct((N_TOK, D_MODEL), BF16),
        input_output_aliases={} if o_prev is None else {4: 0},
        compiler_params=_params(("parallel",)),
        name="norm_mod",
    )(*args)


def _cast_weight(w_ref, wb_ref):
    @pl.when(pl.program_id(1) == 0)
    def _():
        wb_ref[...] = w_ref[...].astype(BF16)
    return wb_ref


TRANS_B = (((1,), (1,)), ((), ()))


def _mm_kernel(x_ref, w_ref, o_ref, wb_ref, *, relu2, transposed):
    w_ref = _cast_weight(w_ref, wb_ref)
    if transposed:
        acc = lax.dot_general(x_ref[...], w_ref[...], TRANS_B, preferred_element_type=F32)
    else:
        acc = jnp.dot(x_ref[...], w_ref[...], preferred_element_type=F32)
    if relu2:
        acc = jnp.square(jnp.maximum(acc, 0.0))
    o_ref[...] = acc.astype(o_ref.dtype)


def _matmul(x, w, out_dtype, layer, n_cols, relu2=False, transposed=False, name="matmul"):
    tm, tn = 512, 1024
    m, k = x.shape
    if transposed:
        w_spec = pl.BlockSpec((None, tn, k), lambda j, i: (layer, j, 0))
    else:
        w_spec = pl.BlockSpec((None, k, tn), lambda j, i: (layer, 0, j))
    return pl.pallas_call(
        functools.partial(_mm_kernel, relu2=relu2, transposed=transposed),
        grid=(n_cols // tn, m // tm),
        in_specs=[pl.BlockSpec((tm, k), lambda j, i: (i, 0)), w_spec],
        out_specs=pl.BlockSpec((tm, tn), lambda j, i: (i, j)),
        out_shape=jax.ShapeDtypeStruct((m, n_cols), out_dtype),
        scratch_shapes=[pltpu.VMEM((tn, k) if transposed else (k, tn), BF16)],
        compiler_params=_params(("parallel", "arbitrary"), VMEM_BIG),
        name=name,
    )(x, w)


def _cast_rows_kernel(w_ref, o_ref):
    o_ref[...] = w_ref[0].astype(BF16)


def _cast_rows(w, layer, row0, n_rows, block_rows, name):
    k = w.shape[2]
    assert row0 % ROW_ALIGN == 0 and block_rows % ROW_ALIGN == 0 and n_rows % block_rows == 0
    start = lambda r: pl.multiple_of(ROW_ALIGN * (row0 // ROW_ALIGN + (block_rows // ROW_ALIGN) * r),
                                     ROW_ALIGN)
    return pl.pallas_call(
        _cast_rows_kernel,
        grid=(n_rows // block_rows,),
        in_specs=[pl.BlockSpec((pl.Element(1), pl.Element(block_rows), pl.Element(k)),
                               lambda r: (layer, start(r), 0))],
        out_specs=pl.BlockSpec((block_rows, k), lambda r: (r, 0)),
        out_shape=jax.ShapeDtypeStruct((n_rows, k), BF16),
        compiler_params=_params(("parallel",)),
        name=name,
    )(w)


def _in_small_kernel(x_ref, w_ref, o_ref):
    x = x_ref[...]
    tm = x.shape[0]
    n_lr, n_kv = 2 * GLA_RANK, KV_LORA + QK_ROPE
    part = lambda r0, rows: lax.dot_general(x, w_ref[r0:r0 + rows, :], TRANS_B,
                                            preferred_element_type=F32)
    o_ref[:, ZS_LR:ZS_LR + n_lr] = part(0, n_lr)
    o_ref[:, ZS_LR + n_lr:] = jnp.zeros((tm, ZS_COLS - ZS_LR - n_lr), F32)
    o_ref[:, ZS_CQ:ZS_CQ + Q_LORA] = part(n_lr, Q_LORA)
    o_ref[:, ZS_CKV:ZS_CKV + n_kv] = part(n_lr + Q_LORA, n_kv)
    o_ref[:, ZS_CKV + n_kv:ZS_LR] = jnp.zeros((tm, ZS_LR - ZS_CKV - n_kv), F32)


def _in_small(h, w_small):
    tm = 512
    return pl.pallas_call(
        _in_small_kernel,
        grid=(N_TOK // tm,),
        in_specs=[pl.BlockSpec((tm, D_MODEL), lambda i: (i, 0)),
                  pl.BlockSpec(w_small.shape, lambda i: (0, 0))],
        out_specs=pl.BlockSpec((tm, ZS_COLS), lambda i: (i, 0)),
        out_shape=jax.ShapeDtypeStruct((N_TOK, ZS_COLS), F32),
        compiler_params=_params(("parallel",), VMEM_BIG),
        name="in_proj_small",
    )(h, w_small)


def _mm_res_kernel(x_ref, w_ref, res_ref, g_ref, *rest):
    o_ref, wb_ref = rest[-2:]
    w_ref = _cast_weight(w_ref, wb_ref)
    x = x_ref[...]
    for c0 in range(0, o_ref.shape[1], MXU_COLS):
        cols = slice(c0, c0 + MXU_COLS)
        part = jnp.dot(x, w_ref[:, cols], preferred_element_type=F32)
        o_ref[:, cols] = res_ref[:, cols] + g_ref[:, cols] * part


def _matmul_residual(a, w, res, mod, layer, gate_chunk, k_block, name, rows=(0, N_TOK),
                     res_row0=None, out_rows=N_TOK, out_row0=None, o_prev=None):
    tm, tn, tk = 512, 1024, D_MODEL
    n = w.shape[2]
    base = layer * MOD_ROWS
    cpc = D_MODEL // tn
    blk0 = rows[0] // tm
    res_blk0 = blk0 if res_row0 is None else res_row0 // tm
    out_blk0 = blk0 if out_row0 is None else out_row0 // tm
    in_specs = [
        pl.BlockSpec((tm, tk), lambda j, i: (blk0 + i, k_block)),
        pl.BlockSpec((None, tk, tn), lambda j, i: (layer, k_block, j)),
        pl.BlockSpec((tm, tn), lambda j, i: (res_blk0 + i, j)),
        pl.BlockSpec((None, 1, tn),
                     lambda j, i: (base + _mod_row(blk0 + i, tm), 0, gate_chunk * cpc + j)),
    ]
    args = [a, w, res, mod]
    if o_prev is not None:
        in_specs.append(pl.BlockSpec(memory_space=pl.ANY))
        args.append(o_prev)
    return pl.pallas_call(
        _mm_res_kernel,
        grid=(n // tn, rows[1] // tm),
        in_specs=in_specs,
        out_specs=pl.BlockSpec((tm, tn), lambda j, i: (out_blk0 + i, j)),
        out_shape=jax.ShapeDtypeStruct((out_rows, n), F32),
        input_output_aliases={} if o_prev is None else {4: 0},
        scratch_shapes=[pltpu.VMEM((tk, tn), BF16)],
        compiler_params=_params(("parallel", "arbitrary"), VMEM_BIG),
        name=name,
    )(*args)


def _merge_kernel(h_ref, wga_ref, wgb_ref, wgc_ref, oa_ref, wa_ref, ob_ref, wb_ref, oc_ref, wc_ref,
                  o_ref):
    h = h_ref[...]

    def branch(wg_ref, o_r, w_r):
        g = lax.dot_general(h, wg_ref[...], TRANS_B, preferred_element_type=F32)
        p = jnp.dot(o_r[...], w_r[...], preferred_element_type=F32)
        return _sigmoid(g) * p

    acc = branch(wga_ref, oa_ref, wa_ref)
    acc = acc + branch(wgb_ref, ob_ref, wb_ref)
    acc = acc + branch(wgc_ref, oc_ref, wc_ref)
    o_ref[...] = acc.astype(BF16)


def _merge(h, wg, o_a, w_a, o_b, w_b, o_c, w_c, layer):
    tm, tn = 512, 512
    nj = D_MODEL // tn
    act = lambda width: pl.BlockSpec((tm, width), lambda j, i: (i, 0))
    wcol = lambda rows: pl.BlockSpec((None, rows, tn), lambda j, i: (layer, 0, j))
    gate = lambda b: pl.BlockSpec((tn, D_MODEL), lambda j, i: (b * nj + j, 0))
    return pl.pallas_call(
        _merge_kernel,
        grid=(nj, N_TOK // tm),
        in_specs=[act(D_MODEL), gate(0), gate(1), gate(2),
                  act(o_a.shape[1]), wcol(w_a.shape[1]),
                  act(o_b.shape[1]), wcol(w_b.shape[1]),
                  act(o_c.shape[1]), wcol(w_c.shape[1])],
        out_specs=pl.BlockSpec((tm, tn), lambda j, i: (i, j)),
        out_shape=jax.ShapeDtypeStruct((N_TOK, D_MODEL), BF16),
        compiler_params=_params(("parallel", "parallel"), VMEM_BIG),
        name="merge",
    )(h, wg, wg, wg, o_a, w_a, o_b, w_b, o_c, w_c)


PAD = 8


def _shifted(ref, n_rows, d, reverse):
    return ref[pl.ds(PAD + d if reverse else PAD - d, n_rows), :]


def _seg_cumsum(g, rowc, reverse, c_scr):
    n_rows = g.shape[0]
    x = g
    for sh in (1, 2, 4, 8, 16):
        mask = rowc < CH - sh if reverse else rowc >= sh
        if sh < PAD:
            c_scr[pl.ds(PAD, n_rows), :] = x
            moved = _shifted(c_scr, n_rows, sh, reverse)
        else:
            moved = pltpu.roll(x, n_rows - sh if reverse else sh, 0)
        x = x + jnp.where(mask, moved, 0.0)
    return x


def _scan_dir(q, k, v, g, s0t, reverse, scratch, accumulate, want_state):
    o_scr, k_scr, c_scr, v_scr = scratch
    n_rows, dk = q.shape
    n = n_rows // CH
    rowc = lax.broadcasted_iota(jnp.int32, (n_rows, dk), 0) & (CH - 1)
    cum = _seg_cumsum(g * LOG2_E, rowc, reverse, c_scr)
    c_scr[pl.ds(PAD, n_rows), :] = cum
    k_scr[pl.ds(PAD, n_rows), :] = k

    def rows_of(r, count):
        return jnp.broadcast_to(cum[r:r + 1, :], (count, dk))

    def chunk_row(r):
        return jnp.concatenate([rows_of(c * CH + r, CH) for c in range(n)], axis=0)

    last_row = 0 if reverse else CH - 1
    last_b = chunk_row(last_row)
    qt = (q * jnp.exp2(cum)).astype(BF16)
    kh = (k * jnp.exp2(last_b - cum)).astype(BF16)
    vb = v.astype(BF16)

    rows = rowc & (SB - 1)
    o_diag = jnp.sum(q * k, axis=-1, keepdims=True) * v
    for d in range(1, SB):
        mask = rows < SB - d if reverse else rows >= d
        kd = _shifted(k_scr, n_rows, d, reverse)
        cd = _shifted(c_scr, n_rows, d, reverse)
        vd = _shifted(v_scr, n_rows, d, reverse)
        e = jnp.exp2(jnp.where(mask, cum - cd, -jnp.inf))
        o_diag = o_diag + jnp.sum(q * kd * e, axis=-1, keepdims=True) * vd
    if accumulate:
        o_scr[...] += o_diag
    else:
        o_scr[...] = o_diag

    sub = rowc // SB
    nsb = CH // SB
    if reverse:
        own = [rows_of(c * CH + SB * min(i + 1, nsb - 1), SB) for c in range(n) for i in range(nsb)]
        has_piv = sub < nsb - 1
    else:
        own = [rows_of(c * CH + SB * max(i, 1) - 1, SB) for c in range(n) for i in range(nsb)]
        has_piv = sub >= 1
    qe = q * jnp.exp2(jnp.where(has_piv, cum - jnp.concatenate(own, axis=0), -jnp.inf))
    qs, ks = [], []
    for pr in range(SB, CH, SB):
        if reverse:
            piv, qmask, kmask = chunk_row(pr), sub == pr // SB - 1, rowc >= pr
        else:
            piv, qmask, kmask = chunk_row(pr - 1), sub == pr // SB, rowc < pr
        qs.append(jnp.where(qmask, qe, 0.0).astype(BF16))
        ks.append((k * jnp.exp2(jnp.where(kmask, piv - cum, -jnp.inf))).astype(BF16))
    qcat = jnp.concatenate(qs, axis=-1)
    kcat = jnp.concatenate(ks, axis=-1)

    trans_a = (((0,), (0,)), ((), ()))
    order = list(range(n - 1, -1, -1) if reverse else range(n))
    rows = [slice(c * CH, (c + 1) * CH) for c in order]
    n_upd = n if want_state else n - 1
    uts = [lax.dot_general(vb[sl], kh[sl], trans_a, preferred_element_type=F32) for sl in rows[:n_upd]]
    atts = [lax.dot_general(qcat[sl], kcat[sl], TRANS_B, preferred_element_type=F32).astype(BF16)
            for sl in rows]
    o_off = [jnp.dot(a, vb[sl], preferred_element_type=F32) for a, sl in zip(atts, rows)]
    st = s0t
    states = []
    for idx, c in enumerate(order):
        states.append(st.astype(BF16))
        if idx < n_upd:
            r = c * CH + last_row
            st = st * jnp.exp2(cum[r:r + 1, :]) + uts[idx]
    for sl, s_in, off in zip(rows, states, o_off):
        o_scr[sl, :] += lax.dot_general(qt[sl], s_in, TRANS_B, preferred_element_type=F32) + off
    return st


def _finish_scan(o_scr, gain_ref, zg_ref, o_ref):
    o = o_scr[...]
    ms = jnp.mean(o * o, axis=-1, keepdims=True)
    zg = zg_ref[...]
    o_ref[...] = (o * lax.rsqrt(ms + EPS) * gain_ref[...] * (zg * _sigmoid(zg))).astype(BF16)


def _split_scan_refs(refs, n_in, sample):
    pos = n_in
    s0_ref = None
    if sample:
        s0_ref = refs[pos]
        pos += 2
    o_ref = refs[pos]
    st_ref = None if sample else refs[pos + 1]
    scratch = refs[pos + (1 if sample else 2):]
    return refs[:n_in], s0_ref, o_ref, st_ref, scratch


def _prep_scan_scratch(scratch, v):
    _, k_scr, c_scr, v_scr = scratch
    n_rows = v.shape[0]
    for ref in (k_scr, c_scr, v_scr):
        zero = jnp.zeros((PAD, ref.shape[1]), F32)
        ref[pl.ds(0, PAD), :] = zero
        ref[pl.ds(PAD + n_rows, PAD), :] = zero
    v_scr[pl.ds(PAD, n_rows), :] = v


def _scan_scratch(seq, dk, dv):
    return [pltpu.VMEM((seq, dv), F32), pltpu.VMEM((seq + 2 * PAD, dk), F32),
            pltpu.VMEM((seq + 2 * PAD, dk), F32), pltpu.VMEM((seq + 2 * PAD, dv), F32)]


def _hgrn_kernel(*refs, layer, sample):
    ins, s0_ref, o_ref, st_ref, scratch = _split_scan_refs(refs, 7, sample)
    lbl_ref, q_ref, ff_ref, fb_ref, i_ref, zg_ref, gain_ref = ins
    has_state, want_state = sample, not sample

    logits = [lbl_ref[l] for l in range(DEPTH)]
    mx = functools.reduce(jnp.maximum, logits)
    ex = [jnp.exp(x - mx) for x in logits]
    tot = functools.reduce(lambda a, b: a + b, ex)
    probs = [e / tot for e in ex]
    cum_first = probs[0]
    cum_l = functools.reduce(lambda a, b: a + b, probs[:layer + 1])
    lb = cum_l - cum_first

    q = q_ref[...]
    v = i_ref[...]
    _prep_scan_scratch(scratch, v)
    for d, (z_ref, reverse) in enumerate(((ff_ref, False), (fb_ref, True))):
        lbd = lb[d:d + 1, :]
        zf = z_ref[...]
        a = jnp.log(lbd)
        b = jnp.log1p(-lbd) + _log_sigmoid(zf)
        delta = a - b
        log_f = jnp.where(delta != delta, a + b,
                          jnp.maximum(a, b) + _log1p_of_exp_neg(jnp.abs(delta)))
        one_minus_f = (1.0 - lbd) * _sigmoid(-zf)
        if has_state:
            s0t = s0_ref[d].T
        else:
            s0t = jnp.zeros((A_DV, A_DK), F32)
        st = _scan_dir(q, one_minus_f, v, log_f, s0t, reverse, scratch, d == 1, want_state)
        if want_state:
            st_ref[d] = st.T
    _finish_scan(scratch[0], gain_ref, zg_ref, o_ref)


def _scan_call(body, name, z_specs, args, heads, dk, dv, state, layer, o_prev):
    sample = o_prev is not None
    seq, nb = (DEC_SEQ, DEC_BATCH) if sample else (SEQ, BATCH)
    row0 = N_PROMPT // seq if sample else 0
    in_specs = [spec(seq, row0) for spec in z_specs]
    args = list(args)
    aliases = {}
    if sample:
        in_specs += [pl.BlockSpec((None, None, 2, None, dk, dv), lambda b, h: (b, layer, 0, h, 0, 0)),
                     pl.BlockSpec(memory_space=pl.ANY)]
        aliases = {len(args) + 1: 0}
        args += [state, o_prev]
    out_specs = [pl.BlockSpec((seq, dv), lambda b, h: (row0 + b, h))]
    out_shape = [jax.ShapeDtypeStruct((N_TOK, heads * dv), BF16)]
    if not sample:
        out_specs.append(pl.BlockSpec((None, 2, None, dk, dv), lambda b, h: (b, 0, h, 0, 0)))
        out_shape.append(jax.ShapeDtypeStruct((nb, 2, heads, dk, dv), F32))
    return pl.pallas_call(
        functools.partial(body, sample=sample),
        grid=(nb, heads),
        in_specs=in_specs,
        out_specs=out_specs,
        out_shape=out_shape,
        input_output_aliases=aliases,
        scratch_shapes=_scan_scratch(seq, dk, dv),
        compiler_params=_params(("parallel", "parallel")),
        name=name + ("_sample" if sample else "_prompt"),
    )(*args)


def _hgrn(z, lb_logits, out_gain, state, layer, o_prev=None):
    zcol = lambda off: (lambda seq, row0: pl.BlockSpec((seq, A_DK), lambda b, h: (row0 + b, off + h)))
    fixed = lambda spec: (lambda seq, row0: spec)
    z_specs = [
        fixed(pl.BlockSpec((DEPTH, 2, A_DK), lambda b, h: (0, 0, h))),
        zcol(0), zcol(A_HEADS), zcol(2 * A_HEADS), zcol(3 * A_HEADS), zcol(4 * A_HEADS),
        fixed(pl.BlockSpec((1, A_DV), lambda b, h: (0, 0))),
    ]
    args = [lb_logits, z, z, z, z, z, out_gain.reshape(1, A_DV)]
    return _scan_call(functools.partial(_hgrn_kernel, layer=layer), "hgrn", z_specs, args,
                      A_HEADS, A_DK, A_DV, state, layer, o_prev)


def _gla_kernel(*refs, sample):
    ins, s0_ref, o_ref, st_ref, scratch = _split_scan_refs(refs, 8, sample)
    q_ref, k_ref, v_ref, zr_ref, lr_ref, wg_ref, bg_ref, gain_ref = ins
    has_state, want_state = sample, not sample

    q = q_ref[...] * (B_DK ** -0.5)
    k = k_ref[...]
    v = v_ref[...]
    _prep_scan_scratch(scratch, v)
    lr = lr_ref[...].astype(BF16)
    for d, reverse in enumerate((False, True)):
        logit = jnp.dot(lr, wg_ref[d], preferred_element_type=F32) + bg_ref[d]
        g = _log_sigmoid(logit) / GLA_GATE_NORM
        if has_state:
            s0t = s0_ref[d].T
        else:
            s0t = jnp.zeros((B_DV, B_DK), F32)
        st = _scan_dir(q, k, v, g, s0t, reverse, scratch, d == 1, want_state)
        if want_state:
            st_ref[d] = st.T
    _finish_scan(scratch[0], gain_ref, zr_ref, o_ref)


def _gla(z, zs, w_gate_ext, b_gate, out_gain, state, layer, o_prev=None):
    zcol = lambda width, col: (lambda seq, row0: pl.BlockSpec(
        (seq, width), lambda b, h: (row0 + b, col // width + h)))
    fixed = lambda spec: (lambda seq, row0: spec)
    z_specs = [
        zcol(B_DK, 5120), zcol(B_DK, 5632), zcol(B_DV, 6144), zcol(B_DV, 7168),
        lambda seq, row0: pl.BlockSpec((seq, LANES), lambda b, h: (row0 + b, ZS_LR // LANES)),
        fixed(pl.BlockSpec((2, LANES, B_DK), lambda b, h: (0, 0, h))),
        fixed(pl.BlockSpec((2, 1, B_DK), lambda b, h: (0, 0, h))),
        fixed(pl.BlockSpec((1, B_DV), lambda b, h: (0, 0))),
    ]
    args = [z, z, z, z, zs, w_gate_ext, b_gate.reshape(2, 1, B_HEADS * B_DK),
            out_gain.reshape(1, B_DV)]
    return _scan_call(_gla_kernel, "gla", z_specs, args, B_HEADS, B_DK, B_DV, state, layer, o_prev)


def _mla_prep_kernel(cq_ref, ckv_ref, kr_ref, gq_ref, gkv_ref, qlat_ref, ckvk_ref):
    cq = cq_ref[...]
    qlat_ref[...] = (cq * lax.rsqrt(jnp.mean(cq * cq, axis=-1, keepdims=True) + EPS)
                     * gq_ref[...]).astype(BF16)
    ckv = ckv_ref[...]
    ckvk_ref[:, :KV_LORA] = (ckv * lax.rsqrt(jnp.mean(ckv * ckv, axis=-1, keepdims=True) + EPS)
                             * gkv_ref[...])
    ckvk_ref[:, KV_LORA:] = kr_ref[:, :QK_ROPE]


def _mla_prep(z, q_gain, kv_gain):
    tm = 512
    return pl.pallas_call(
        _mla_prep_kernel,
        grid=(N_TOK // tm,),
        in_specs=[
            pl.BlockSpec((tm, Q_LORA), lambda i: (i, ZS_CQ // Q_LORA)),
            pl.BlockSpec((tm, KV_LORA), lambda i: (i, ZS_CKV // KV_LORA)),
            pl.BlockSpec((tm, LANES), lambda i: (i, ZS_KR // LANES)),
            pl.BlockSpec((1, Q_LORA), lambda i: (0, 0)),
            pl.BlockSpec((1, KV_LORA), lambda i: (0, 0)),
        ],
        out_specs=[pl.BlockSpec((tm, Q_LORA), lambda i: (i, 0)),
                   pl.BlockSpec((tm, KV_LORA + QK_ROPE), lambda i: (i, 0))],
        out_shape=[jax.ShapeDtypeStruct((N_TOK, Q_LORA), BF16),
                   jax.ShapeDtypeStruct((N_TOK, KV_LORA + QK_ROPE), F32)],
        compiler_params=_params(("parallel",)),
        name="mla_prep",
    )(z, z, z, q_gain.reshape(1, Q_LORA), kv_gain.reshape(1, KV_LORA))


def _swap_halves(t, lane):
    width = t.shape[-1]
    return jnp.where((lane & 31) < 16, pltpu.roll(t, width - 16, 1), pltpu.roll(t, 16, 1))


def _qup_kernel(x_ref, w_ref, gn_ref, gr_ref, c_ref, s_ref, qn_ref, qr_ref):
    acc = jnp.dot(x_ref[...], w_ref[...], preferred_element_type=F32)
    nope_w = C_HEADS * QK_NOPE
    tm = acc.shape[0]
    lane = lax.broadcasted_iota(jnp.int32, (tm, LANES), 1)
    low = lane < QK_ROPE
    gn = gn_ref[...]
    gr = gr_ref[...]
    cos = c_ref[...]
    sin = s_ref[...]
    for p in range(C_HEADS // 2):
        r2 = acc[:, nope_w + LANES * p: nope_w + LANES * (p + 1)]
        sq = r2 * r2
        ss_rope = (jnp.sum(jnp.where(low, sq, 0.0), axis=-1, keepdims=True),
                   jnp.sum(jnp.where(low, 0.0, sq), axis=-1, keepdims=True))
        inv = []
        for hh in range(2):
            h = 2 * p + hh
            nh = acc[:, QK_NOPE * h: QK_NOPE * (h + 1)]
            ssn = jnp.sum(nh * nh, axis=-1, keepdims=True)
            r = lax.rsqrt((ssn + ss_rope[hh]) / QK_DIM + EPS)
            qn_ref[:, QK_NOPE * h: QK_NOPE * (h + 1)] = (nh * r * gn).astype(BF16)
            inv.append(r)
        t = r2 * jnp.where(low, inv[0], inv[1]) * gr
        qr_ref[:, LANES * p: LANES * (p + 1)] = (t * cos + _swap_halves(t, lane) * sin).astype(BF16)


def _qup(qlat, w, gn, gr, cos, sin):
    tm = 512
    n_rows = qlat.shape[0]
    row = lambda width: pl.BlockSpec((tm, width), lambda i: (i, 0))
    full = lambda a: pl.BlockSpec(a.shape, lambda i: (0, 0))
    return pl.pallas_call(
        _qup_kernel,
        grid=(n_rows // tm,),
        in_specs=[row(Q_LORA), full(w), full(gn), full(gr), row(LANES), row(LANES)],
        out_specs=[row(C_HEADS * QK_NOPE), row(C_HEADS * QK_ROPE)],
        out_shape=[jax.ShapeDtypeStruct((n_rows, C_HEADS * QK_NOPE), BF16),
                   jax.ShapeDtypeStruct((n_rows, C_HEADS * QK_ROPE), BF16)],
        compiler_params=_params(("parallel",)),
        name="mla_q_up",
    )(qlat, w, gn, gr, cos, sin)


def _kvup_kernel(*refs, sample):
    if sample:
        cache_ref, x_ref, w_ref, gn_ref, gr_ref, c_ref, s_ref, kn_ref, kr_ref, v_ref = refs
        from_cache = pl.program_id(1) == 0
        x = jnp.where(from_cache, cache_ref[...], x_ref[...])
    else:
        x_ref, w_ref, gn_ref, gr_ref, kn_ref, kr_ref, v_ref = refs
        x = x_ref[...]
    acc = jnp.dot(x[:, :KV_LORA].astype(BF16), w_ref[...], preferred_element_type=F32)
    nope_w = C_HEADS * QK_NOPE
    v_ref[...] = acc[:, nope_w:].astype(BF16)
    tm = acc.shape[0]
    lane = lax.broadcasted_iota(jnp.int32, (tm, LANES), 1)
    low = lane < QK_ROPE
    kr = x[:, KV_LORA:]
    ss_rope = jnp.sum(kr * kr, axis=-1, keepdims=True)
    rot = jnp.concatenate([kr, kr], axis=-1) * gr_ref[...]
    if sample:
        turned = rot * c_ref[...] + _swap_halves(rot, lane) * s_ref[...]
        rot = jnp.where(from_cache, rot, turned)
    gn = gn_ref[...]
    for p in range(C_HEADS // 2):
        inv = []
        for hh in range(2):
            h = 2 * p + hh
            nh = acc[:, QK_NOPE * h: QK_NOPE * (h + 1)]
            ssn = jnp.sum(nh * nh, axis=-1, keepdims=True)
            r = lax.rsqrt((ssn + ss_rope) / QK_DIM + EPS)
            kn_ref[:, QK_NOPE * h: QK_NOPE * (h + 1)] = (nh * r * gn).astype(BF16)
            inv.append(r)
        kr_ref[:, LANES * p: LANES * (p + 1)] = (rot * jnp.where(low, inv[0], inv[1])).astype(BF16)


def _kvup(ckvk, w, gn, gr, cache=None, layer=None, cos=None, sin=None):
    tm = PAST_LEN
    sample = cache is not None
    width = KV_LORA + QK_ROPE
    if sample:
        per_seq = 1 + DEC_SEQ // tm
        grid = (DEC_BATCH, per_seq)
        row_blk = lambda b, t: b * per_seq + t
        new_blk = lambda b, t: N_PROMPT // tm + b * (per_seq - 1) + jnp.maximum(t - 1, 0)
        full = lambda a: pl.BlockSpec(a.shape, lambda b, t: (0, 0))
        table = pl.BlockSpec((tm, LANES), lambda b, t: (jnp.maximum(t - 1, 0), 0))
        in_specs = [pl.BlockSpec((None, None, tm, width), lambda b, t: (b, layer, 0, 0)),
                    pl.BlockSpec((tm, width), lambda b, t: (new_blk(b, t), 0)),
                    full(w), full(gn), full(gr), table, table]
        args = (cache, ckvk, w, gn, gr, cos, sin)
        row = lambda wd: pl.BlockSpec((tm, wd), lambda b, t: (row_blk(b, t), 0))
        n_rows = DEC_BATCH * per_seq * tm
    else:
        grid = (N_PROMPT // tm,)
        full = lambda a: pl.BlockSpec(a.shape, lambda i: (0, 0))
        in_specs = [pl.BlockSpec((tm, width), lambda i: (i, 0)), full(w), full(gn), full(gr)]
        args = (ckvk, w, gn, gr)
        row = lambda wd: pl.BlockSpec((tm, wd), lambda i: (i, 0))
        n_rows = N_PROMPT
    return pl.pallas_call(
        functools.partial(_kvup_kernel, sample=sample),
        grid=grid,
        in_specs=in_specs,
        out_specs=[row(C_HEADS * QK_NOPE), row(C_HEADS * QK_ROPE), row(C_HEADS * C_DV)],
        out_shape=[jax.ShapeDtypeStruct((n_rows, C_HEADS * QK_NOPE), BF16),
                   jax.ShapeDtypeStruct((n_rows, C_HEADS * QK_ROPE), BF16),
                   jax.ShapeDtypeStruct((n_rows, C_HEADS * C_DV), BF16)],
        compiler_params=_params(("parallel",) * len(grid)),
        name="mla_kv_up_sample" if sample else "mla_kv_up_prompt",
    )(*args)


def _attn_kernel(qn_ref, qr_ref, kn_ref, kr_ref, v_ref, *rest, n_seq):
    o_ref = rest[-1]
    lq = qr_ref.shape[0] // n_seq
    lk = kr_ref.shape[0] // n_seq
    lane = lax.broadcasted_iota(jnp.int32, (lq, LANES), 1)
    blocks = [(slice(s * lq, (s + 1) * lq), slice(s * lk, (s + 1) * lk),
               slice(QK_NOPE * hh, QK_NOPE * (hh + 1)), hh)
              for s in range(n_seq) for hh in range(2)]
    scores = []
    for qrows, krows, cols, hh in blocks:
        qr = qr_ref[qrows, :]
        own = (lane < QK_ROPE) if hh == 0 else (lane >= QK_ROPE)
        qrm = jnp.where(own, qr, jnp.zeros_like(qr))
        s = lax.dot_general(qn_ref[qrows, cols], kn_ref[krows, cols], TRANS_B,
                            preferred_element_type=F32)
        s = s + lax.dot_general(qrm, kr_ref[krows, :], TRANS_B, preferred_element_type=F32)
        scores.append(s * (QK_DIM ** -0.5 * LOG2_E))
    tops = [jnp.max(s, axis=-1, keepdims=True) for s in scores]
    exps = [jnp.exp2(s - m) for s, m in zip(scores, tops)]
    sums = [jnp.sum(e, axis=-1, keepdims=True) for e in exps]
    probs = [(e / t).astype(BF16) for e, t in zip(exps, sums)]
    for p, (qrows, krows, cols, _) in zip(probs, blocks):
        o_ref[qrows, cols] = jnp.dot(p, v_ref[krows, cols], preferred_element_type=F32).astype(BF16)


def _attention(qn, qr, kn, kr, v, o_prev=None):
    sample = o_prev is not None
    if sample:
        q_row0, tq, tk, n_seq = N_PROMPT, SEQ, PAST_LEN + DEC_SEQ, 1
        grid = (DEC_BATCH, C_HEADS // 2, DEC_SEQ // tq)
    else:
        n_seq = 4
        q_row0, tq, tk = 0, n_seq * SEQ, n_seq * SEQ
        grid = (BATCH // n_seq, C_HEADS // 2, 1)
    nq = grid[2]
    qspec = lambda width: pl.BlockSpec((tq, width), lambda b, p, i: (q_row0 // tq + b * nq + i, p))
    kspec = lambda width: pl.BlockSpec((tk, width), lambda b, p, i: (b, p))
    in_specs = [qspec(2 * QK_NOPE), qspec(LANES), kspec(2 * QK_NOPE), kspec(LANES), kspec(2 * C_DV)]
    args = [qn, qr, kn, kr, v]
    if sample:
        in_specs.append(pl.BlockSpec(memory_space=pl.ANY))
        args.append(o_prev)
    return pl.pallas_call(
        functools.partial(_attn_kernel, n_seq=n_seq),
        grid=grid,
        in_specs=in_specs,
        out_specs=qspec(2 * C_DV),
        out_shape=jax.ShapeDtypeStruct((N_TOK, C_HEADS * C_DV), BF16),
        input_output_aliases={5: 0} if sample else {},
        compiler_params=_params(("parallel", "parallel", "parallel")),
        name="mla_attention_sample" if sample else "mla_attention_prompt",
    )(*args)


def _rope_tables():
    rows = DEC_SEQ // GRID_W
    row = jnp.repeat(jnp.arange(rows, dtype=F32), GRID_W)
    col = jnp.tile(jnp.arange(GRID_W, dtype=F32), rows)
    inv_freq = ROPE_BASE ** (-jnp.arange(0, ROPE_AXIS, 2, dtype=F32) / ROPE_AXIS)
    ar = row[:, None] * inv_freq
    ac = col[:, None] * inv_freq
    cos = jnp.concatenate([jnp.cos(ar), jnp.cos(ar), jnp.cos(ac), jnp.cos(ac)], axis=-1)
    sin = jnp.concatenate([-jnp.sin(ar), jnp.sin(ar), -jnp.sin(ac), jnp.sin(ac)], axis=-1)
    return jnp.tile(cos, (1, 2)), jnp.tile(sin, (1, 2))


def _layer_weights(l, gla_w_gate_up, mla_w_q_up, mla_w_kv_up, mla_q_norm, mla_k_norm):
    ext =jnp.zeros((2, LANES, B_HEADS * B_DK), F32)
    ext = ext.at[0, :GLA_RANK].set(gla_w_gate_up[l, 0])
    ext = ext.at[1, GLA_RANK:2 * GLA_RANK].set(gla_w_gate_up[l, 1])
    wq = mla_w_q_up[l].reshape(Q_LORA, C_HEADS, QK_DIM)
    wq = jnp.concatenate([wq[:, :, :QK_NOPE].reshape(Q_LORA, -1),
                          wq[:, :, QK_NOPE:].reshape(Q_LORA, -1)], axis=1).astype(BF16)
    wkv = mla_w_kv_up[l].reshape(KV_LORA, C_HEADS, QK_NOPE + C_DV)
    wkv = jnp.concatenate([wkv[:, :, :QK_NOPE].reshape(KV_LORA, -1),
                           wkv[:, :, QK_NOPE:].reshape(KV_LORA, -1)], axis=1).astype(BF16)
    split = lambda g: (g[:QK_NOPE].reshape(1, QK_NOPE), jnp.tile(g[QK_NOPE:], 2).reshape(1, LANES))
    return ext.astype(BF16), wq, wkv, split(mla_q_norm[l]), split(mla_k_norm[l])


def kernel(x_prompt, x_sample, cache_mla, state_hgrn, state_gla, c, c_ctx, norm1, w_mod, b_mod, w_in,
           hgrn_lb_logits, hgrn_out_norm, gla_w_gate_up, gla_b_gate, gla_out_norm, mla_q_a_norm,
           mla_w_q_up, mla_kv_a_norm, mla_w_kv_up, mla_q_norm, mla_k_norm, w_branch_a, w_branch_b,
           w_branch_c, w_out, norm2, w_mlp_in, w_mlp_out):
    ctx_rows, lat_rows = (0, N_PROMPT), (N_PROMPT, N_SAMPLE)
    x = (x_prompt.reshape(N_PROMPT, D_MODEL), x_sample.reshape(N_SAMPLE, D_MODEL))
    cond =jnp.concatenate([c_ctx[None], c, jnp.zeros((MOD_ROWS - 1 - DEC_BATCH, D_MODEL), F32)])
    mod = _modulation(cond, w_mod, b_mod).reshape(DEPTH * MOD_ROWS, 1, N_MOD * D_MODEL)

    cos_s, sin_s = _rope_tables()
    cos_q = jnp.concatenate([jnp.ones((N_PROMPT, LANES), F32), jnp.tile(cos_s, (DEC_BATCH, 1))])
    sin_q = jnp.concatenate([jnp.zeros((N_PROMPT, LANES), F32), jnp.tile(sin_s, (DEC_BATCH, 1))])

    wb_a, wb_b, wb_c = w_branch_a.astype(BF16), w_branch_b.astype(BF16), w_branch_c.astype(BF16)
    w_in_t = jnp.swapaxes(w_in, 1, 2)
    narrow0, gate0 = Z_MAIN, Z_MAIN + 2 * GLA_RANK + Q_LORA + KV_LORA + QK_ROPE
    caches, hgrn_states, gla_states = [], [], []
    for l in range(DEPTH):
        wlr, wq, wkv, (qgn, qgr), (kgn, kgr) = _layer_weights(
            l, gla_w_gate_up, mla_w_q_up, mla_w_kv_up, mla_q_norm, mla_k_norm)
        ws = _cast_rows(w_in_t, l, narrow0, gate0 - narrow0, (gate0 - narrow0) // 3, "cast_narrow")
        wg = _cast_rows(w_in_t, l, gate0, 3 * D_MODEL, 512, "cast_gates")

        if l == 0:
            h = _norm_mod(x[0], norm1[l], mod, l, 0)
            h = _norm_mod(x[1], norm1[l], mod, l, 0, row0=N_PROMPT, o_prev=h)
        else:
            h = _norm_mod(x, norm1[l], mod, l, 0)
        z = _matmul(h, w_in_t, F32, l, Z_MAIN, transposed=True, name="in_proj")
        zs = _in_small(h, ws)

        o_a, sa = _hgrn(z, hgrn_lb_logits, hgrn_out_norm[l], None, l)
        (o_a,) = _hgrn(z, hgrn_lb_logits, hgrn_out_norm[l], state_hgrn, l, o_prev=o_a)
        o_b, sb = _gla(z, zs, wlr, gla_b_gate[l], gla_out_norm[l], None, l)
        (o_b,) = _gla(z, zs, wlr, gla_b_gate[l], gla_out_norm[l], state_gla, l, o_prev=o_b)
        hgrn_states.append(sa)
        gla_states.append(sb)

        qlat, ckvk = _mla_prep(zs, mla_q_a_norm[l], mla_kv_a_norm[l])
        caches.append(ckvk[:N_PROMPT].reshape(BATCH, SEQ, KV_LORA + QK_ROPE))
        qn, qr = _qup(qlat, wq, qgn, qgr, cos_q, sin_q)
        kn_p, kr_p, v_p = _kvup(ckvk, wkv, kgn, kgr)
        kn_s, kr_s, v_s = _kvup(ckvk, wkv, kgn, kgr, cache_mla, l, cos_s, sin_s)
        o_c = _attention(qn, qr, kn_p, kr_p, v_p)
        o_c = _attention(qn, qr, kn_s, kr_s, v_s, o_prev=o_c)

        merged = _merge(h, wg, o_a, wb_a, o_b, wb_b, o_c, wb_c, l)
        if l == 0:
            xo = _matmul_residual(merged, w_out, x[0], mod, l, 2, 0, "out_proj", rows=ctx_rows)
            x = _matmul_residual(merged, w_out, x[1], mod, l, 2, 0, "out_proj", rows=lat_rows,
                                 res_row0=0, o_prev=xo)
        else:
            x = _matmul_residual(merged, w_out, x, mod, l, 2, 0, "out_proj")

        h2 = _norm_mod(x, norm2[l], mod, l, 3)
        u = _matmul(h2, w_mlp_in, BF16, l, D_FF, relu2=True, name="mlp_in")
        n_kb = D_FF // D_MODEL
        for kb in range(n_kb):
            if l == DEPTH - 1 and kb == n_kb - 1:
                x = tuple(_matmul_residual(u, w_mlp_out, x, mod, l, 5, kb, "mlp_out", rows=rows,
                                           out_rows=rows[1], out_row0=0)
                          for rows in (ctx_rows, lat_rows))
            else:
                x = _matmul_residual(u, w_mlp_out, x, mod, l, 5, kb, "mlp_out")

    y_prompt = x[0].reshape(BATCH, SEQ, D_MODEL)
    y_sample = x[1].reshape(DEC_BATCH, DEC_SEQ, D_MODEL)
    return (y_prompt, y_sample, jnp.stack(caches, axis=1), jnp.stack(hgrn_states, axis=1),
            jnp.stack(gla_states, axis=1))
```

```python
import functools

import numpy as np
import jax
import jax.numpy as jnp
from jax import lax
from jax.experimental import pallas as pl
from jax.experimental.pallas import tpu as pltpu

F32 = jnp.float32
BF16 = jnp.bfloat16

D_MODEL = 4096
BATCH = 32
SEQ = 256
DEPTH = 2
DEC_BATCH = 2
DEC_SEQ = 1024
PAST_LEN = 512
GRID_W = 64
A_HEADS = 8
A_DK = 128
A_DV = 128
B_HEADS = 4
B_DK = 128
B_DV = 256
GLA_RANK = 16
GLA_GATE_NORM = 16.0
C_HEADS = 16
Q_LORA = 1024
KV_LORA = 512
QK_NOPE = 128
QK_ROPE = 64
C_DV = 128
QK_DIM = QK_NOPE + QK_ROPE
ROPE_AXIS = QK_ROPE // 2
ROPE_BASE = 10000.0
N_MOD = 6
D_FF = 4 * D_MODEL
EPS = 1e-6

N_PROMPT = BATCH * SEQ
N_SAMPLE = DEC_BATCH * DEC_SEQ
N_TOK = N_PROMPT + N_SAMPLE
MOD_ROWS = 8

Z_MAIN = 8192
ZS_CQ = 0
ZS_CKV = ZS_CQ + Q_LORA
ZS_KR = ZS_CKV + KV_LORA
ZS_LR = ZS_KR + 128
ZS_COLS = ZS_LR + 128
ROW_ALIGN = 32

LANES = 128
MXU_COLS = 256
V7X_VMEM_BYTES = 64 * 1024 * 1024
VMEM_BIG = 60 * 1024 * 1024
VMEM_MID = 40 * 1024 * 1024

LOG2_E = 1.4426950408889634
CH = 32
SB = 8


def _params(sem, vmem=VMEM_MID):
    return pltpu.CompilerParams(dimension_semantics=sem, vmem_limit_bytes=vmem)


def _mod_row(i, tm):
    return jnp.where(i < N_PROMPT // tm, 0, 1 + (i - N_PROMPT // tm) // (DEC_SEQ // tm))


def _sigmoid(x):
    return 1.0 / (1.0 + jnp.exp(-x))


def _log1p_of_exp_neg(t):
    return jnp.log(1.0 + jnp.exp(-t))


def _log_sigmoid(x):
    return jnp.minimum(x, 0.0) - _log1p_of_exp_neg(jnp.abs(x))


def _mod_kernel(cond_ref, w_ref, b_ref, o_ref):
    c = cond_ref[...]
    s = (c * _sigmoid(c)).astype(BF16)
    o_ref[...] = jnp.dot(s, w_ref[...].astype(BF16), preferred_element_type=F32) + b_ref[...]


def _modulation(cond, w_mod, b_mod):
    tn = 512
    n = N_MOD * D_MODEL
    return pl.pallas_call(
        _mod_kernel,
        grid=(DEPTH, n // tn),
        in_specs=[
            pl.BlockSpec((MOD_ROWS, D_MODEL), lambda l, j: (0, 0)),
            pl.BlockSpec((None, D_MODEL, tn), lambda l, j: (l, 0, j)),
            pl.BlockSpec((None, 1, tn), lambda l, j: (l, 0, j)),
        ],
        out_specs=pl.BlockSpec((None, MOD_ROWS, tn), lambda l, j: (l, 0, j)),
        out_shape=jax.ShapeDtypeStruct((DEPTH, MOD_ROWS, n), F32),
        compiler_params=_params(("parallel", "parallel")),
        name="modulation",
    )(cond, w_mod, b_mod.reshape(DEPTH, 1, n))


def _norm_mod_kernel(x_ref, g_ref, sh_ref, sc_ref, *rest):
    o_ref = rest[-1]
    x = x_ref[...]
    ms = jnp.mean(x * x, axis=-1, keepdims=True)
    y = x * lax.rsqrt(ms + EPS) * g_ref[...]
    o_ref[...] = (y * (1.0 + sc_ref[...]) + sh_ref[...]).astype(BF16)


def _norm_mod(x, gain, mod, layer, shift_chunk, row0=0, o_prev=None):
    tm = 512
    base = layer * MOD_ROWS
    blk0 = row0 // tm
    mod_spec = lambda chunk: pl.BlockSpec(
        (None, 1, D_MODEL), lambda i: (base + _mod_row(blk0 + i, tm), 0, chunk))
    in_specs = [pl.BlockSpec((tm, D_MODEL), lambda i: (i, 0)),
                pl.BlockSpec((1, D_MODEL), lambda i: (0, 0)),
                mod_spec(shift_chunk), mod_spec(shift_chunk + 1)]
    args = [x, gain.reshape(1, D_MODEL), mod, mod]
    if o_prev is not None:
        in_specs.append(pl.BlockSpec(memory_space=pl.ANY))
        args.append(o_prev)
    return pl.pallas_call(
        _norm_mod_kernel,
        grid=(x.shape[0] // tm,),
        in_specs=in_specs,
        out_specs=pl.BlockSpec((tm, D_MODEL), lambda i: (blk0 + i, 0)),
        out_shape=jax.ShapeDtypeStruct((N_TOK, D_MODEL), BF16),
        input_output_aliases={} if o_prev is None else {4: 0},
        compiler_params=_params(("parallel",)),
        name="norm_mod",
    )(*args)


def _cast_weight(w_ref, wb_ref):
    @pl.when(pl.program_id(1) == 0)
    def _():
        wb_ref[...] = w_ref[...].astype(BF16)
    return wb_ref


TRANS_B = (((1,), (1,)), ((), ()))


def _mm_kernel(x_ref, w_ref, o_ref, wb_ref, *, relu2, transposed):
    w_ref = _cast_weight(w_ref, wb_ref)
    if transposed:
        acc = lax.dot_general(x_ref[...], w_ref[...], TRANS_B, preferred_element_type=F32)
    else:
        acc = jnp.dot(x_ref[...], w_ref[...], preferred_element_type=F32)
    if relu2:
        acc = jnp.square(jnp.maximum(acc, 0.0))
    o_ref[...] = acc.astype(o_ref.dtype)


def _matmul(x, w, out_dtype, layer, n_cols, relu2=False, transposed=False, name="matmul"):
    tm, tn = 512, 1024
    m, k = x.shape
    if transposed:
        w_spec = pl.BlockSpec((None, tn, k), lambda j, i: (layer, j, 0))
    else:
        w_spec = pl.BlockSpec((None, k, tn), lambda j, i: (layer, 0, j))
    return pl.pallas_call(
        functools.partial(_mm_kernel, relu2=relu2, transposed=transposed),
        grid=(n_cols // tn, m // tm),
        in_specs=[pl.BlockSpec((tm, k), lambda j, i: (i, 0)), w_spec],
        out_specs=pl.BlockSpec((tm, tn), lambda j, i: (i, j)),
        out_shape=jax.ShapeDtypeStruct((m, n_cols), out_dtype),
        scratch_shapes=[pltpu.VMEM((tn, k) if transposed else (k, tn), BF16)],
        compiler_params=_params(("parallel", "arbitrary"), VMEM_BIG),
        name=name,
    )(x, w)


def _cast_rows_kernel(w_ref, o_ref):
    o_ref[...] = w_ref[0].astype(BF16)


def _cast_rows(w, layer, row0, n_rows, block_rows, name):
    k = w.shape[2]
    assert row0 % ROW_ALIGN == 0 and block_rows % ROW_ALIGN == 0 and n_rows % block_rows == 0
    start = lambda r: pl.multiple_of(ROW_ALIGN * (row0 // ROW_ALIGN + (block_rows // ROW_ALIGN) * r),
                                     ROW_ALIGN)
    return pl.pallas_call(
        _cast_rows_kernel,
        grid=(n_rows // block_rows,),
        in_specs=[pl.BlockSpec((pl.Element(1), pl.Element(block_rows), pl.Element(k)),
                               lambda r: (layer, start(r), 0))],
        out_specs=pl.BlockSpec((block_rows, k), lambda r: (r, 0)),
        out_shape=jax.ShapeDtypeStruct((n_rows, k), BF16),
        compiler_params=_params(("parallel",)),
        name=name,
    )(w)


def _in_small_kernel(x_ref, w_ref, o_ref):
    x = x_ref[...]
    tm = x.shape[0]
    n_lr, n_kv = 2 * GLA_RANK, KV_LORA + QK_ROPE
    part = lambda r0, rows: lax.dot_general(x, w_ref[r0:r0 + rows, :], TRANS_B,
                                            preferred_element_type=F32)
    o_ref[:, ZS_LR:ZS_LR + n_lr] = part(0, n_lr)
    o_ref[:, ZS_LR + n_lr:] = jnp.zeros((tm, ZS_COLS - ZS_LR - n_lr), F32)
    o_ref[:, ZS_CQ:ZS_CQ + Q_LORA] = part(n_lr, Q_LORA)
    o_ref[:, ZS_CKV:ZS_CKV + n_kv] = part(n_lr + Q_LORA, n_kv)
    o_ref[:, ZS_CKV + n_kv:ZS_LR] = jnp.zeros((tm, ZS_LR - ZS_CKV - n_kv), F32)


def _in_small(h, w_small):
    tm = 512
    return pl.pallas_call(
        _in_small_kernel,
        grid=(N_TOK // tm,),
        in_specs=[pl.BlockSpec((tm, D_MODEL), lambda i: (i, 0)),
                  pl.BlockSpec(w_small.shape, lambda i: (0, 0))],
        out_specs=pl.BlockSpec((tm, ZS_COLS), lambda i: (i, 0)),
        out_shape=jax.ShapeDtypeStruct((N_TOK, ZS_COLS), F32),
        compiler_params=_params(("parallel",), VMEM_BIG),
        name="in_proj_small",
    )(h, w_small)


def _mm_res_kernel(x_ref, w_ref, res_ref, g_ref, *rest):
    o_ref, wb_ref = rest[-2:]
    w_ref = _cast_weight(w_ref, wb_ref)
    x = x_ref[...]
    for c0 in range(0, o_ref.shape[1], MXU_COLS):
        cols = slice(c0, c0 + MXU_COLS)
        part = jnp.dot(x, w_ref[:, cols], preferred_element_type=F32)
        o_ref[:, cols] = res_ref[:, cols] + g_ref[:, cols] * part


def _matmul_residual(a, w, res, mod, layer, gate_chunk, k_block, name, rows=(0, N_TOK),
                     res_row0=None, out_rows=N_TOK, out_row0=None, o_prev=None):
    tm, tn, tk = 512, 1024, D_MODEL
    n = w.shape[2]
    base = layer * MOD_ROWS
    cpc = D_MODEL // tn
    blk0 = rows[0] // tm
    res_blk0 = blk0 if res_row0 is None else res_row0 // tm
    out_blk0 = blk0 if out_row0 is None else out_row0 // tm
    in_specs = [
        pl.BlockSpec((tm, tk), lambda j, i: (blk0 + i, k_block)),
        pl.BlockSpec((None, tk, tn), lambda j, i: (layer, k_block, j)),
        pl.BlockSpec((tm, tn), lambda j, i: (res_blk0 + i, j)),
        pl.BlockSpec((None, 1, tn),
                     lambda j, i: (base + _mod_row(blk0 + i, tm), 0, gate_chunk * cpc + j)),
    ]
    args = [a, w, res, mod]
    if o_prev is not None:
        in_specs.append(pl.BlockSpec(memory_space=pl.ANY))
        args.append(o_prev)
    return pl.pallas_call(
        _mm_res_kernel,
        grid=(n // tn, rows[1] // tm),
        in_specs=in_specs,
        out_specs=pl.BlockSpec((tm, tn), lambda j, i: (out_blk0 + i, j)),
        out_shape=jax.ShapeDtypeStruct((out_rows, n), F32),
        input_output_aliases={} if o_prev is None else {4: 0},
        scratch_shapes=[pltpu.VMEM((tk, tn), BF16)],
        compiler_params=_params(("parallel", "arbitrary"), VMEM_BIG),
        name=name,
    )(*args)


def _merge_kernel(h_ref, wga_ref, wgb_ref, wgc_ref, oa_ref, wa_ref, ob_ref, wb_ref, oc_ref, wc_ref,
                  o_ref):
    h = h_ref[...]
    branches = ((wga_ref, oa_ref[...], wa_ref), (wgb_ref, ob_ref[...], wb_ref),
                (wgc_ref, oc_ref[...], wc_ref))
    for c0 in range(0, o_ref.shape[1], MXU_COLS):
        cols = slice(c0, c0 + MXU_COLS)
        acc = None
        for wg_ref, o_b, w_ref in branches:
            g = lax.dot_general(h, wg_ref[cols, :], TRANS_B, preferred_element_type=F32)
            p = jnp.dot(o_b, w_ref[:, cols], preferred_element_type=F32)
            term = _sigmoid(g) * p
            acc = term if acc is None else acc + term
        o_ref[:, cols] = acc.astype(BF16)


def _merge(h, wg, o_a, w_a, o_b, w_b, o_c, w_c, layer):
    tm, tn = 512, 512
    nj = D_MODEL // tn
    act = lambda width: pl.BlockSpec((tm, width), lambda j, i: (i, 0))
    wcol = lambda rows: pl.BlockSpec((None, rows, tn), lambda j, i: (layer, 0, j))
    gate = lambda b: pl.BlockSpec((tn, D_MODEL), lambda j, i: (b * nj + j, 0))
    return pl.pallas_call(
        _merge_kernel,
        grid=(nj, N_TOK // tm),
        in_specs=[act(D_MODEL), gate(0), gate(1), gate(2),
                  act(o_a.shape[1]), wcol(w_a.shape[1]),
                  act(o_b.shape[1]), wcol(w_b.shape[1]),
                  act(o_c.shape[1]), wcol(w_c.shape[1])],
        out_specs=pl.BlockSpec((tm, tn), lambda j, i: (i, j)),
        out_shape=jax.ShapeDtypeStruct((N_TOK, D_MODEL), BF16),
        compiler_params=_params(("parallel", "parallel"), VMEM_BIG),
        name="merge",
    )(h, wg, wg, wg, o_a, w_a, o_b, w_b, o_c, w_c)


PAD = 8


def _shifted(ref, n_rows, d, reverse):
    return ref[pl.ds(PAD + d if reverse else PAD - d, n_rows), :]


def _seg_cumsum(g, rowc, reverse, c_scr):
    n_rows = g.shape[0]
    x = g
    for sh in (1, 2, 4, 8, 16):
        mask = rowc < CH - sh if reverse else rowc >= sh
        if sh < PAD:
            c_scr[pl.ds(PAD, n_rows), :] = x
            moved = _shifted(c_scr, n_rows, sh, reverse)
        else:
            moved = pltpu.roll(x, n_rows - sh if reverse else sh, 0)
        x = x + jnp.where(mask, moved, 0.0)
    return x


def _scan_dir(q, k, v, g, s0t, reverse, scratch, accumulate, want_state):
    o_scr, k_scr, c_scr, v_scr = scratch
    n_rows, dk = q.shape
    n = n_rows // CH
    rowc = lax.broadcasted_iota(jnp.int32, (n_rows, dk), 0) & (CH - 1)
    cum = _seg_cumsum(g * LOG2_E, rowc, reverse, c_scr)
    c_scr[pl.ds(PAD, n_rows), :] = cum
    k_scr[pl.ds(PAD, n_rows), :] = k

    def rows_of(r, count):
        return jnp.broadcast_to(cum[r:r + 1, :], (count, dk))

    def chunk_row(r):
        return jnp.concatenate([rows_of(c * CH + r, CH) for c in range(n)], axis=0)

    last_row = 0 if reverse else CH - 1
    last_b = chunk_row(last_row)
    qt = (q * jnp.exp2(cum)).astype(BF16)
    kh = (k * jnp.exp2(last_b - cum)).astype(BF16)
    vb = v.astype(BF16)

    rows = rowc & (SB - 1)
    o_diag = jnp.sum(q * k, axis=-1, keepdims=True) * v
    for d in range(1, SB):
        mask = rows < SB - d if reverse else rows >= d
        kd = _shifted(k_scr, n_rows, d, reverse)
        cd = _shifted(c_scr, n_rows, d, reverse)
        vd = _shifted(v_scr, n_rows, d, reverse)
        e = jnp.exp2(jnp.where(mask, cum - cd, -jnp.inf))
        o_diag = o_diag + jnp.sum(q * kd * e, axis=-1, keepdims=True) * vd
    if accumulate:
        o_scr[...] += o_diag
    else:
        o_scr[...] = o_diag

    sub = rowc // SB
    nsb = CH // SB
    if reverse:
        own = [rows_of(c * CH + SB * min(i + 1, nsb - 1), SB) for c in range(n) for i in range(nsb)]
        has_piv = sub < nsb - 1
    else:
        own = [rows_of(c * CH + SB * max(i, 1) - 1, SB) for c in range(n) for i in range(nsb)]
        has_piv = sub >= 1
    qe = q * jnp.exp2(jnp.where(has_piv, cum - jnp.concatenate(own, axis=0), -jnp.inf))
    qs, ks = [], []
    for pr in range(SB, CH, SB):
        if reverse:
            piv, qmask, kmask = chunk_row(pr), sub == pr // SB - 1, rowc >= pr
        else:
            piv, qmask, kmask = chunk_row(pr - 1), sub == pr // SB, rowc < pr
        qs.append(jnp.where(qmask, qe, 0.0).astype(BF16))
        ks.append((k * jnp.exp2(jnp.where(kmask, piv - cum, -jnp.inf))).astype(BF16))
    qcat = jnp.concatenate(qs, axis=-1)
    kcat = jnp.concatenate(ks, axis=-1)

    trans_a = (((0,), (0,)), ((), ()))
    order = list(range(n - 1, -1, -1) if reverse else range(n))
    rows = [slice(c * CH, (c + 1) * CH) for c in order]
    n_upd = n if want_state else n - 1
    uts = [lax.dot_general(vb[sl], kh[sl], trans_a, preferred_element_type=F32) for sl in rows[:n_upd]]
    atts = [lax.dot_general(qcat[sl], kcat[sl], TRANS_B, preferred_element_type=F32).astype(BF16)
            for sl in rows]
    o_off = [jnp.dot(a, vb[sl], preferred_element_type=F32) for a, sl in zip(atts, rows)]
    st = s0t
    states = []
    for idx, c in enumerate(order):
        states.append(st.astype(BF16))
        if idx < n_upd:
            r = c * CH + last_row
            st = st * jnp.exp2(cum[r:r + 1, :]) + uts[idx]
    for sl, s_in, off in zip(rows, states, o_off):
        o_scr[sl, :] += lax.dot_general(qt[sl], s_in, TRANS_B, preferred_element_type=F32) + off
    return st


def _finish_scan(o_scr, gain_ref, zg_ref, o_ref):
    o = o_scr[...]
    ms = jnp.mean(o * o, axis=-1, keepdims=True)
    zg = zg_ref[...]
    o_ref[...] = (o * lax.rsqrt(ms + EPS) * gain_ref[...] * (zg * _sigmoid(zg))).astype(BF16)


def _split_scan_refs(refs, n_in, sample):
    pos = n_in
    s0_ref = None
    if sample:
        s0_ref = refs[pos]
        pos += 2
    o_ref = refs[pos]
    st_ref = None if sample else refs[pos + 1]
    scratch = refs[pos + (1 if sample else 2):]
    return refs[:n_in], s0_ref, o_ref, st_ref, scratch


def _prep_scan_scratch(scratch, v):
    _, k_scr, c_scr, v_scr = scratch
    n_rows = v.shape[0]
    for ref in (k_scr, c_scr, v_scr):
        zero = jnp.zeros((PAD, ref.shape[1]), F32)
        ref[pl.ds(0, PAD), :] = zero
        ref[pl.ds(PAD + n_rows, PAD), :] = zero
    v_scr[pl.ds(PAD, n_rows), :] = v


def _scan_scratch(seq, dk, dv):
    return [pltpu.VMEM((seq, dv), F32), pltpu.VMEM((seq + 2 * PAD, dk), F32),
            pltpu.VMEM((seq + 2 * PAD, dk), F32), pltpu.VMEM((seq + 2 * PAD, dv), F32)]


def _scan_kernel(*refs, head_fn, lane_widths, sample, hps):
    ins, s0_ref, o_ref, st_ref, scratch = _split_scan_refs(refs, len(lane_widths), sample)
    per_head = len(scratch) // hps
    dv = o_ref.shape[1] // hps

    def lanes(ref, width, hh):
        if width is None:
            return ref
        return ref.at[(slice(None),) * (len(ref.shape) - 1) + (pl.ds(width * hh, width),)]

    for hh in range(hps):
        head_fn([lanes(r, w, hh) for r, w in zip(ins, lane_widths)],
                None if s0_ref is None else s0_ref.at[:, hh],
                lanes(o_ref, dv, hh),
                None if st_ref is None else st_ref.at[:, hh],
                scratch[per_head * hh:per_head * (hh + 1)])


def _hgrn_head(ins, s0_ref, o_ref, st_ref, scratch, *, layer):
    lbl_ref, q_ref, ff_ref, fb_ref, i_ref, zg_ref, gain_ref = ins
    has_state, want_state = s0_ref is not None, st_ref is not None

    logits = [lbl_ref[l] for l in range(DEPTH)]
    mx = functools.reduce(jnp.maximum, logits)
    ex = [jnp.exp(x - mx) for x in logits]
    tot = functools.reduce(lambda a, b: a + b, ex)
    probs = [e / tot for e in ex]
    cum_first = probs[0]
    cum_l = functools.reduce(lambda a, b: a + b, probs[:layer + 1])
    lb = cum_l - cum_first

    q = q_ref[...]
    v = i_ref[...]
    _prep_scan_scratch(scratch, v)
    for d, (z_ref, reverse) in enumerate(((ff_ref, False), (fb_ref, True))):
        lbd = lb[d:d + 1, :]
        zf = z_ref[...]
        a = jnp.log(lbd)
        b = jnp.log1p(-lbd) + _log_sigmoid(zf)
        delta = a - b
        log_f = jnp.where(delta != delta, a + b,
                          jnp.maximum(a, b) + _log1p_of_exp_neg(jnp.abs(delta)))
        one_minus_f = (1.0 - lbd) * _sigmoid(-zf)
        if has_state:
            s0t = s0_ref[d].T
        else:
            s0t = jnp.zeros((A_DV, A_DK), F32)
        st = _scan_dir(q, one_minus_f, v, log_f, s0t, reverse, scratch, d == 1, want_state)
        if want_state:
            st_ref[d] = st.T
    _finish_scan(scratch[0], gain_ref, zg_ref, o_ref)


def _scan_call(head_fn, name, inputs, heads, dk, dv, state, layer, o_prev):
    sample = o_prev is not None
    seq, nb = (DEC_SEQ, DEC_BATCH) if sample else (SEQ, BATCH)
    hps = 1 if sample else 2
    row0 = N_PROMPT // seq if sample else 0
    in_specs, args, lane_widths = [], [], []
    for arr, kind, *par in inputs:
        args.append(arr)
        if kind == "cols":
            col0, width = par
            blk = col0 // (width * hps)
            in_specs.append(pl.BlockSpec((seq, width * hps), lambda b, h, blk=blk: (row0 + b, blk + h)))
            lane_widths.append(width)
        elif kind == "shared_cols":
            col0, width = par
            in_specs.append(pl.BlockSpec((seq, width), lambda b, h, blk=col0 // width: (row0 + b, blk)))
            lane_widths.append(None)
        elif kind == "heads_last":
            (width,) = par
            lead = arr.shape[:-1]
            in_specs.append(pl.BlockSpec(lead + (width * hps,),
                                         lambda b, h, n=len(lead): (0,) * n + (h,)))
            lane_widths.append(width)
        else:
            in_specs.append(pl.BlockSpec(arr.shape, lambda b, h, n=arr.ndim: (0,) * n))
            lane_widths.append(None)
    aliases = {}
    if sample:
        in_specs += [pl.BlockSpec((None, None, 2, hps, dk, dv), lambda b, h: (b, layer, 0, h, 0, 0)),
                     pl.BlockSpec(memory_space=pl.ANY)]
        aliases = {len(args) + 1: 0}
        args += [state, o_prev]
    out_specs = [pl.BlockSpec((seq, dv * hps), lambda b, h: (row0 + b, h))]
    out_shape = [jax.ShapeDtypeStruct((N_TOK, heads * dv), BF16)]
    if not sample:
        out_specs.append(pl.BlockSpec((None, 2, hps, dk, dv), lambda b, h: (b, 0, h, 0, 0)))
        out_shape.append(jax.ShapeDtypeStruct((nb, 2, heads, dk, dv), F32))
    return pl.pallas_call(
        functools.partial(_scan_kernel, head_fn=head_fn, lane_widths=tuple(lane_widths),
                          sample=sample, hps=hps),
        grid=(nb, heads // hps),
        in_specs=in_specs,
        out_specs=out_specs,
        out_shape=out_shape,
        input_output_aliases=aliases,
        scratch_shapes=_scan_scratch(seq, dk, dv) * hps,
        compiler_params=_params(("parallel", "parallel")),
        name=name + ("_sample" if sample else "_prompt"),
    )(*args)


def _hgrn(z, lb_logits, out_gain, state, layer, o_prev=None):
    width = A_HEADS * A_DK
    inputs = [(lb_logits, "heads_last", A_DK)]
    inputs += [(z, "cols", g * width, A_DK) for g in range(5)]
    inputs += [(out_gain.reshape(1, A_DV), "whole")]
    return _scan_call(functools.partial(_hgrn_head, layer=layer), "hgrn", inputs,
                      A_HEADS, A_DK, A_DV, state, layer, o_prev)


def _gla_head(ins, s0_ref, o_ref, st_ref, scratch):
    q_ref, k_ref, v_ref, zr_ref, lr_ref, wg_ref, bg_ref, gain_ref = ins
    has_state, want_state = s0_ref is not None, st_ref is not None

    q = q_ref[...] * (B_DK ** -0.5)
    k = k_ref[...]
    v = v_ref[...]
    _prep_scan_scratch(scratch, v)
    lr = lr_ref[...].astype(BF16)
    for d, reverse in enumerate((False, True)):
        logit = jnp.dot(lr, wg_ref[d], preferred_element_type=F32) + bg_ref[d]
        g = _log_sigmoid(logit) / GLA_GATE_NORM
        if has_state:
            s0t = s0_ref[d].T
        else:
            s0t = jnp.zeros((B_DV, B_DK), F32)
        st = _scan_dir(q, k, v, g, s0t, reverse, scratch, d == 1, want_state)
        if want_state:
            st_ref[d] = st.T
    _finish_scan(scratch[0], gain_ref, zr_ref, o_ref)


def _gla(z, zs, w_gate_ext, b_gate, out_gain, state, layer, o_prev=None):
    inputs = [(z, "cols", 5120, B_DK), (z, "cols", 5632, B_DK), (z, "cols", 6144, B_DV),
              (z, "cols", 7168, B_DV), (zs, "shared_cols", ZS_LR, LANES),
              (w_gate_ext, "heads_last", B_DK), (b_gate.reshape(2, 1, B_HEADS * B_DK), "heads_last", B_DK),
              (out_gain.reshape(1, B_DV), "whole")]
    return _scan_call(_gla_head, "gla", inputs, B_HEADS, B_DK, B_DV, state, layer, o_prev)


def _mla_prep_kernel(cq_ref, ckv_ref, kr_ref, gq_ref, gkv_ref, qlat_ref, ckvk_ref):
    cq = cq_ref[...]
    qlat_ref[...] = (cq * lax.rsqrt(jnp.mean(cq * cq, axis=-1, keepdims=True) + EPS)
                     * gq_ref[...]).astype(BF16)
    ckv = ckv_ref[...]
    ckvk_ref[:, :KV_LORA] = (ckv * lax.rsqrt(jnp.mean(ckv * ckv, axis=-1, keepdims=True) + EPS)
                             * gkv_ref[...])
    ckvk_ref[:, KV_LORA:] = kr_ref[:, :QK_ROPE]


def _mla_prep(z, q_gain, kv_gain):
    tm = 512
    return pl.pallas_call(
        _mla_prep_kernel,
        grid=(N_TOK // tm,),
        in_specs=[
            pl.BlockSpec((tm, Q_LORA), lambda i: (i, ZS_CQ // Q_LORA)),
            pl.BlockSpec((tm, KV_LORA), lambda i: (i, ZS_CKV // KV_LORA)),
            pl.BlockSpec((tm, LANES), lambda i: (i, ZS_KR // LANES)),
            pl.BlockSpec((1, Q_LORA), lambda i: (0, 0)),
            pl.BlockSpec((1, KV_LORA), lambda i: (0, 0)),
        ],
        out_specs=[pl.BlockSpec((tm, Q_LORA), lambda i: (i, 0)),
                   pl.BlockSpec((tm, KV_LORA + QK_ROPE), lambda i: (i, 0))],
        out_shape=[jax.ShapeDtypeStruct((N_TOK, Q_LORA), BF16),
                   jax.ShapeDtypeStruct((N_TOK, KV_LORA + QK_ROPE), F32)],
        compiler_params=_params(("parallel",)),
        name="mla_prep",
    )(z, z, z, q_gain.reshape(1, Q_LORA), kv_gain.reshape(1, KV_LORA))


def _swap_halves(t, lane):
    width = t.shape[-1]
    return jnp.where((lane & 31) < 16, pltpu.roll(t, width - 16, 1), pltpu.roll(t, 16, 1))


def _qup_kernel(x_ref, w_ref, gn_ref, gr_ref, c_ref, s_ref, qn_ref, qr_ref):
    x = x_ref[...]
    nope_w = C_HEADS * QK_NOPE
    tm = x.shape[0]
    lane = lax.broadcasted_iota(jnp.int32, (tm, LANES), 1)
    low = lane < QK_ROPE
    gn = gn_ref[...]
    gr = gr_ref[...]
    cos = c_ref[...]
    sin = s_ref[...]
    strip = lambda c0: jnp.dot(x, w_ref[:, c0:c0 + MXU_COLS], preferred_element_type=F32)
    for p in range(C_HEADS // 2):
        acc = strip(MXU_COLS * p)
        if p % 2 == 0:
            rope2 = strip(nope_w + LANES * p)
        r2 = rope2[:, LANES * (p % 2):LANES * (p % 2 + 1)]
        sq = r2 * r2
        ss_rope = (jnp.sum(jnp.where(low, sq, 0.0), axis=-1, keepdims=True),
                   jnp.sum(jnp.where(low, 0.0, sq), axis=-1, keepdims=True))
        inv = []
        for hh in range(2):
            h = 2 * p + hh
            nh = acc[:, QK_NOPE * hh: QK_NOPE * (hh + 1)]
            ssn = jnp.sum(nh * nh, axis=-1, keepdims=True)
            r = lax.rsqrt((ssn + ss_rope[hh]) / QK_DIM + EPS)
            qn_ref[:, QK_NOPE * h: QK_NOPE * (h + 1)] = (nh * r * gn).astype(BF16)
            inv.append(r)
        t = r2 * jnp.where(low, inv[0], inv[1]) * gr
        qr_ref[:, LANES * p: LANES * (p + 1)] = (t * cos + _swap_halves(t, lane) * sin).astype(BF16)


def _qup(qlat, w, gn, gr, cos, sin):
    tm = 512
    n_rows = qlat.shape[0]
    row = lambda width: pl.BlockSpec((tm, width), lambda i: (i, 0))
    full = lambda a: pl.BlockSpec(a.shape, lambda i: (0, 0))
    return pl.pallas_call(
        _qup_kernel,
        grid=(n_rows // tm,),
        in_specs=[row(Q_LORA), full(w), full(gn), full(gr), row(LANES), row(LANES)],
        out_specs=[row(C_HEADS * QK_NOPE), row(C_HEADS * QK_ROPE)],
        out_shape=[jax.ShapeDtypeStruct((n_rows, C_HEADS * QK_NOPE), BF16),
                   jax.ShapeDtypeStruct((n_rows, C_HEADS * QK_ROPE), BF16)],
        compiler_params=_params(("parallel",)),
        name="mla_q_up",
    )(qlat, w, gn, gr, cos, sin)


def _kvup_kernel(*refs, sample):
    if sample:
        cache_ref, x_ref, w_ref, gn_ref, gr_ref, c_ref, s_ref, kn_ref, kr_ref, v_ref = refs
        from_cache = pl.program_id(1) == 0
        x = jnp.where(from_cache, cache_ref[...], x_ref[...])
    else:
        x_ref, w_ref, gn_ref, gr_ref, kn_ref, kr_ref, v_ref = refs
        x = x_ref[...]
    ckv = x[:, :KV_LORA].astype(BF16)
    nope_w = C_HEADS * QK_NOPE
    tm = x.shape[0]
    lane = lax.broadcasted_iota(jnp.int32, (tm, LANES), 1)
    low = lane < QK_ROPE
    kr = x[:, KV_LORA:]
    ss_rope = jnp.sum(kr * kr, axis=-1, keepdims=True)
    rot = jnp.concatenate([kr, kr], axis=-1) * gr_ref[...]
    if sample:
        turned = rot * c_ref[...] + _swap_halves(rot, lane) * s_ref[...]
        rot = jnp.where(from_cache, rot, turned)
    gn = gn_ref[...]
    for p in range(C_HEADS // 2):
        pair = slice(2 * QK_NOPE * p, 2 * QK_NOPE * (p + 1))
        acc = jnp.dot(ckv, w_ref[:, pair], preferred_element_type=F32)
        v_ref[:, pair] = jnp.dot(ckv, w_ref[:, nope_w + pair.start:nope_w + pair.stop],
                                 preferred_element_type=F32).astype(BF16)
        inv = []
        for hh in range(2):
            h = 2 * p + hh
            nh = acc[:, QK_NOPE * hh: QK_NOPE * (hh + 1)]
            ssn = jnp.sum(nh * nh, axis=-1, keepdims=True)
            r = lax.rsqrt((ssn + ss_rope) / QK_DIM + EPS)
            kn_ref[:, QK_NOPE * h: QK_NOPE * (h + 1)] = (nh * r * gn).astype(BF16)
            inv.append(r)
        kr_ref[:, LANES * p: LANES * (p + 1)] = (rot * jnp.where(low, inv[0], inv[1])).astype(BF16)


def _kvup(ckvk, w, gn, gr, cache=None, layer=None, cos=None, sin=None):
    tm = PAST_LEN
    sample = cache is not None
    width = KV_LORA + QK_ROPE
    if sample:
        per_seq = 1 + DEC_SEQ // tm
        grid = (DEC_BATCH, per_seq)
        row_blk = lambda b, t: b * per_seq + t
        new_blk = lambda b, t: N_PROMPT // tm + b * (per_seq - 1) + jnp.maximum(t - 1, 0)
        full = lambda a: pl.BlockSpec(a.shape, lambda b, t: (0, 0))
        table = pl.BlockSpec((tm, LANES), lambda b, t: (jnp.maximum(t - 1, 0), 0))
        in_specs = [pl.BlockSpec((None, None, tm, width), lambda b, t: (b, layer, 0, 0)),
                    pl.BlockSpec((tm, width), lambda b, t: (new_blk(b, t), 0)),
                    full(w), full(gn), full(gr), table, table]
        args = (cache, ckvk, w, gn, gr, cos, sin)
        row = lambda wd: pl.BlockSpec((tm, wd), lambda b, t: (row_blk(b, t), 0))
        n_rows = DEC_BATCH * per_seq * tm
    else:
        grid = (N_PROMPT // tm,)
        full = lambda a: pl.BlockSpec(a.shape, lambda i: (0, 0))
        in_specs = [pl.BlockSpec((tm, width), lambda i: (i, 0)), full(w), full(gn), full(gr)]
        args = (ckvk, w, gn, gr)
        row = lambda wd: pl.BlockSpec((tm, wd), lambda i: (i, 0))
        n_rows = N_PROMPT
    return pl.pallas_call(
        functools.partial(_kvup_kernel, sample=sample),
        grid=grid,
        in_specs=in_specs,
        out_specs=[row(C_HEADS * QK_NOPE), row(C_HEADS * QK_ROPE), row(C_HEADS * C_DV)],
        out_shape=[jax.ShapeDtypeStruct((n_rows, C_HEADS * QK_NOPE), BF16),
                   jax.ShapeDtypeStruct((n_rows, C_HEADS * QK_ROPE), BF16),
                   jax.ShapeDtypeStruct((n_rows, C_HEADS * C_DV), BF16)],
        compiler_params=_params(("parallel",) * len(grid)),
        name="mla_kv_up_sample" if sample else "mla_kv_up_prompt",
    )(*args)


def _attn_kernel(qn_ref, qr_ref, kn_ref, kr_ref, v_ref, *rest, n_seq):
    o_ref = rest[-1]
    lq = qr_ref.shape[0] // n_seq
    lk = kr_ref.shape[0] // n_seq
    lane = lax.broadcasted_iota(jnp.int32, (lq, LANES), 1)
    blocks = [(slice(s * lq, (s + 1) * lq), slice(s * lk, (s + 1) * lk),
               slice(QK_NOPE * hh, QK_NOPE * (hh + 1)), hh)
              for s in range(n_seq) for hh in range(2)]
    scores = []
    for qrows, krows, cols, hh in blocks:
        qr = qr_ref[qrows, :]
        own = (lane < QK_ROPE) if hh == 0 else (lane >= QK_ROPE)
        qrm = jnp.where(own, qr, jnp.zeros_like(qr))
        s = lax.dot_general(qn_ref[qrows, cols], kn_ref[krows, cols], TRANS_B,
                            preferred_element_type=F32)
        s = s + lax.dot_general(qrm, kr_ref[krows, :], TRANS_B, preferred_element_type=F32)
        scores.append(s * (QK_DIM ** -0.5 * LOG2_E))
    tops = [jnp.max(s, axis=-1, keepdims=True) for s in scores]
    exps = [jnp.exp2(s - m) for s, m in zip(scores, tops)]
    sums = [jnp.sum(e, axis=-1, keepdims=True) for e in exps]
    probs = [(e / t).astype(BF16) for e, t in zip(exps, sums)]
    for p, (qrows, krows, cols, _) in zip(probs, blocks):
        o_ref[qrows, cols] = jnp.dot(p, v_ref[krows, cols], preferred_element_type=F32).astype(BF16)


def _attention(qn, qr, kn, kr, v, o_prev=None):
    sample = o_prev is not None
    if sample:
        q_row0, tq, tk, n_seq = N_PROMPT, SEQ, PAST_LEN + DEC_SEQ, 1
        grid = (DEC_BATCH, C_HEADS // 2, DEC_SEQ // tq)
    else:
        n_seq = 4
        q_row0, tq, tk = 0, n_seq * SEQ, n_seq * SEQ
        grid = (BATCH // n_seq, C_HEADS // 2, 1)
    nq = grid[2]
    qspec = lambda width: pl.BlockSpec((tq, width), lambda b, p, i: (q_row0 // tq + b * nq + i, p))
    kspec = lambda width: pl.BlockSpec((tk, width), lambda b, p, i: (b, p))
    in_specs = [qspec(2 * QK_NOPE), qspec(LANES), kspec(2 * QK_NOPE), kspec(LANES), kspec(2 * C_DV)]
    args = [qn, qr, kn, kr, v]
    if sample:
        in_specs.append(pl.BlockSpec(memory_space=pl.ANY))
        args.append(o_prev)
    return pl.pallas_call(
        functools.partial(_attn_kernel, n_seq=n_seq),
        grid=grid,
        in_specs=in_specs,
        out_specs=qspec(2 * C_DV),
        out_shape=jax.ShapeDtypeStruct((N_TOK, C_HEADS * C_DV), BF16),
        input_output_aliases={5: 0} if sample else {},
        compiler_params=_params(("parallel", "parallel", "parallel")),
        name="mla_attention_sample" if sample else "mla_attention_prompt",
    )(*args)


def _rope_tables():
    rows = DEC_SEQ // GRID_W
    row = jnp.repeat(jnp.arange(rows, dtype=F32), GRID_W)
    col = jnp.tile(jnp.arange(GRID_W, dtype=F32), rows)
    inv_freq = ROPE_BASE ** (-jnp.arange(0, ROPE_AXIS, 2, dtype=F32) / ROPE_AXIS)
    ar = row[:, None] * inv_freq
    ac = col[:, None] * inv_freq
    cos = jnp.concatenate([jnp.cos(ar), jnp.cos(ar), jnp.cos(ac), jnp.cos(ac)], axis=-1)
    sin = jnp.concatenate([-jnp.sin(ar), jnp.sin(ar), -jnp.sin(ac), jnp.sin(ac)], axis=-1)
    return jnp.tile(cos, (1, 2)), jnp.tile(sin, (1, 2))


def _layer_weights(l, gla_w_gate_up, mla_w_q_up, mla_w_kv_up, mla_q_norm, mla_k_norm):
    ext =jnp.zeros((2, LANES, B_HEADS * B_DK), F32)
    ext = ext.at[0, :GLA_RANK].set(gla_w_gate_up[l, 0])
    ext = ext.at[1, GLA_RANK:2 * GLA_RANK].set(gla_w_gate_up[l, 1])
    wq = mla_w_q_up[l].reshape(Q_LORA, C_HEADS, QK_DIM)
    wq = jnp.concatenate([wq[:, :, :QK_NOPE].reshape(Q_LORA, -1),
                          wq[:, :, QK_NOPE:].reshape(Q_LORA, -1)], axis=1).astype(BF16)
    wkv = mla_w_kv_up[l].reshape(KV_LORA, C_HEADS, QK_NOPE + C_DV)
    wkv = jnp.concatenate([wkv[:, :, :QK_NOPE].reshape(KV_LORA, -1),
                           wkv[:, :, QK_NOPE:].reshape(KV_LORA, -1)], axis=1).astype(BF16)
    split = lambda g: (g[:QK_NOPE].reshape(1, QK_NOPE), jnp.tile(g[QK_NOPE:], 2).reshape(1, LANES))
    return ext.astype(BF16), wq, wkv, split(mla_q_norm[l]), split(mla_k_norm[l])


def kernel(x_prompt, x_sample, cache_mla, state_hgrn, state_gla, c, c_ctx, norm1, w_mod, b_mod, w_in,
           hgrn_lb_logits, hgrn_out_norm, gla_w_gate_up, gla_b_gate, gla_out_norm, mla_q_a_norm,
           mla_w_q_up, mla_kv_a_norm, mla_w_kv_up, mla_q_norm, mla_k_norm, w_branch_a, w_branch_b,
           w_branch_c, w_out, norm2, w_mlp_in, w_mlp_out):
    ctx_rows, lat_rows = (0, N_PROMPT), (N_PROMPT, N_SAMPLE)
    x = (x_prompt.reshape(N_PROMPT, D_MODEL), x_sample.reshape(N_SAMPLE, D_MODEL))
    cond =jnp.concatenate([c_ctx[None], c, jnp.zeros((MOD_ROWS - 1 - DEC_BATCH, D_MODEL), F32)])
    mod = _modulation(cond, w_mod, b_mod).reshape(DEPTH * MOD_ROWS, 1, N_MOD * D_MODEL)

    cos_s, sin_s = _rope_tables()
    cos_q = jnp.concatenate([jnp.ones((N_PROMPT, LANES), F32), jnp.tile(cos_s, (DEC_BATCH, 1))])
    sin_q = jnp.concatenate([jnp.zeros((N_PROMPT, LANES), F32), jnp.tile(sin_s, (DEC_BATCH, 1))])

    wb_a, wb_b, wb_c = w_branch_a.astype(BF16), w_branch_b.astype(BF16), w_branch_c.astype(BF16)
    w_in_t = jnp.swapaxes(w_in, 1, 2)
    narrow0, gate0 = Z_MAIN, Z_MAIN + 2 * GLA_RANK + Q_LORA + KV_LORA + QK_ROPE
    caches, hgrn_states, gla_states = [], [], []
    for l in range(DEPTH):
        wlr, wq, wkv, (qgn, qgr), (kgn, kgr) = _layer_weights(
            l, gla_w_gate_up, mla_w_q_up, mla_w_kv_up, mla_q_norm, mla_k_norm)
        ws = _cast_rows(w_in_t, l, narrow0, gate0 - narrow0, (gate0 - narrow0) // 3, "cast_narrow")
        wg = _cast_rows(w_in_t, l, gate0, 3 * D_MODEL, 512, "cast_gates")

        if l == 0:
            h = _norm_mod(x[0], norm1[l], mod, l, 0)
            h = _norm_mod(x[1], norm1[l], mod, l, 0, row0=N_PROMPT, o_prev=h)
        else:
            h = _norm_mod(x, norm1[l], mod, l, 0)
        z = _matmul(h, w_in_t, F32, l, Z_MAIN, transposed=True, name="in_proj")
        zs = _in_small(h, ws)

        o_a, sa = _hgrn(z, hgrn_lb_logits, hgrn_out_norm[l], None, l)
        (o_a,) = _hgrn(z, hgrn_lb_logits, hgrn_out_norm[l], state_hgrn, l, o_prev=o_a)
        o_b, sb = _gla(z, zs, wlr, gla_b_gate[l], gla_out_norm[l], None, l)
        (o_b,) = _gla(z, zs, wlr, gla_b_gate[l], gla_out_norm[l], state_gla, l, o_prev=o_b)
        hgrn_states.append(sa)
        gla_states.append(sb)

        qlat, ckvk = _mla_prep(zs, mla_q_a_norm[l], mla_kv_a_norm[l])
        caches.append(ckvk[:N_PROMPT].reshape(BATCH, SEQ, KV_LORA + QK_ROPE))
        qn, qr = _qup(qlat, wq, qgn, qgr, cos_q, sin_q)
        kn_p, kr_p, v_p = _kvup(ckvk, wkv, kgn, kgr)
        kn_s, kr_s, v_s = _kvup(ckvk, wkv, kgn, kgr, cache_mla, l, cos_s, sin_s)
        o_c = _attention(qn, qr, kn_p, kr_p, v_p)
        o_c = _attention(qn, qr, kn_s, kr_s, v_s, o_prev=o_c)

        merged = _merge(h, wg, o_a, wb_a, o_b, wb_b, o_c, wb_c, l)
        if l == 0:
            xo = _matmul_residual(merged, w_out, x[0], mod, l, 2, 0, "out_proj", rows=ctx_rows)
            x = _matmul_residual(merged, w_out, x[1], mod, l, 2, 0, "out_proj", rows=lat_rows,
                                 res_row0=0, o_prev=xo)
        else:
            x = _matmul_residual(merged, w_out, x, mod, l, 2, 0, "out_proj")

        h2 = _norm_mod(x, norm2[l], mod, l, 3)
        u = _matmul(h2, w_mlp_in, BF16, l, D_FF, relu2=True, name="mlp_in")
        n_kb = D_FF // D_MODEL
        for kb in range(n_kb):
            if l == DEPTH - 1 and kb == n_kb - 1:
                x = tuple(_matmul_residual(u, w_mlp_out, x, mod, l, 5, kb, "mlp_out", rows=rows,
                                           out_rows=rows[1], out_row0=0)
                          for rows in (ctx_rows, lat_rows))
            else:
                x = _matmul_residual(u, w_mlp_out, x, mod, l, 5, kb, "mlp_out")

    y_prompt = x[0].reshape(BATCH, SEQ, D_MODEL)
    y_sample = x[1].reshape(DEC_BATCH, DEC_SEQ, D_MODEL)
    return (y_prompt, y_sample, jnp.stack(caches, axis=1), jnp.stack(hgrn_states, axis=1),
            jnp.stack(gla_states, axis=1))
```

```python
import functools

import numpy as np
import jax
import jax.numpy as jnp
from jax import lax
from jax.experimental import pallas as pl
from jax.experimental.pallas import tpu as pltpu

F32 = jnp.float32
BF16 = jnp.bfloat16

D_MODEL = 4096
BATCH = 32
SEQ = 256
DEPTH = 2
DEC_BATCH = 2
DEC_SEQ = 1024
PAST_LEN = 512
GRID_W = 64
A_HEADS = 8
A_DK = 128
A_DV = 128
B_HEADS = 4
B_DK = 128
B_DV = 256
GLA_RANK = 16
GLA_GATE_NORM = 16.0
C_HEADS = 16
Q_LORA = 1024
KV_LORA = 512
QK_NOPE = 128
QK_ROPE = 64
C_DV = 128
QK_DIM = QK_NOPE + QK_ROPE
ROPE_AXIS = QK_ROPE // 2
ROPE_BASE = 10000.0
N_MOD = 6
D_FF = 4 * D_MODEL
EPS = 1e-6

N_PROMPT = BATCH * SEQ
N_SAMPLE = DEC_BATCH * DEC_SEQ
N_TOK = N_PROMPT + N_SAMPLE
MOD_ROWS = 8

Z_MAIN = 8192
ZS_CQ = 0
ZS_CKV = ZS_CQ + Q_LORA
ZS_KR = ZS_CKV + KV_LORA
ZS_LR = ZS_KR + 128
ZS_COLS = ZS_LR + 128
ROW_ALIGN = 32

LANES = 128
MXU_COLS = 256
V7X_VMEM_BYTES = 64 * 1024 * 1024
VMEM_BIG = 60 * 1024 * 1024
VMEM_MID = 40 * 1024 * 1024

LOG2_E = 1.4426950408889634
CH = 32
SB = 8


def _params(sem, vmem=VMEM_MID):
    return pltpu.CompilerParams(dimension_semantics=sem, vmem_limit_bytes=vmem)


def _mod_row(i, tm):
    return jnp.where(i < N_PROMPT // tm, 0, 1 + (i - N_PROMPT // tm) // (DEC_SEQ // tm))


def _sigmoid(x):
    return 1.0 / (1.0 + jnp.exp(-x))


def _log1p_of_exp_neg(t):
    return jnp.log(1.0 + jnp.exp(-t))


def _log_sigmoid(x):
    return jnp.minimum(x, 0.0) - _log1p_of_exp_neg(jnp.abs(x))


def _mod_kernel(cond_ref, w_ref, b_ref, o_ref):
    c = cond_ref[...]
    s = (c * _sigmoid(c)).astype(BF16)
    o_ref[...] = jnp.dot(s, w_ref[...].astype(BF16), preferred_element_type=F32) + b_ref[...]


def _modulation(cond, w_mod, b_mod):
    tn = 512
    n = N_MOD * D_MODEL
    return pl.pallas_call(
        _mod_kernel,
        grid=(DEPTH, n // tn),
        in_specs=[
            pl.BlockSpec((MOD_ROWS, D_MODEL), lambda l, j: (0, 0)),
            pl.BlockSpec((None, D_MODEL, tn), lambda l, j: (l, 0, j)),
            pl.BlockSpec((None, 1, tn), lambda l, j: (l, 0, j)),
        ],
        out_specs=pl.BlockSpec((None, MOD_ROWS, tn), lambda l, j: (l, 0, j)),
        out_shape=jax.ShapeDtypeStruct((DEPTH, MOD_ROWS, n), F32),
        compiler_params=_params(("parallel", "parallel")),
        name="modulation",
    )(cond, w_mod, b_mod.reshape(DEPTH, 1, n))


def _norm_mod_kernel(x_ref, g_ref, sh_ref, sc_ref, *rest):
    o_ref = rest[-1]
    x = x_ref[...]
    ms = jnp.mean(x * x, axis=-1, keepdims=True)
    y = x * lax.rsqrt(ms + EPS) * g_ref[...]
    o_ref[...] = (y * (1.0 + sc_ref[...]) + sh_ref[...]).astype(BF16)


def _norm_mod(x, gain, mod, layer, shift_chunk, row0=0, o_prev=None):
    tm = 512
    base = layer * MOD_ROWS
    blk0 = row0 // tm
    mod_spec = lambda chunk: pl.BlockSpec(
        (None, 1, D_MODEL), lambda i: (base + _mod_row(blk0 + i, tm), 0, chunk))
    in_specs = [pl.BlockSpec((tm, D_MODEL), lambda i: (i, 0)),
                pl.BlockSpec((1, D_MODEL), lambda i: (0, 0)),
                mod_spec(shift_chunk), mod_spec(shift_chunk + 1)]
    args = [x, gain.reshape(1, D_MODEL), mod, mod]
    if o_prev is not None:
        in_specs.append(pl.BlockSpec(memory_space=pl.ANY))
        args.append(o_prev)
    return pl.pallas_call(
        _norm_mod_kernel,
        grid=(x.shape[0] // tm,),
        in_specs=in_specs,
        out_specs=pl.BlockSpec((tm, D_MODEL), lambda i: (blk0 + i, 0)),
        out_shape=jax.ShapeDtypeStruct((N_TOK, D_MODEL), BF16),
        input_output_aliases={} if o_prev is None else {4: 0},
        compiler_params=_params(("parallel",)),
        name="norm_mod",
    )(*args)


def _cast_weight(w_ref, wb_ref):
    @pl.when(pl.program_id(1) == 0)
    def _():
        wb_ref[...] = w_ref[...].astype(BF16)
    return wb_ref


TRANS_B = (((1,), (1,)), ((), ()))


def _mm_kernel(x_ref, w_ref, o_ref, wb_ref, *, relu2, transposed):
    w_ref = _cast_weight(w_ref, wb_ref)
    if transposed:
        acc = lax.dot_general(x_ref[...], w_ref[...], TRANS_B, preferred_element_type=F32)
    else:
        acc = jnp.dot(x_ref[...], w_ref[...], preferred_element_type=F32)
    if relu2:
        acc = jnp.square(jnp.maximum(acc, 0.0))
    o_ref[...] = acc.astype(o_ref.dtype)


def _matmul(x, w, out_dtype, layer, n_cols, relu2=False, transposed=False, name="matmul"):
    tm, tn = 512, 1024
    m, k = x.shape
    if transposed:
        w_spec = pl.BlockSpec((None, tn, k), lambda j, i: (layer, j, 0))
    else:
        w_spec = pl.BlockSpec((None, k, tn), lambda j, i: (layer, 0, j))
    return pl.pallas_call(
        functools.partial(_mm_kernel, relu2=relu2, transposed=transposed),
        grid=(n_cols // tn, m // tm),
        in_specs=[pl.BlockSpec((tm, k), lambda j, i: (i, 0)), w_spec],
        out_specs=pl.BlockSpec((tm, tn), lambda j, i: (i, j)),
        out_shape=jax.ShapeDtypeStruct((m, n_cols), out_dtype),
        scratch_shapes=[pltpu.VMEM((tn, k) if transposed else (k, tn), BF16)],
        compiler_params=_params(("parallel", "arbitrary"), VMEM_BIG),
        name=name,
    )(x, w)


def _cast_rows_kernel(w_ref, o_ref):
    o_ref[...] = w_ref[0].astype(BF16)


def _cast_rows(w, layer, row0, n_rows, block_rows, name):
    k = w.shape[2]
    assert row0 % ROW_ALIGN == 0 and block_rows % ROW_ALIGN == 0 and n_rows % block_rows == 0
    start = lambda r: pl.multiple_of(ROW_ALIGN * (row0 // ROW_ALIGN + (block_rows // ROW_ALIGN) * r),
                                     ROW_ALIGN)
    return pl.pallas_call(
        _cast_rows_kernel,
        grid=(n_rows // block_rows,),
        in_specs=[pl.BlockSpec((pl.Element(1), pl.Element(block_rows), pl.Element(k)),
                               lambda r: (layer, start(r), 0))],
        out_specs=pl.BlockSpec((block_rows, k), lambda r: (r, 0)),
        out_shape=jax.ShapeDtypeStruct((n_rows, k), BF16),
        compiler_params=_params(("parallel",)),
        name=name,
    )(w)


def _in_small_kernel(x_ref, w_ref, o_ref):
    x = x_ref[...]
    tm = x.shape[0]
    n_lr, n_kv = 2 * GLA_RANK, KV_LORA + QK_ROPE
    part = lambda r0, rows: lax.dot_general(x, w_ref[r0:r0 + rows, :], TRANS_B,
                                            preferred_element_type=F32)
    o_ref[:, ZS_LR:ZS_LR + n_lr] = part(0, n_lr)
    o_ref[:, ZS_LR + n_lr:] = jnp.zeros((tm, ZS_COLS - ZS_LR - n_lr), F32)
    o_ref[:, ZS_CQ:ZS_CQ + Q_LORA] = part(n_lr, Q_LORA)
    o_ref[:, ZS_CKV:ZS_CKV + n_kv] = part(n_lr + Q_LORA, n_kv)
    o_ref[:, ZS_CKV + n_kv:ZS_LR] = jnp.zeros((tm, ZS_LR - ZS_CKV - n_kv), F32)


def _in_small(h, w_small):
    tm = 512
    return pl.pallas_call(
        _in_small_kernel,
        grid=(N_TOK // tm,),
        in_specs=[pl.BlockSpec((tm, D_MODEL), lambda i: (i, 0)),
                  pl.BlockSpec(w_small.shape, lambda i: (0, 0))],
        out_specs=pl.BlockSpec((tm, ZS_COLS), lambda i: (i, 0)),
        out_shape=jax.ShapeDtypeStruct((N_TOK, ZS_COLS), F32),
        compiler_params=_params(("parallel",), VMEM_BIG),
        name="in_proj_small",
    )(h, w_small)


def _mm_res_kernel(x_ref, w_ref, res_ref, g_ref, *rest):
    o_ref, wb_ref = rest[-2:]
    w_ref = _cast_weight(w_ref, wb_ref)
    x = x_ref[...]
    for c0 in range(0, o_ref.shape[1], MXU_COLS):
        cols = slice(c0, c0 + MXU_COLS)
        part = jnp.dot(x, w_ref[:, cols], preferred_element_type=F32)
        o_ref[:, cols] = res_ref[:, cols] + g_ref[:, cols] * part


def _matmul_residual(a, w, res, mod, layer, gate_chunk, k_block, name, rows=(0, N_TOK),
                     res_row0=None, out_rows=N_TOK, out_row0=None, o_prev=None):
    tm, tn, tk = 512, 1024, D_MODEL
    n = w.shape[2]
    base = layer * MOD_ROWS
    cpc = D_MODEL // tn
    blk0 = rows[0] // tm
    res_blk0 = blk0 if res_row0 is None else res_row0 // tm
    out_blk0 = blk0 if out_row0 is None else out_row0 // tm
    in_specs = [
        pl.BlockSpec((tm, tk), lambda j, i: (blk0 + i, k_block)),
        pl.BlockSpec((None, tk, tn), lambda j, i: (layer, k_block, j)),
        pl.BlockSpec((tm, tn), lambda j, i: (res_blk0 + i, j)),
        pl.BlockSpec((None, 1, tn),
                     lambda j, i: (base + _mod_row(blk0 + i, tm), 0, gate_chunk * cpc + j)),
    ]
    args = [a, w, res, mod]
    if o_prev is not None:
        in_specs.append(pl.BlockSpec(memory_space=pl.ANY))
        args.append(o_prev)
    return pl.pallas_call(
        _mm_res_kernel,
        grid=(n // tn, rows[1] // tm),
        in_specs=in_specs,
        out_specs=pl.BlockSpec((tm, tn), lambda j, i: (out_blk0 + i, j)),
        out_shape=jax.ShapeDtypeStruct((out_rows, n), F32),
        input_output_aliases={} if o_prev is None else {4: 0},
        scratch_shapes=[pltpu.VMEM((tk, tn), BF16)],
        compiler_params=_params(("parallel", "arbitrary"), VMEM_BIG),
        name=name,
    )(*args)


def _merge_kernel(h_ref, wga_ref, wgb_ref, wgc_ref, oa_ref, wa_ref, ob_ref, wb_ref, oc_ref, wc_ref,
                  o_ref):
    h = h_ref[...]
    branches = ((wga_ref, oa_ref[...], wa_ref), (wgb_ref, ob_ref[...], wb_ref),
                (wgc_ref, oc_ref[...], wc_ref))
    for c0 in range(0, o_ref.shape[1], MXU_COLS):
        cols = slice(c0, c0 + MXU_COLS)
        acc = None
        for wg_ref, o_b, w_ref in branches:
            g = lax.dot_general(h, wg_ref[cols, :], TRANS_B, preferred_element_type=F32)
            p = jnp.dot(o_b, w_ref[:, cols], preferred_element_type=F32)
            term = _sigmoid(g) * p
            acc = term if acc is None else acc + term
        o_ref[:, cols] = acc.astype(BF16)


def _merge(h, wg, o_a, w_a, o_b, w_b, o_c, w_c, layer):
    tm, tn = 512, 512
    nj = D_MODEL // tn
    act = lambda width: pl.BlockSpec((tm, width), lambda j, i: (i, 0))
    wcol = lambda rows: pl.BlockSpec((None, rows, tn), lambda j, i: (layer, 0, j))
    gate = lambda b: pl.BlockSpec((tn, D_MODEL), lambda j, i: (b * nj + j, 0))
    return pl.pallas_call(
        _merge_kernel,
        grid=(nj, N_TOK // tm),
        in_specs=[act(D_MODEL), gate(0), gate(1), gate(2),
                  act(o_a.shape[1]), wcol(w_a.shape[1]),
                  act(o_b.shape[1]), wcol(w_b.shape[1]),
                  act(o_c.shape[1]), wcol(w_c.shape[1])],
        out_specs=pl.BlockSpec((tm, tn), lambda j, i: (i, j)),
        out_shape=jax.ShapeDtypeStruct((N_TOK, D_MODEL), BF16),
        compiler_params=_params(("parallel", "parallel"), VMEM_BIG),
        name="merge",
    )(h, wg, wg, wg, o_a, w_a, o_b, w_b, o_c, w_c)


PAD = 8


def _shifted(ref, n_rows, d, reverse):
    return ref[pl.ds(PAD + d if reverse else PAD - d, n_rows), :]


def _seg_cumsum(g, rowc, reverse, c_scr):
    n_rows = g.shape[0]
    x = g
    for sh in (1, 2, 4, 8, 16):
        mask = rowc < CH - sh if reverse else rowc >= sh
        if sh < PAD:
            c_scr[pl.ds(PAD, n_rows), :] = x
            moved = _shifted(c_scr, n_rows, sh, reverse)
        else:
            moved = pltpu.roll(x, n_rows - sh if reverse else sh, 0)
        x = x + jnp.where(mask, moved, 0.0)
    return x


def _scan_dir(q, k, v, g, s0t, reverse, scratch, accumulate, want_state):
    o_scr, k_scr, c_scr, v_scr = scratch
    n_rows, dk = q.shape
    n = n_rows // CH
    rowc = lax.broadcasted_iota(jnp.int32, (n_rows, dk), 0) & (CH - 1)
    cum = _seg_cumsum(g * LOG2_E, rowc, reverse, c_scr)
    c_scr[pl.ds(PAD, n_rows), :] = cum
    k_scr[pl.ds(PAD, n_rows), :] = k

    def rows_of(r, count):
        return jnp.broadcast_to(cum[r:r + 1, :], (count, dk))

    def chunk_row(r):
        return jnp.concatenate([rows_of(c * CH + r, CH) for c in range(n)], axis=0)

    last_row = 0 if reverse else CH - 1
    last_b = chunk_row(last_row)
    qt = (q * jnp.exp2(cum)).astype(BF16)
    kh = (k * jnp.exp2(last_b - cum)).astype(BF16)
    vb = v.astype(BF16)

    rows = rowc & (SB - 1)
    o_diag = jnp.sum(q * k, axis=-1, keepdims=True) * v
    for d in range(1, SB):
        mask = rows < SB - d if reverse else rows >= d
        kd = _shifted(k_scr, n_rows, d, reverse)
        cd = _shifted(c_scr, n_rows, d, reverse)
        vd = _shifted(v_scr, n_rows, d, reverse)
        e = jnp.exp2(jnp.where(mask, cum - cd, -jnp.inf))
        o_diag = o_diag + jnp.sum(q * kd * e, axis=-1, keepdims=True) * vd
    if accumulate:
        o_scr[...] += o_diag
    else:
        o_scr[...] = o_diag

    sub = rowc // SB
    nsb = CH // SB
    if reverse:
        own = [rows_of(c * CH + SB * min(i + 1, nsb - 1), SB) for c in range(n) for i in range(nsb)]
        has_piv = sub < nsb - 1
    else:
        own = [rows_of(c * CH + SB * max(i, 1) - 1, SB) for c in range(n) for i in range(nsb)]
        has_piv = sub >= 1
    qe = q * jnp.exp2(jnp.where(has_piv, cum - jnp.concatenate(own, axis=0), -jnp.inf))
    qs, ks = [], []
    for pr in range(SB, CH, SB):
        if reverse:
            piv, qmask, kmask = chunk_row(pr), sub == pr // SB - 1, rowc >= pr
        else:
            piv, qmask, kmask = chunk_row(pr - 1), sub == pr // SB, rowc < pr
        qs.append(jnp.where(qmask, qe, 0.0).astype(BF16))
        ks.append((k * jnp.exp2(jnp.where(kmask, piv - cum, -jnp.inf))).astype(BF16))
    qcat = jnp.concatenate(qs, axis=-1)
    kcat = jnp.concatenate(ks, axis=-1)

    trans_a = (((0,), (0,)), ((), ()))
    order = list(range(n - 1, -1, -1) if reverse else range(n))
    rows = [slice(c * CH, (c + 1) * CH) for c in order]
    n_upd = n if want_state else n - 1
    uts = [lax.dot_general(vb[sl], kh[sl], trans_a, preferred_element_type=F32) for sl in rows[:n_upd]]
    atts = [lax.dot_general(qcat[sl], kcat[sl], TRANS_B, preferred_element_type=F32).astype(BF16)
            for sl in rows]
    o_off = [jnp.dot(a, vb[sl], preferred_element_type=F32) for a, sl in zip(atts, rows)]
    st = s0t
    states = []
    for idx, c in enumerate(order):
        states.append(st.astype(BF16))
        if idx < n_upd:
            r = c * CH + last_row
            st = st * jnp.exp2(cum[r:r + 1, :]) + uts[idx]
    for sl, s_in, off in zip(rows, states, o_off):
        o_scr[sl, :] += lax.dot_general(qt[sl], s_in, TRANS_B, preferred_element_type=F32) + off
    return st


def _finish_scan(o_scr, gain_ref, zg_ref, o_ref):
    o = o_scr[...]
    ms = jnp.mean(o * o, axis=-1, keepdims=True)
    zg = zg_ref[...]
    o_ref[...] = (o * lax.rsqrt(ms + EPS) * gain_ref[...] * (zg * _sigmoid(zg))).astype(BF16)


def _split_scan_refs(refs, n_in, sample, n_alias):
    pos = n_in
    s0_ref = None
    if sample:
        s0_ref = refs[pos]
        pos += 1
    pos += n_alias
    o_ref = refs[pos]
    st_ref = None if sample else refs[pos + 1]
    scratch = refs[pos + (1 if sample else 2):]
    return refs[:n_in], s0_ref, o_ref, st_ref, scratch


def _prep_scan_scratch(scratch, v):
    _, k_scr, c_scr, v_scr = scratch
    n_rows = v.shape[0]
    for ref in (k_scr, c_scr, v_scr):
        zero = jnp.zeros((PAD, ref.shape[1]), F32)
        ref[pl.ds(0, PAD), :] = zero
        ref[pl.ds(PAD + n_rows, PAD), :] = zero
    v_scr[pl.ds(PAD, n_rows), :] = v


def _scan_scratch(seq, dk, dv):
    return [pltpu.VMEM((seq, dv), F32), pltpu.VMEM((seq + 2 * PAD, dk), F32),
            pltpu.VMEM((seq + 2 * PAD, dk), F32), pltpu.VMEM((seq + 2 * PAD, dv), F32)]


def _scan_kernel(*refs, head_fn, lane_widths, sample, hps, n_alias):
    ins, s0_ref, o_ref, st_ref, scratch = _split_scan_refs(refs, len(lane_widths), sample, n_alias)
    per_head = len(scratch) // hps
    dv = o_ref.shape[1] // hps

    def lanes(ref, width, hh):
        if width is None:
            return ref
        return ref.at[(slice(None),) * (len(ref.shape) - 1) + (pl.ds(width * hh, width),)]

    for hh in range(hps):
        head_fn([lanes(r, w, hh) for r, w in zip(ins, lane_widths)],
                None if s0_ref is None else s0_ref.at[:, hh],
                lanes(o_ref, dv, hh),
                None if st_ref is None else st_ref.at[:, hh],
                scratch[per_head * hh:per_head * (hh + 1)])


def _hgrn_head(ins, s0_ref, o_ref, st_ref, scratch, *, layer):
    lbl_ref, q_ref, ff_ref, fb_ref, i_ref, zg_ref, gain_ref = ins
    has_state, want_state = s0_ref is not None, st_ref is not None

    logits = [lbl_ref[l] for l in range(DEPTH)]
    mx = functools.reduce(jnp.maximum, logits)
    ex = [jnp.exp(x - mx) for x in logits]
    tot = functools.reduce(lambda a, b: a + b, ex)
    probs = [e / tot for e in ex]
    cum_first = probs[0]
    cum_l = functools.reduce(lambda a, b: a + b, probs[:layer + 1])
    lb = cum_l - cum_first

    q = q_ref[...]
    v = i_ref[...]
    _prep_scan_scratch(scratch, v)
    for d, (z_ref, reverse) in enumerate(((ff_ref, False), (fb_ref, True))):
        lbd = lb[d:d + 1, :]
        zf = z_ref[...]
        a = jnp.log(lbd)
        b = jnp.log1p(-lbd) + _log_sigmoid(zf)
        delta = a - b
        log_f = jnp.where(delta != delta, a + b,
                          jnp.maximum(a, b) + _log1p_of_exp_neg(jnp.abs(delta)))
        one_minus_f = (1.0 - lbd) * _sigmoid(-zf)
        if has_state:
            s0t = s0_ref[d].T
        else:
            s0t = jnp.zeros((A_DV, A_DK), F32)
        st = _scan_dir(q, one_minus_f, v, log_f, s0t, reverse, scratch, d == 1, want_state)
        if want_state:
            st_ref[d] = st.T
    _finish_scan(scratch[0], gain_ref, zg_ref, o_ref)


def _scan_call(head_fn, name, inputs, heads, dk, dv, state, layer, o_prev):
    sample = o_prev is not None
    seq, nb = (DEC_SEQ, DEC_BATCH) if sample else (SEQ, BATCH)
    hps = 1 if sample else 4
    row0 = N_PROMPT // seq if sample else 0
    in_specs, args, lane_widths = [], [], []
    for arr, kind, *par in inputs:
        args.append(arr)
        if kind == "cols":
            col0, width = par
            blk = col0 // (width * hps)
            in_specs.append(pl.BlockSpec((seq, width * hps), lambda b, h, blk=blk: (row0 + b, blk + h)))
            lane_widths.append(width)
        elif kind == "shared_cols":
            col0, width = par
            in_specs.append(pl.BlockSpec((seq, width), lambda b, h, blk=col0 // width: (row0 + b, blk)))
            lane_widths.append(None)
        elif kind == "heads_last":
            (width,) = par
            lead = arr.shape[:-1]
            in_specs.append(pl.BlockSpec(lead + (width * hps,),
                                         lambda b, h, n=len(lead): (0,) * n + (h,)))
            lane_widths.append(width)
        else:
            in_specs.append(pl.BlockSpec(arr.shape, lambda b, h, n=arr.ndim: (0,) * n))
            lane_widths.append(None)
    state_spec = pl.BlockSpec((None, None, 2, hps, dk, dv), lambda b, h: (b, layer, 0, h, 0, 0))
    aliases = {}
    if sample:
        in_specs += [state_spec, pl.BlockSpec(memory_space=pl.ANY)]
        aliases = {len(args) + 1: 0}
        args += [state, o_prev]
    elif state is not None:
        in_specs.append(pl.BlockSpec(memory_space=pl.ANY))
        aliases = {len(args): 1}
        args.append(state)
    out_specs = [pl.BlockSpec((seq, dv * hps), lambda b, h: (row0 + b, h))]
    out_shape = [jax.ShapeDtypeStruct((N_TOK, heads * dv), BF16)]
    if not sample:
        out_specs.append(state_spec)
        out_shape.append(jax.ShapeDtypeStruct((nb, DEPTH, 2, heads, dk, dv), F32))
    return pl.pallas_call(
        functools.partial(_scan_kernel, head_fn=head_fn, lane_widths=tuple(lane_widths),
                          sample=sample, hps=hps, n_alias=len(aliases)),
        grid=(nb, heads // hps),
        in_specs=in_specs,
        out_specs=out_specs,
        out_shape=out_shape,
        input_output_aliases=aliases,
        scratch_shapes=_scan_scratch(seq, dk, dv) * hps,
        compiler_params=_params(("parallel", "parallel")),
        name=name + ("_sample" if sample else "_prompt"),
    )(*args)


def _hgrn(z, lb_logits, out_gain, state, layer, o_prev=None):
    width = A_HEADS * A_DK
    inputs = [(lb_logits, "heads_last", A_DK)]
    inputs += [(z, "cols", g * width, A_DK) for g in range(5)]
    inputs += [(out_gain.reshape(1, A_DV), "whole")]
    return _scan_call(functools.partial(_hgrn_head, layer=layer), "hgrn", inputs,
                      A_HEADS, A_DK, A_DV, state, layer, o_prev)


def _gla_head(ins, s0_ref, o_ref, st_ref, scratch):
    q_ref, k_ref, v_ref, zr_ref, lr_ref, wg_ref, bg_ref, gain_ref = ins
    has_state, want_state = s0_ref is not None, st_ref is not None

    q = q_ref[...] * (B_DK ** -0.5)
    k = k_ref[...]
    v = v_ref[...]
    _prep_scan_scratch(scratch, v)
    lr = lr_ref[...].astype(BF16)
    for d, reverse in enumerate((False, True)):
        logit = jnp.dot(lr, wg_ref[d], preferred_element_type=F32) + bg_ref[d]
        g = _log_sigmoid(logit) / GLA_GATE_NORM
        if has_state:
            s0t = s0_ref[d].T
        else:
            s0t = jnp.zeros((B_DV, B_DK), F32)
        st = _scan_dir(q, k, v, g, s0t, reverse, scratch, d == 1, want_state)
        if want_state:
            st_ref[d] = st.T
    _finish_scan(scratch[0], gain_ref, zr_ref, o_ref)


def _gla(z, zs, w_gate_ext, b_gate, out_gain, state, layer, o_prev=None):
    inputs = [(z, "cols", 5120, B_DK), (z, "cols", 5632, B_DK), (z, "cols", 6144, B_DV),
              (z, "cols", 7168, B_DV), (zs, "shared_cols", ZS_LR, LANES),
              (w_gate_ext, "heads_last", B_DK), (b_gate.reshape(2, 1, B_HEADS * B_DK), "heads_last", B_DK),
              (out_gain.reshape(1, B_DV), "whole")]
    return _scan_call(_gla_head, "gla", inputs, B_HEADS, B_DK, B_DV, state, layer, o_prev)


def _mla_prep_kernel(cq_ref, ckv_ref, kr_ref, gq_ref, gkv_ref, qlat_ref, ckvk_ref):
    cq = cq_ref[...]
    qlat_ref[...] = (cq * lax.rsqrt(jnp.mean(cq * cq, axis=-1, keepdims=True) + EPS)
                     * gq_ref[...]).astype(BF16)
    ckv = ckv_ref[...]
    ckvk_ref[:, :KV_LORA] = (ckv * lax.rsqrt(jnp.mean(ckv * ckv, axis=-1, keepdims=True) + EPS)
                             * gkv_ref[...])
    ckvk_ref[:, KV_LORA:] = kr_ref[:, :QK_ROPE]


def _mla_prep(z, q_gain, kv_gain):
    tm = 512
    return pl.pallas_call(
        _mla_prep_kernel,
        grid=(N_TOK // tm,),
        in_specs=[
            pl.BlockSpec((tm, Q_LORA), lambda i: (i, ZS_CQ // Q_LORA)),
            pl.BlockSpec((tm, KV_LORA), lambda i: (i, ZS_CKV // KV_LORA)),
            pl.BlockSpec((tm, LANES), lambda i: (i, ZS_KR // LANES)),
            pl.BlockSpec((1, Q_LORA), lambda i: (0, 0)),
            pl.BlockSpec((1, KV_LORA), lambda i: (0, 0)),
        ],
        out_specs=[pl.BlockSpec((tm, Q_LORA), lambda i: (i, 0)),
                   pl.BlockSpec((tm, KV_LORA + QK_ROPE), lambda i: (i, 0))],
        out_shape=[jax.ShapeDtypeStruct((N_TOK, Q_LORA), BF16),
                   jax.ShapeDtypeStruct((N_TOK, KV_LORA + QK_ROPE), F32)],
        compiler_params=_params(("parallel",)),
        name="mla_prep",
    )(z, z, z, q_gain.reshape(1, Q_LORA), kv_gain.reshape(1, KV_LORA))


def _swap_halves(t, lane):
    width = t.shape[-1]
    return jnp.where((lane & 31) < 16, pltpu.roll(t, width - 16, 1), pltpu.roll(t, 16, 1))


def _qup_kernel(x_ref, w_ref, gn_ref, gr_ref, c_ref, s_ref, qn_ref, qr_ref):
    x = x_ref[...]
    nope_w = C_HEADS * QK_NOPE
    tm = x.shape[0]
    lane = lax.broadcasted_iota(jnp.int32, (tm, LANES), 1)
    low = lane < QK_ROPE
    gn = gn_ref[...]
    gr = gr_ref[...]
    cos = c_ref[...]
    sin = s_ref[...]
    strip = lambda c0: jnp.dot(x, w_ref[:, c0:c0 + MXU_COLS], preferred_element_type=F32)
    for p in range(C_HEADS // 2):
        acc = strip(MXU_COLS * p)
        if p % 2 == 0:
            rope2 = strip(nope_w + LANES * p)
        r2 = rope2[:, LANES * (p % 2):LANES * (p % 2 + 1)]
        sq = r2 * r2
        ss_rope = (jnp.sum(jnp.where(low, sq, 0.0), axis=-1, keepdims=True),
                   jnp.sum(jnp.where(low, 0.0, sq), axis=-1, keepdims=True))
        inv = []
        for hh in range(2):
            h = 2 * p + hh
            nh = acc[:, QK_NOPE * hh: QK_NOPE * (hh + 1)]
            ssn = jnp.sum(nh * nh, axis=-1, keepdims=True)
            r = lax.rsqrt((ssn + ss_rope[hh]) / QK_DIM + EPS)
            qn_ref[:, QK_NOPE * h: QK_NOPE * (h + 1)] = (nh * r * gn).astype(BF16)
            inv.append(r)
        t = r2 * jnp.where(low, inv[0], inv[1]) * gr
        qr_ref[:, LANES * p: LANES * (p + 1)] = (t * cos + _swap_halves(t, lane) * sin).astype(BF16)


def _qup(qlat, w, gn, gr, cos, sin):
    tm = 512
    n_rows = qlat.shape[0]
    row = lambda width: pl.BlockSpec((tm, width), lambda i: (i, 0))
    full = lambda a: pl.BlockSpec(a.shape, lambda i: (0, 0))
    return pl.pallas_call(
        _qup_kernel,
        grid=(n_rows // tm,),
        in_specs=[row(Q_LORA), full(w), full(gn), full(gr), row(LANES), row(LANES)],
        out_specs=[row(C_HEADS * QK_NOPE), row(C_HEADS * QK_ROPE)],
        out_shape=[jax.ShapeDtypeStruct((n_rows, C_HEADS * QK_NOPE), BF16),
                   jax.ShapeDtypeStruct((n_rows, C_HEADS * QK_ROPE), BF16)],
        compiler_params=_params(("parallel",)),
        name="mla_q_up",
    )(qlat, w, gn, gr, cos, sin)


def _kvup_kernel(*refs, sample):
    if sample:
        cache_ref, x_ref, w_ref, gn_ref, gr_ref, c_ref, s_ref, kn_ref, kr_ref, v_ref = refs
        from_cache = pl.program_id(1) == 0
        x = jnp.where(from_cache, cache_ref[...], x_ref[...])
    else:
        x_ref, w_ref, gn_ref, gr_ref, kn_ref, kr_ref, v_ref = refs
        x = x_ref[...]
    ckv = x[:, :KV_LORA].astype(BF16)
    nope_w = C_HEADS * QK_NOPE
    tm = x.shape[0]
    lane = lax.broadcasted_iota(jnp.int32, (tm, LANES), 1)
    low = lane < QK_ROPE
    kr = x[:, KV_LORA:]
    ss_rope = jnp.sum(kr * kr, axis=-1, keepdims=True)
    rot = jnp.concatenate([kr, kr], axis=-1) * gr_ref[...]
    if sample:
        turned = rot * c_ref[...] + _swap_halves(rot, lane) * s_ref[...]
        rot = jnp.where(from_cache, rot, turned)
    gn = gn_ref[...]
    for p in range(C_HEADS // 2):
        pair = slice(2 * QK_NOPE * p, 2 * QK_NOPE * (p + 1))
        acc = jnp.dot(ckv, w_ref[:, pair], preferred_element_type=F32)
        v_ref[:, pair] = jnp.dot(ckv, w_ref[:, nope_w + pair.start:nope_w + pair.stop],
                                 preferred_element_type=F32).astype(BF16)
        inv = []
        for hh in range(2):
            h = 2 * p + hh
            nh = acc[:, QK_NOPE * hh: QK_NOPE * (hh + 1)]
            ssn = jnp.sum(nh * nh, axis=-1, keepdims=True)
            r = lax.rsqrt((ssn + ss_rope) / QK_DIM + EPS)
            kn_ref[:, QK_NOPE * h: QK_NOPE * (h + 1)] = (nh * r * gn).astype(BF16)
            inv.append(r)
        kr_ref[:, LANES * p: LANES * (p + 1)] = (rot * jnp.where(low, inv[0], inv[1])).astype(BF16)


def _kvup(ckvk, w, gn, gr, cache=None, layer=None, cos=None, sin=None):
    tm = PAST_LEN
    sample = cache is not None
    width = KV_LORA + QK_ROPE
    if sample:
        per_seq = 1 + DEC_SEQ // tm
        grid = (DEC_BATCH, per_seq)
        row_blk = lambda b, t: b * per_seq + t
        new_blk = lambda b, t: N_PROMPT // tm + b * (per_seq - 1) + jnp.maximum(t - 1, 0)
        full = lambda a: pl.BlockSpec(a.shape, lambda b, t: (0, 0))
        table = pl.BlockSpec((tm, LANES), lambda b, t: (jnp.maximum(t - 1, 0), 0))
        in_specs = [pl.BlockSpec((None, None, tm, width), lambda b, t: (b, layer, 0, 0)),
                    pl.BlockSpec((tm, width), lambda b, t: (new_blk(b, t), 0)),
                    full(w), full(gn), full(gr), table, table]
        args = (cache, ckvk, w, gn, gr, cos, sin)
        row = lambda wd: pl.BlockSpec((tm, wd), lambda b, t: (row_blk(b, t), 0))
        n_rows = DEC_BATCH * per_seq * tm
    else:
        grid = (N_PROMPT // tm,)
        full = lambda a: pl.BlockSpec(a.shape, lambda i: (0, 0))
        in_specs = [pl.BlockSpec((tm, width), lambda i: (i, 0)), full(w), full(gn), full(gr)]
        args = (ckvk, w, gn, gr)
        row = lambda wd: pl.BlockSpec((tm, wd), lambda i: (i, 0))
        n_rows = N_PROMPT
    return pl.pallas_call(
        functools.partial(_kvup_kernel, sample=sample),
        grid=grid,
        in_specs=in_specs,
        out_specs=[row(C_HEADS * QK_NOPE), row(C_HEADS * QK_ROPE), row(C_HEADS * C_DV)],
        out_shape=[jax.ShapeDtypeStruct((n_rows, C_HEADS * QK_NOPE), BF16),
                   jax.ShapeDtypeStruct((n_rows, C_HEADS * QK_ROPE), BF16),
                   jax.ShapeDtypeStruct((n_rows, C_HEADS * C_DV), BF16)],
        compiler_params=_params(("parallel",) * len(grid)),
        name="mla_kv_up_sample" if sample else "mla_kv_up_prompt",
    )(*args)


def _attn_kernel(qn_ref, qr_ref, kn_ref, kr_ref, v_ref, *rest, n_seq):
    o_ref = rest[-1]
    lq = qr_ref.shape[0] // n_seq
    lk = kr_ref.shape[0] // n_seq
    lane = lax.broadcasted_iota(jnp.int32, (lq, LANES), 1)
    blocks = [(slice(s * lq, (s + 1) * lq), slice(s * lk, (s + 1) * lk),
               slice(QK_NOPE * hh, QK_NOPE * (hh + 1)), hh)
              for s in range(n_seq) for hh in range(2)]
    scores = []
    for qrows, krows, cols, hh in blocks:
        qr = qr_ref[qrows, :]
        own = (lane < QK_ROPE) if hh == 0 else (lane >= QK_ROPE)
        qrm = jnp.where(own, qr, jnp.zeros_like(qr))
        s = lax.dot_general(qn_ref[qrows, cols], kn_ref[krows, cols], TRANS_B,
                            preferred_element_type=F32)
        s = s + lax.dot_general(qrm, kr_ref[krows, :], TRANS_B, preferred_element_type=F32)
        scores.append(s * (QK_DIM ** -0.5 * LOG2_E))
    tops = [jnp.max(s, axis=-1, keepdims=True) for s in scores]
    exps = [jnp.exp2(s - m) for s, m in zip(scores, tops)]
    sums = [jnp.sum(e, axis=-1, keepdims=True) for e in exps]
    probs = [(e / t).astype(BF16) for e, t in zip(exps, sums)]
    for p, (qrows, krows, cols, _) in zip(probs, blocks):
        o_ref[qrows, cols] = jnp.dot(p, v_ref[krows, cols], preferred_element_type=F32).astype(BF16)


def _attention(qn, qr, kn, kr, v, o_prev=None):
    sample = o_prev is not None
    if sample:
        q_row0, tq, tk, n_seq = N_PROMPT, SEQ, PAST_LEN + DEC_SEQ, 1
        grid = (DEC_BATCH, C_HEADS // 2, DEC_SEQ // tq)
    else:
        n_seq = 4
        q_row0, tq, tk = 0, n_seq * SEQ, n_seq * SEQ
        grid = (BATCH // n_seq, C_HEADS // 2, 1)
    nq = grid[2]
    qspec = lambda width: pl.BlockSpec((tq, width), lambda b, p, i: (q_row0 // tq + b * nq + i, p))
    kspec = lambda width: pl.BlockSpec((tk, width), lambda b, p, i: (b, p))
    in_specs = [qspec(2 * QK_NOPE), qspec(LANES), kspec(2 * QK_NOPE), kspec(LANES), kspec(2 * C_DV)]
    args = [qn, qr, kn, kr, v]
    if sample:
        in_specs.append(pl.BlockSpec(memory_space=pl.ANY))
        args.append(o_prev)
    return pl.pallas_call(
        functools.partial(_attn_kernel, n_seq=n_seq),
        grid=grid,
        in_specs=in_specs,
        out_specs=qspec(2 * C_DV),
        out_shape=jax.ShapeDtypeStruct((N_TOK, C_HEADS * C_DV), BF16),
        input_output_aliases={5: 0} if sample else {},
        compiler_params=_params(("parallel", "parallel", "parallel")),
        name="mla_attention_sample" if sample else "mla_attention_prompt",
    )(*args)


def _rope_tables():
    rows = DEC_SEQ // GRID_W
    row = jnp.repeat(jnp.arange(rows, dtype=F32), GRID_W)
    col = jnp.tile(jnp.arange(GRID_W, dtype=F32), rows)
    inv_freq = ROPE_BASE ** (-jnp.arange(0, ROPE_AXIS, 2, dtype=F32) / ROPE_AXIS)
    ar = row[:, None] * inv_freq
    ac = col[:, None] * inv_freq
    cos = jnp.concatenate([jnp.cos(ar), jnp.cos(ar), jnp.cos(ac), jnp.cos(ac)], axis=-1)
    sin = jnp.concatenate([-jnp.sin(ar), jnp.sin(ar), -jnp.sin(ac), jnp.sin(ac)], axis=-1)
    return jnp.tile(cos, (1, 2)), jnp.tile(sin, (1, 2))


def _layer_weights(l, gla_w_gate_up, mla_w_q_up, mla_w_kv_up, mla_q_norm, mla_k_norm):
    ext =jnp.zeros((2, LANES, B_HEADS * B_DK), F32)
    ext = ext.at[0, :GLA_RANK].set(gla_w_gate_up[l, 0])
    ext = ext.at[1, GLA_RANK:2 * GLA_RANK].set(gla_w_gate_up[l, 1])
    wq = mla_w_q_up[l].reshape(Q_LORA, C_HEADS, QK_DIM)
    wq = jnp.concatenate([wq[:, :, :QK_NOPE].reshape(Q_LORA, -1),
                          wq[:, :, QK_NOPE:].reshape(Q_LORA, -1)], axis=1).astype(BF16)
    wkv = mla_w_kv_up[l].reshape(KV_LORA, C_HEADS, QK_NOPE + C_DV)
    wkv = jnp.concatenate([wkv[:, :, :QK_NOPE].reshape(KV_LORA, -1),
                           wkv[:, :, QK_NOPE:].reshape(KV_LORA, -1)], axis=1).astype(BF16)
    split = lambda g: (g[:QK_NOPE].reshape(1, QK_NOPE), jnp.tile(g[QK_NOPE:], 2).reshape(1, LANES))
    return ext.astype(BF16), wq, wkv, split(mla_q_norm[l]), split(mla_k_norm[l])


def kernel(x_prompt, x_sample, cache_mla, state_hgrn, state_gla, c, c_ctx, norm1, w_mod, b_mod, w_in,
           hgrn_lb_logits, hgrn_out_norm, gla_w_gate_up, gla_b_gate, gla_out_norm, mla_q_a_norm,
           mla_w_q_up, mla_kv_a_norm, mla_w_kv_up, mla_q_norm, mla_k_norm, w_branch_a, w_branch_b,
           w_branch_c, w_out, norm2, w_mlp_in, w_mlp_out):
    ctx_rows, lat_rows = (0, N_PROMPT), (N_PROMPT, N_SAMPLE)
    x = (x_prompt.reshape(N_PROMPT, D_MODEL), x_sample.reshape(N_SAMPLE, D_MODEL))
    cond =jnp.concatenate([c_ctx[None], c, jnp.zeros((MOD_ROWS - 1 - DEC_BATCH, D_MODEL), F32)])
    mod = _modulation(cond, w_mod, b_mod).reshape(DEPTH * MOD_ROWS, 1, N_MOD * D_MODEL)

    cos_s, sin_s = _rope_tables()
    cos_q = jnp.concatenate([jnp.ones((N_PROMPT, LANES), F32), jnp.tile(cos_s, (DEC_BATCH, 1))])
    sin_q = jnp.concatenate([jnp.zeros((N_PROMPT, LANES), F32), jnp.tile(sin_s, (DEC_BATCH, 1))])

    wb_a, wb_b, wb_c = w_branch_a.astype(BF16), w_branch_b.astype(BF16), w_branch_c.astype(BF16)
    w_in_t = jnp.swapaxes(w_in, 1, 2)
    narrow0, gate0 = Z_MAIN, Z_MAIN + 2 * GLA_RANK + Q_LORA + KV_LORA + QK_ROPE
    caches = []
    new_hgrn = new_gla = None
    for l in range(DEPTH):
        wlr, wq, wkv, (qgn, qgr), (kgn, kgr) = _layer_weights(
            l, gla_w_gate_up, mla_w_q_up, mla_w_kv_up, mla_q_norm, mla_k_norm)
        ws = _cast_rows(w_in_t, l, narrow0, gate0 - narrow0, (gate0 - narrow0) // 3, "cast_narrow")
        wg = _cast_rows(w_in_t, l, gate0, 3 * D_MODEL, 512, "cast_gates")

        if l == 0:
            h = _norm_mod(x[0], norm1[l], mod, l, 0)
            h = _norm_mod(x[1], norm1[l], mod, l, 0, row0=N_PROMPT, o_prev=h)
        else:
            h = _norm_mod(x, norm1[l], mod, l, 0)
        z = _matmul(h, w_in_t, F32, l, Z_MAIN, transposed=True, name="in_proj")
        zs = _in_small(h, ws)

        o_a, new_hgrn = _hgrn(z, hgrn_lb_logits, hgrn_out_norm[l], new_hgrn, l)
        (o_a,) = _hgrn(z, hgrn_lb_logits, hgrn_out_norm[l], state_hgrn, l, o_prev=o_a)
        o_b, new_gla = _gla(z, zs, wlr, gla_b_gate[l], gla_out_norm[l], new_gla, l)
        (o_b,) = _gla(z, zs, wlr, gla_b_gate[l], gla_out_norm[l], state_gla, l, o_prev=o_b)

        qlat, ckvk = _mla_prep(zs, mla_q_a_norm[l], mla_kv_a_norm[l])
        caches.append(ckvk[:N_PROMPT].reshape(BATCH, SEQ, KV_LORA + QK_ROPE))
        qn, qr = _qup(qlat, wq, qgn, qgr, cos_q, sin_q)
        kn_p, kr_p, v_p = _kvup(ckvk, wkv, kgn, kgr)
        kn_s, kr_s, v_s = _kvup(ckvk, wkv, kgn, kgr, cache_mla, l, cos_s, sin_s)
        o_c = _attention(qn, qr, kn_p, kr_p, v_p)
        o_c = _attention(qn, qr, kn_s, kr_s, v_s, o_prev=o_c)

        merged = _merge(h, wg, o_a, wb_a, o_b, wb_b, o_c, wb_c, l)
        if l == 0:
            xo = _matmul_residual(merged, w_out, x[0], mod, l, 2, 0, "out_proj", rows=ctx_rows)
            x = _matmul_residual(merged, w_out, x[1], mod, l, 2, 0, "out_proj", rows=lat_rows,
                                 res_row0=0, o_prev=xo)
        else:
            x = _matmul_residual(merged, w_out, x, mod, l, 2, 0, "out_proj")

        h2 = _norm_mod(x, norm2[l], mod, l, 3)
        u = _matmul(h2, w_mlp_in, BF16, l, D_FF, relu2=True, name="mlp_in")
        n_kb = D_FF // D_MODEL
        for kb in range(n_kb):
            if l == DEPTH - 1 and kb == n_kb - 1:
                x = tuple(_matmul_residual(u, w_mlp_out, x, mod, l, 5, kb, "mlp_out", rows=rows,
                                           out_rows=rows[1], out_row0=0)
                          for rows in (ctx_rows, lat_rows))
            else:
                x = _matmul_residual(u, w_mlp_out, x, mod, l, 5, kb, "mlp_out")

    y_prompt = x[0].reshape(BATCH, SEQ, D_MODEL)
    y_sample = x[1].reshape(DEC_BATCH, DEC_SEQ, D_MODEL)
    return (y_prompt, y_sample, jnp.stack(caches, axis=1), new_hgrn, new_gla)
```

```python
import functools

import numpy as np
import jax
import jax.numpy as jnp
from jax import lax
from jax.experimental import pallas as pl
from jax.experimental.pallas import tpu as pltpu

F32 = jnp.float32
BF16 = jnp.bfloat16

D_MODEL = 4096
BATCH = 32
SEQ = 256
DEPTH = 2
DEC_BATCH = 2
DEC_SEQ = 1024
PAST_LEN = 512
GRID_W = 64
A_HEADS = 8
A_DK = 128
A_DV = 128
B_HEADS = 4
B_DK = 128
B_DV = 256
GLA_RANK = 16
GLA_GATE_NORM = 16.0
C_HEADS = 16
Q_LORA = 1024
KV_LORA = 512
QK_NOPE = 128
QK_ROPE = 64
C_DV = 128
QK_DIM = QK_NOPE + QK_ROPE
ROPE_AXIS = QK_ROPE // 2
ROPE_BASE = 10000.0
N_MOD = 6
D_FF = 4 * D_MODEL
EPS = 1e-6

N_PROMPT = BATCH * SEQ
N_SAMPLE = DEC_BATCH * DEC_SEQ
N_TOK = N_PROMPT + N_SAMPLE
MOD_ROWS = 8

Z_MAIN = 8192
ROW_ALIGN = 32

LANES = 128
MXU_COLS = 256
V7X_VMEM_BYTES = 64 * 1024 * 1024
VMEM_BIG = 60 * 1024 * 1024
VMEM_MID = 40 * 1024 * 1024

LOG2_E = 1.4426950408889634
CH = 32
SB = 8


def _params(sem, vmem=VMEM_MID):
    return pltpu.CompilerParams(dimension_semantics=sem, vmem_limit_bytes=vmem)


def _mod_row(i, tm):
    return jnp.where(i < N_PROMPT // tm, 0, 1 + (i - N_PROMPT // tm) // (DEC_SEQ // tm))


def _sigmoid(x):
    return 1.0 / (1.0 + jnp.exp(-x))


def _log1p_of_exp_neg(t):
    return jnp.log(1.0 + jnp.exp(-t))


def _log_sigmoid(x):
    return jnp.minimum(x, 0.0) - _log1p_of_exp_neg(jnp.abs(x))


def _mod_kernel(cond_ref, w_ref, b_ref, o_ref):
    c = cond_ref[...]
    s = (c * _sigmoid(c)).astype(BF16)
    o_ref[...] = jnp.dot(s, w_ref[...].astype(BF16), preferred_element_type=F32) + b_ref[...]


def _modulation(cond, w_mod, b_mod):
    tn = 512
    n = N_MOD * D_MODEL
    return pl.pallas_call(
        _mod_kernel,
        grid=(DEPTH, n // tn),
        in_specs=[
            pl.BlockSpec((MOD_ROWS, D_MODEL), lambda l, j: (0, 0)),
            pl.BlockSpec((None, D_MODEL, tn), lambda l, j: (l, 0, j)),
            pl.BlockSpec((None, 1, tn), lambda l, j: (l, 0, j)),
        ],
        out_specs=pl.BlockSpec((None, MOD_ROWS, tn), lambda l, j: (l, 0, j)),
        out_shape=jax.ShapeDtypeStruct((DEPTH, MOD_ROWS, n), F32),
        compiler_params=_params(("parallel", "parallel")),
        name="modulation",
    )(cond, w_mod, b_mod.reshape(DEPTH, 1, n))


def _norm_mod_kernel(x_ref, g_ref, sh_ref, sc_ref, *rest):
    o_ref = rest[-1]
    x = x_ref[...]
    ms = jnp.mean(x * x, axis=-1, keepdims=True)
    y = x * lax.rsqrt(ms + EPS) * g_ref[...]
    o_ref[...] = (y * (1.0 + sc_ref[...]) + sh_ref[...]).astype(BF16)


def _norm_mod(x, gain, mod, layer, shift_chunk, row0=0, o_prev=None):
    tm = 512
    base = layer * MOD_ROWS
    blk0 = row0 // tm
    mod_spec = lambda chunk: pl.BlockSpec(
        (None, 1, D_MODEL), lambda i: (base + _mod_row(blk0 + i, tm), 0, chunk))
    in_specs = [pl.BlockSpec((tm, D_MODEL), lambda i: (i, 0)),
                pl.BlockSpec((1, D_MODEL), lambda i: (0, 0)),
                mod_spec(shift_chunk), mod_spec(shift_chunk + 1)]
    args = [x, gain.reshape(1, D_MODEL), mod, mod]
    if o_prev is not None:
        in_specs.append(pl.BlockSpec(memory_space=pl.ANY))
        args.append(o_prev)
    return pl.pallas_call(
        _norm_mod_kernel,
        grid=(x.shape[0] // tm,),
        in_specs=in_specs,
        out_specs=pl.BlockSpec((tm, D_MODEL), lambda i: (blk0 + i, 0)),
        out_shape=jax.ShapeDtypeStruct((N_TOK, D_MODEL), BF16),
        input_output_aliases={} if o_prev is None else {4: 0},
        compiler_params=_params(("parallel",)),
        name="norm_mod",
    )(*args)


def _cast_weight(w_ref, wb_ref):
    @pl.when(pl.program_id(1) == 0)
    def _():
        wb_ref[...] = w_ref[...].astype(BF16)
    return wb_ref


TRANS_B = (((1,), (1,)), ((), ()))


def _mm_kernel(x_ref, w_ref, o_ref, wb_ref, *, relu2, transposed):
    w_ref = _cast_weight(w_ref, wb_ref)
    if transposed:
        acc = lax.dot_general(x_ref[...], w_ref[...], TRANS_B, preferred_element_type=F32)
    else:
        acc = jnp.dot(x_ref[...], w_ref[...], preferred_element_type=F32)
    if relu2:
        acc = jnp.square(jnp.maximum(acc, 0.0))
    o_ref[...] = acc.astype(o_ref.dtype)


def _matmul(x, w, out_dtype, layer, n_cols, relu2=False, transposed=False, name="matmul"):
    tm, tn = 512, 1024
    m, k = x.shape
    if transposed:
        w_spec = pl.BlockSpec((None, tn, k), lambda j, i: (layer, j, 0))
    else:
        w_spec = pl.BlockSpec((None, k, tn), lambda j, i: (layer, 0, j))
    return pl.pallas_call(
        functools.partial(_mm_kernel, relu2=relu2, transposed=transposed),
        grid=(n_cols // tn, m // tm),
        in_specs=[pl.BlockSpec((tm, k), lambda j, i: (i, 0)), w_spec],
        out_specs=pl.BlockSpec((tm, tn), lambda j, i: (i, j)),
        out_shape=jax.ShapeDtypeStruct((m, n_cols), out_dtype),
        scratch_shapes=[pltpu.VMEM((tn, k) if transposed else (k, tn), BF16)],
        compiler_params=_params(("parallel", "arbitrary"), VMEM_BIG),
        name=name,
    )(x, w)


def _cast_rows_kernel(w_ref, o_ref):
    o_ref[...] = w_ref[0].astype(BF16)


def _cast_rows(w, layer, row0, n_rows, block_rows, name):
    k = w.shape[2]
    assert row0 % ROW_ALIGN == 0 and block_rows % ROW_ALIGN == 0 and n_rows % block_rows == 0
    start = lambda r: pl.multiple_of(ROW_ALIGN * (row0 // ROW_ALIGN + (block_rows // ROW_ALIGN) * r),
                                     ROW_ALIGN)
    return pl.pallas_call(
        _cast_rows_kernel,
        grid=(n_rows // block_rows,),
        in_specs=[pl.BlockSpec((pl.Element(1), pl.Element(block_rows), pl.Element(k)),
                               lambda r: (layer, start(r), 0))],
        out_specs=pl.BlockSpec((block_rows, k), lambda r: (r, 0)),
        out_shape=jax.ShapeDtypeStruct((n_rows, k), BF16),
        compiler_params=_params(("parallel",)),
        name=name,
    )(w)


def _in_small_kernel(x_ref, w_ref, gq_ref, gkv_ref, qlat_ref, ckvk_ref, lr_ref):
    x = x_ref[...]
    tm = x.shape[0]
    n_kv, n_lr = KV_LORA + QK_ROPE, 2 * GLA_RANK
    cq = lax.dot_general(x, w_ref[:Q_LORA, :], TRANS_B, preferred_element_type=F32)
    rest = lax.dot_general(x, w_ref[Q_LORA:, :], TRANS_B, preferred_element_type=F32)
    qlat_ref[...] = (cq * lax.rsqrt(jnp.mean(cq * cq, axis=-1, keepdims=True) + EPS)
                     * gq_ref[...]).astype(BF16)
    ckv = rest[:, :KV_LORA]
    ckvk_ref[:, :KV_LORA] = (ckv * lax.rsqrt(jnp.mean(ckv * ckv, axis=-1, keepdims=True) + EPS)
                             * gkv_ref[...])
    ckvk_ref[:, KV_LORA:] = rest[:, KV_LORA:n_kv]
    lr_ref[:, :QK_ROPE + n_lr] = rest[:, KV_LORA:]
    lr_ref[:, QK_ROPE + n_lr:] = jnp.zeros((tm, LANES - QK_ROPE - n_lr), F32)


def _in_small(h, w_small, q_gain, kv_gain):
    tm = 512
    n_kv = KV_LORA + QK_ROPE
    row = lambda width: pl.BlockSpec((tm, width), lambda i: (i, 0))
    full = lambda a: pl.BlockSpec(a.shape, lambda i: (0, 0))
    q_gain, kv_gain = q_gain.reshape(1, Q_LORA), kv_gain.reshape(1, KV_LORA)
    return pl.pallas_call(
        _in_small_kernel,
        grid=(N_TOK // tm,),
        in_specs=[row(D_MODEL), full(w_small), full(q_gain), full(kv_gain)],
        out_specs=[row(Q_LORA), row(n_kv), row(LANES)],
        out_shape=[jax.ShapeDtypeStruct((N_TOK, Q_LORA), BF16),
                   jax.ShapeDtypeStruct((N_TOK, n_kv), F32),
                   jax.ShapeDtypeStruct((N_TOK, LANES), F32)],
        compiler_params=_params(("parallel",), VMEM_BIG),
        name="in_proj_small",
    )(h, w_small, q_gain, kv_gain)


def _mm_res_kernel(x_ref, w_ref, res_ref, g_ref, *rest):
    o_ref, wb_ref = rest[-2:]
    w_ref = _cast_weight(w_ref, wb_ref)
    x = x_ref[...]
    for c0 in range(0, o_ref.shape[1], MXU_COLS):
        cols = slice(c0, c0 + MXU_COLS)
        part = jnp.dot(x, w_ref[:, cols], preferred_element_type=F32)
        o_ref[:, cols] = res_ref[:, cols] + g_ref[:, cols] * part


def _matmul_residual(a, w, res, mod, layer, gate_chunk, k_block, name, rows=(0, N_TOK),
                     res_row0=None, out_rows=N_TOK, out_row0=None, o_prev=None):
    tm, tn, tk = 512, 1024, D_MODEL
    n = w.shape[2]
    base = layer * MOD_ROWS
    cpc = D_MODEL // tn
    blk0 = rows[0] // tm
    res_blk0 = blk0 if res_row0 is None else res_row0 // tm
    out_blk0 = blk0 if out_row0 is None else out_row0 // tm
    in_specs = [
        pl.BlockSpec((tm, tk), lambda j, i: (blk0 + i, k_block)),
        pl.BlockSpec((None, tk, tn), lambda j, i: (layer, k_block, j)),
        pl.BlockSpec((tm, tn), lambda j, i: (res_blk0 + i, j)),
        pl.BlockSpec((None, 1, tn),
                     lambda j, i: (base + _mod_row(blk0 + i, tm), 0, gate_chunk * cpc + j)),
    ]
    args = [a, w, res, mod]
    if o_prev is not None:
        in_specs.append(pl.BlockSpec(memory_space=pl.ANY))
        args.append(o_prev)
    return pl.pallas_call(
        _mm_res_kernel,
        grid=(n // tn, rows[1] // tm),
        in_specs=in_specs,
        out_specs=pl.BlockSpec((tm, tn), lambda j, i: (out_blk0 + i, j)),
        out_shape=jax.ShapeDtypeStruct((out_rows, n), F32),
        input_output_aliases={} if o_prev is None else {4: 0},
        scratch_shapes=[pltpu.VMEM((tk, tn), BF16)],
        compiler_params=_params(("parallel", "arbitrary"), VMEM_BIG),
        name=name,
    )(*args)


def _merge_kernel(h_ref, wga_ref, wgb_ref, wgc_ref, oa_ref, wa_ref, ob_ref, wb_ref, oc_ref, wc_ref,
                  o_ref):
    h = h_ref[...]
    branches = ((wga_ref, oa_ref[...], wa_ref), (wgb_ref, ob_ref[...], wb_ref),
                (wgc_ref, oc_ref[...], wc_ref))
    for c0 in range(0, o_ref.shape[1], MXU_COLS):
        cols = slice(c0, c0 + MXU_COLS)
        acc = None
        for wg_ref, o_b, w_ref in branches:
            g = lax.dot_general(h, wg_ref[cols, :], TRANS_B, preferred_element_type=F32)
            p = jnp.dot(o_b, w_ref[:, cols], preferred_element_type=F32)
            term = _sigmoid(g) * p
            acc = term if acc is None else acc + term
        o_ref[:, cols] = acc.astype(BF16)


def _merge(h, wg, o_a, w_a, o_b, w_b, o_c, w_c, layer):
    tm, tn = 512, 512
    nj = D_MODEL // tn
    act = lambda width: pl.BlockSpec((tm, width), lambda j, i: (i, 0))
    wcol = lambda rows: pl.BlockSpec((None, rows, tn), lambda j, i: (layer, 0, j))
    gate = lambda b: pl.BlockSpec((tn, D_MODEL), lambda j, i: (b * nj + j, 0))
    return pl.pallas_call(
        _merge_kernel,
        grid=(nj, N_TOK // tm),
        in_specs=[act(D_MODEL), gate(0), gate(1), gate(2),
                  act(o_a.shape[1]), wcol(w_a.shape[1]),
                  act(o_b.shape[1]), wcol(w_b.shape[1]),
                  act(o_c.shape[1]), wcol(w_c.shape[1])],
        out_specs=pl.BlockSpec((tm, tn), lambda j, i: (i, j)),
        out_shape=jax.ShapeDtypeStruct((N_TOK, D_MODEL), BF16),
        compiler_params=_params(("parallel", "parallel"), VMEM_BIG),
        name="merge",
    )(h, wg, wg, wg, o_a, w_a, o_b, w_b, o_c, w_c)


PAD = 8


def _shifted(ref, n_rows, d, reverse):
    return ref[pl.ds(PAD + d if reverse else PAD - d, n_rows), :]


def _seg_cumsum(g, rowc, reverse, c_scr):
    n_rows = g.shape[0]
    x = g
    for sh in (1, 2, 4, 8, 16):
        mask = rowc < CH - sh if reverse else rowc >= sh
        if sh < PAD:
            c_scr[pl.ds(PAD, n_rows), :] = x
            moved = _shifted(c_scr, n_rows, sh, reverse)
        else:
            moved = pltpu.roll(x, n_rows - sh if reverse else sh, 0)
        x = x + jnp.where(mask, moved, 0.0)
    return x


def _scan_dir(q, k, v, g, s0t, reverse, scratch, accumulate, want_state):
    o_scr, k_scr, c_scr, v_scr = scratch
    n_rows, dk = q.shape
    n = n_rows // CH
    rowc = lax.broadcasted_iota(jnp.int32, (n_rows, dk), 0) & (CH - 1)
    cum = _seg_cumsum(g * LOG2_E, rowc, reverse, c_scr)
    c_scr[pl.ds(PAD, n_rows), :] = cum
    k_scr[pl.ds(PAD, n_rows), :] = k

    def rows_of(r, count):
        return jnp.broadcast_to(cum[r:r + 1, :], (count, dk))

    def chunk_row(r):
        return jnp.concatenate([rows_of(c * CH + r, CH) for c in range(n)], axis=0)

    last_row = 0 if reverse else CH - 1
    last_b = chunk_row(last_row)
    qt = (q * jnp.exp2(cum)).astype(BF16)
    kh = (k * jnp.exp2(last_b - cum)).astype(BF16)
    vb = v.astype(BF16)

    rows = rowc & (SB - 1)
    o_diag = jnp.sum(q * k, axis=-1, keepdims=True) * v
    for d in range(1, SB):
        mask = rows < SB - d if reverse else rows >= d
        kd = _shifted(k_scr, n_rows, d, reverse)
        cd = _shifted(c_scr, n_rows, d, reverse)
        vd = _shifted(v_scr, n_rows, d, reverse)
        e = jnp.exp2(jnp.where(mask, cum - cd, -jnp.inf))
        o_diag = o_diag + jnp.sum(q * kd * e, axis=-1, keepdims=True) * vd
    if accumulate:
        o_scr[...] += o_diag
    else:
        o_scr[...] = o_diag

    sub = rowc // SB
    nsb = CH // SB
    if reverse:
        own = [rows_of(c * CH + SB * min(i + 1, nsb - 1), SB) for c in range(n) for i in range(nsb)]
        has_piv = sub < nsb - 1
    else:
        own = [rows_of(c * CH + SB * max(i, 1) - 1, SB) for c in range(n) for i in range(nsb)]
        has_piv = sub >= 1
    qe = q * jnp.exp2(jnp.where(has_piv, cum - jnp.concatenate(own, axis=0), -jnp.inf))
    qs, ks = [], []
    for pr in range(SB, CH, SB):
        if reverse:
            piv, qmask, kmask = chunk_row(pr), sub == pr // SB - 1, rowc >= pr
        else:
            piv, qmask, kmask = chunk_row(pr - 1), sub == pr // SB, rowc < pr
        qs.append(jnp.where(qmask, qe, 0.0).astype(BF16))
        ks.append((k * jnp.exp2(jnp.where(kmask, piv - cum, -jnp.inf))).astype(BF16))
    qcat = jnp.concatenate(qs, axis=-1)
    kcat = jnp.concatenate(ks, axis=-1)

    trans_a = (((0,), (0,)), ((), ()))
    order = list(range(n - 1, -1, -1) if reverse else range(n))
    rows = [slice(c * CH, (c + 1) * CH) for c in order]
    n_upd = n if want_state else n - 1
    uts = [lax.dot_general(vb[sl], kh[sl], trans_a, preferred_element_type=F32) for sl in rows[:n_upd]]
    atts = [lax.dot_general(qcat[sl], kcat[sl], TRANS_B, preferred_element_type=F32).astype(BF16)
            for sl in rows]
    o_off = [jnp.dot(a, vb[sl], preferred_element_type=F32) for a, sl in zip(atts, rows)]
    st = s0t
    states = []
    for idx, c in enumerate(order):
        states.append(st.astype(BF16))
        if idx < n_upd:
            r = c * CH + last_row
            st = st * jnp.exp2(cum[r:r + 1, :]) + uts[idx]
    for sl, s_in, off in zip(rows, states, o_off):
        o_scr[sl, :] += lax.dot_general(qt[sl], s_in, TRANS_B, preferred_element_type=F32) + off
    return st


def _finish_scan(o_scr, gain_ref, zg_ref, o_ref):
    o = o_scr[...]
    ms = jnp.mean(o * o, axis=-1, keepdims=True)
    zg = zg_ref[...]
    o_ref[...] = (o * lax.rsqrt(ms + EPS) * gain_ref[...] * (zg * _sigmoid(zg))).astype(BF16)


def _split_scan_refs(refs, n_in, sample, n_alias):
    pos = n_in
    s0_ref = None
    if sample:
        s0_ref = refs[pos]
        pos += 1
    pos += n_alias
    o_ref = refs[pos]
    st_ref = None if sample else refs[pos + 1]
    scratch = refs[pos + (1 if sample else 2):]
    return refs[:n_in], s0_ref, o_ref, st_ref, scratch


def _prep_scan_scratch(scratch, v):
    _, k_scr, c_scr, v_scr = scratch
    n_rows = v.shape[0]
    for ref in (k_scr, c_scr, v_scr):
        zero = jnp.zeros((PAD, ref.shape[1]), F32)
        ref[pl.ds(0, PAD), :] = zero
        ref[pl.ds(PAD + n_rows, PAD), :] = zero
    v_scr[pl.ds(PAD, n_rows), :] = v


def _scan_scratch(seq, dk, dv):
    return [pltpu.VMEM((seq, dv), F32), pltpu.VMEM((seq + 2 * PAD, dk), F32),
            pltpu.VMEM((seq + 2 * PAD, dk), F32), pltpu.VMEM((seq + 2 * PAD, dv), F32)]


def _scan_kernel(*refs, head_fn, lane_widths, sample, hps, n_alias):
    ins, s0_ref, o_ref, st_ref, scratch = _split_scan_refs(refs, len(lane_widths), sample, n_alias)
    per_head = len(scratch) // hps
    dv = o_ref.shape[1] // hps

    def lanes(ref, width, hh):
        if width is None:
            return ref
        return ref.at[(slice(None),) * (len(ref.shape) - 1) + (pl.ds(width * hh, width),)]

    for hh in range(hps):
        head_fn([lanes(r, w, hh) for r, w in zip(ins, lane_widths)],
                None if s0_ref is None else s0_ref.at[:, hh],
                lanes(o_ref, dv, hh),
                None if st_ref is None else st_ref.at[:, hh],
                scratch[per_head * hh:per_head * (hh + 1)])


def _hgrn_head(ins, s0_ref, o_ref, st_ref, scratch, *, layer):
    lbl_ref, q_ref, ff_ref, fb_ref, i_ref, zg_ref, gain_ref = ins
    has_state, want_state = s0_ref is not None, st_ref is not None

    logits = [lbl_ref[l] for l in range(DEPTH)]
    mx = functools.reduce(jnp.maximum, logits)
    ex = [jnp.exp(x - mx) for x in logits]
    tot = functools.reduce(lambda a, b: a + b, ex)
    probs = [e / tot for e in ex]
    cum_first = probs[0]
    cum_l = functools.reduce(lambda a, b: a + b, probs[:layer + 1])
    lb = cum_l - cum_first

    q = q_ref[...]
    v = i_ref[...]
    _prep_scan_scratch(scratch, v)
    for d, (z_ref, reverse) in enumerate(((ff_ref, False), (fb_ref, True))):
        lbd = lb[d:d + 1, :]
        zf = z_ref[...]
        a = jnp.log(lbd)
        b = jnp.log1p(-lbd) + _log_sigmoid(zf)
        delta = a - b
        log_f = jnp.where(delta != delta, a + b,
                          jnp.maximum(a, b) + _log1p_of_exp_neg(jnp.abs(delta)))
        one_minus_f = (1.0 - lbd) * _sigmoid(-zf)
        if has_state:
            s0t = s0_ref[d].T
        else:
            s0t = jnp.zeros((A_DV, A_DK), F32)
        st = _scan_dir(q, one_minus_f, v, log_f, s0t, reverse, scratch, d == 1, want_state)
        if want_state:
            st_ref[d] = st.T
    _finish_scan(scratch[0], gain_ref, zg_ref, o_ref)


def _scan_call(head_fn, name, inputs, heads, dk, dv, state, layer, o_prev):
    sample = o_prev is not None
    seq, nb = (DEC_SEQ, DEC_BATCH) if sample else (SEQ, BATCH)
    hps = 1 if sample else 4
    row0 = N_PROMPT // seq if sample else 0
    in_specs, args, lane_widths = [], [], []
    for arr, kind, *par in inputs:
        args.append(arr)
        if kind == "cols":
            col0, width = par
            blk = col0 // (width * hps)
            in_specs.append(pl.BlockSpec((seq, width * hps), lambda b, h, blk=blk: (row0 + b, blk + h)))
            lane_widths.append(width)
        elif kind == "shared_cols":
            col0, width = par
            in_specs.append(pl.BlockSpec((seq, width), lambda b, h, blk=col0 // width: (row0 + b, blk)))
            lane_widths.append(None)
        elif kind == "heads_last":
            (width,) = par
            lead = arr.shape[:-1]
            in_specs.append(pl.BlockSpec(lead + (width * hps,),
                                         lambda b, h, n=len(lead): (0,) * n + (h,)))
            lane_widths.append(width)
        else:
            in_specs.append(pl.BlockSpec(arr.shape, lambda b, h, n=arr.ndim: (0,) * n))
            lane_widths.append(None)
    state_spec = pl.BlockSpec((None, None, 2, hps, dk, dv), lambda b, h: (b, layer, 0, h, 0, 0))
    aliases = {}
    if sample:
        in_specs += [state_spec, pl.BlockSpec(memory_space=pl.ANY)]
        aliases = {len(args) + 1: 0}
        args += [state, o_prev]
    elif state is not None:
        in_specs.append(pl.BlockSpec(memory_space=pl.ANY))
        aliases = {len(args): 1}
        args.append(state)
    out_specs = [pl.BlockSpec((seq, dv * hps), lambda b, h: (row0 + b, h))]
    out_shape = [jax.ShapeDtypeStruct((N_TOK, heads * dv), BF16)]
    if not sample:
        out_specs.append(state_spec)
        out_shape.append(jax.ShapeDtypeStruct((nb, DEPTH, 2, heads, dk, dv), F32))
    return pl.pallas_call(
        functools.partial(_scan_kernel, head_fn=head_fn, lane_widths=tuple(lane_widths),
                          sample=sample, hps=hps, n_alias=len(aliases)),
        grid=(nb, heads // hps),
        in_specs=in_specs,
        out_specs=out_specs,
        out_shape=out_shape,
        input_output_aliases=aliases,
        scratch_shapes=_scan_scratch(seq, dk, dv) * hps,
        compiler_params=_params(("parallel", "parallel")),
        name=name + ("_sample" if sample else "_prompt"),
    )(*args)


def _hgrn(z, lb_logits, out_gain, state, layer, o_prev=None):
    width = A_HEADS * A_DK
    inputs = [(lb_logits, "heads_last", A_DK)]
    inputs += [(z, "cols", g * width, A_DK) for g in range(5)]
    inputs += [(out_gain.reshape(1, A_DV), "whole")]
    return _scan_call(functools.partial(_hgrn_head, layer=layer), "hgrn", inputs,
                      A_HEADS, A_DK, A_DV, state, layer, o_prev)


def _gla_head(ins, s0_ref, o_ref, st_ref, scratch):
    q_ref, k_ref, v_ref, zr_ref, lr_ref, wg_ref, bg_ref, gain_ref = ins
    has_state, want_state = s0_ref is not None, st_ref is not None

    q = q_ref[...] * (B_DK ** -0.5)
    k = k_ref[...]
    v = v_ref[...]
    _prep_scan_scratch(scratch, v)
    lr = lr_ref[...].astype(BF16)
    for d, reverse in enumerate((False, True)):
        logit = jnp.dot(lr, wg_ref[d], preferred_element_type=F32) + bg_ref[d]
        g = _log_sigmoid(logit) / GLA_GATE_NORM
        if has_state:
            s0t = s0_ref[d].T
        else:
            s0t = jnp.zeros((B_DV, B_DK), F32)
        st = _scan_dir(q, k, v, g, s0t, reverse, scratch, d == 1, want_state)
        if want_state:
            st_ref[d] = st.T
    _finish_scan(scratch[0], gain_ref, zr_ref, o_ref)


def _gla(z, zlr, w_gate_ext, b_gate, out_gain, state, layer, o_prev=None):
    inputs = [(z, "cols", 5120, B_DK), (z, "cols", 5632, B_DK), (z, "cols", 6144, B_DV),
              (z, "cols", 7168, B_DV), (zlr, "shared_cols", 0, LANES),
              (w_gate_ext, "heads_last", B_DK), (b_gate.reshape(2, 1, B_HEADS * B_DK), "heads_last", B_DK),
              (out_gain.reshape(1, B_DV), "whole")]
    return _scan_call(_gla_head, "gla", inputs, B_HEADS, B_DK, B_DV, state, layer, o_prev)


def _swap_halves(t, lane):
    width = t.shape[-1]
    return jnp.where((lane & 31) < 16, pltpu.roll(t, width - 16, 1), pltpu.roll(t, 16, 1))


def _qup_kernel(x_ref, w_ref, gn_ref, gr_ref, c_ref, s_ref, qn_ref, qr_ref):
    x = x_ref[...]
    nope_w = C_HEADS * QK_NOPE
    tm = x.shape[0]
    lane = lax.broadcasted_iota(jnp.int32, (tm, LANES), 1)
    low = lane < QK_ROPE
    gn = gn_ref[...]
    gr = gr_ref[...]
    cos = c_ref[...]
    sin = s_ref[...]
    strip = lambda c0: jnp.dot(x, w_ref[:, c0:c0 + MXU_COLS], preferred_element_type=F32)
    for p in range(C_HEADS // 2):
        acc = strip(MXU_COLS * p)
        if p % 2 == 0:
            rope2 = strip(nope_w + LANES * p)
        r2 = rope2[:, LANES * (p % 2):LANES * (p % 2 + 1)]
        sq = r2 * r2
        ss_rope = (jnp.sum(jnp.where(low, sq, 0.0), axis=-1, keepdims=True),
                   jnp.sum(jnp.where(low, 0.0, sq), axis=-1, keepdims=True))
        inv = []
        for hh in range(2):
            h = 2 * p + hh
            nh = acc[:, QK_NOPE * hh: QK_NOPE * (hh + 1)]
            ssn = jnp.sum(nh * nh, axis=-1, keepdims=True)
            r = lax.rsqrt((ssn + ss_rope[hh]) / QK_DIM + EPS)
            qn_ref[:, QK_NOPE * h: QK_NOPE * (h + 1)] = (nh * r * gn).astype(BF16)
            inv.append(r)
        t = r2 * jnp.where(low, inv[0], inv[1]) * gr
        qr_ref[:, LANES * p: LANES * (p + 1)] = (t * cos + _swap_halves(t, lane) * sin).astype(BF16)


def _qup(qlat, w, gn, gr, cos, sin):
    tm = 512
    n_rows = qlat.shape[0]
    row = lambda width: pl.BlockSpec((tm, width), lambda i: (i, 0))
    full = lambda a: pl.BlockSpec(a.shape, lambda i: (0, 0))
    return pl.pallas_call(
        _qup_kernel,
        grid=(n_rows // tm,),
        in_specs=[row(Q_LORA), full(w), full(gn), full(gr), row(LANES), row(LANES)],
        out_specs=[row(C_HEADS * QK_NOPE), row(C_HEADS * QK_ROPE)],
        out_shape=[jax.ShapeDtypeStruct((n_rows, C_HEADS * QK_NOPE), BF16),
                   jax.ShapeDtypeStruct((n_rows, C_HEADS * QK_ROPE), BF16)],
        compiler_params=_params(("parallel",)),
        name="mla_q_up",
    )(qlat, w, gn, gr, cos, sin)


def _kvup_kernel(*refs, sample):
    if sample:
        cache_ref, x_ref, w_ref, gn_ref, gr_ref, c_ref, s_ref, kn_ref, kr_ref, v_ref = refs
        from_cache = pl.program_id(1) == 0
        x = jnp.where(from_cache, cache_ref[...], x_ref[...])
    else:
        x_ref, w_ref, gn_ref, gr_ref, kn_ref, kr_ref, v_ref = refs
        x = x_ref[...]
    ckv = x[:, :KV_LORA].astype(BF16)
    nope_w = C_HEADS * QK_NOPE
    tm = x.shape[0]
    lane = lax.broadcasted_iota(jnp.int32, (tm, LANES), 1)
    low = lane < QK_ROPE
    kr = x[:, KV_LORA:]
    ss_rope = jnp.sum(kr * kr, axis=-1, keepdims=True)
    rot = jnp.concatenate([kr, kr], axis=-1) * gr_ref[...]
    if sample:
        turned = rot * c_ref[...] + _swap_halves(rot, lane) * s_ref[...]
        rot = jnp.where(from_cache, rot, turned)
    gn = gn_ref[...]
    for p in range(C_HEADS // 2):
        pair = slice(2 * QK_NOPE * p, 2 * QK_NOPE * (p + 1))
        acc = jnp.dot(ckv, w_ref[:, pair], preferred_element_type=F32)
        v_ref[:, pair] = jnp.dot(ckv, w_ref[:, nope_w + pair.start:nope_w + pair.stop],
                                 preferred_element_type=F32).astype(BF16)
        inv = []
        for hh in range(2):
            h = 2 * p + hh
            nh = acc[:, QK_NOPE * hh: QK_NOPE * (hh + 1)]
            ssn = jnp.sum(nh * nh, axis=-1, keepdims=True)
            r = lax.rsqrt((ssn + ss_rope) / QK_DIM + EPS)
            kn_ref[:, QK_NOPE * h: QK_NOPE * (h + 1)] = (nh * r * gn).astype(BF16)
            inv.append(r)
        kr_ref[:, LANES * p: LANES * (p + 1)] = (rot * jnp.where(low, inv[0], inv[1])).astype(BF16)


def _kvup(ckvk, w, gn, gr, cache=None, layer=None, cos=None, sin=None):
    tm = PAST_LEN
    sample = cache is not None
    width = KV_LORA + QK_ROPE
    if sample:
        per_seq = 1 + DEC_SEQ // tm
        grid = (DEC_BATCH, per_seq)
        row_blk = lambda b, t: b * per_seq + t
        new_blk = lambda b, t: N_PROMPT // tm + b * (per_seq - 1) + jnp.maximum(t - 1, 0)
        full = lambda a: pl.BlockSpec(a.shape, lambda b, t: (0, 0))
        table = pl.BlockSpec((tm, LANES), lambda b, t: (jnp.maximum(t - 1, 0), 0))
        in_specs = [pl.BlockSpec((None, None, tm, width), lambda b, t: (b, layer, 0, 0)),
                    pl.BlockSpec((tm, width), lambda b, t: (new_blk(b, t), 0)),
                    full(w), full(gn), full(gr), table, table]
        args = (cache, ckvk, w, gn, gr, cos, sin)
        row = lambda wd: pl.BlockSpec((tm, wd), lambda b, t: (row_blk(b, t), 0))
        n_rows = DEC_BATCH * per_seq * tm
    else:
        grid = (N_PROMPT // tm,)
        full = lambda a: pl.BlockSpec(a.shape, lambda i: (0, 0))
        in_specs = [pl.BlockSpec((tm, width), lambda i: (i, 0)), full(w), full(gn), full(gr)]
        args = (ckvk, w, gn, gr)
        row = lambda wd: pl.BlockSpec((tm, wd), lambda i: (i, 0))
        n_rows = N_PROMPT
    return pl.pallas_call(
        functools.partial(_kvup_kernel, sample=sample),
        grid=grid,
        in_specs=in_specs,
        out_specs=[row(C_HEADS * QK_NOPE), row(C_HEADS * QK_ROPE), row(C_HEADS * C_DV)],
        out_shape=[jax.ShapeDtypeStruct((n_rows, C_HEADS * QK_NOPE), BF16),
                   jax.ShapeDtypeStruct((n_rows, C_HEADS * QK_ROPE), BF16),
                   jax.ShapeDtypeStruct((n_rows, C_HEADS * C_DV), BF16)],
        compiler_params=_params(("parallel",) * len(grid)),
        name="mla_kv_up_sample" if sample else "mla_kv_up_prompt",
    )(*args)


def _attn_kernel(qn_ref, qr_ref, kn_ref, kr_ref, v_ref, *rest, n_seq):
    o_ref = rest[-1]
    lq = qr_ref.shape[0] // n_seq
    lk = kr_ref.shape[0] // n_seq
    lane = lax.broadcasted_iota(jnp.int32, (lq, LANES), 1)
    blocks = [(slice(s * lq, (s + 1) * lq), slice(s * lk, (s + 1) * lk),
               slice(QK_NOPE * hh, QK_NOPE * (hh + 1)), hh)
              for s in range(n_seq) for hh in range(2)]
    scores = []
    for qrows, krows, cols, hh in blocks:
        qr = qr_ref[qrows, :]
        own = (lane < QK_ROPE) if hh == 0 else (lane >= QK_ROPE)
        qrm = jnp.where(own, qr, jnp.zeros_like(qr))
        s = lax.dot_general(qn_ref[qrows, cols], kn_ref[krows, cols], TRANS_B,
                            preferred_element_type=F32)
        s = s + lax.dot_general(qrm, kr_ref[krows, :], TRANS_B, preferred_element_type=F32)
        scores.append(s * (QK_DIM ** -0.5 * LOG2_E))
    tops = [jnp.max(s, axis=-1, keepdims=True) for s in scores]
    exps = [jnp.exp2(s - m) for s, m in zip(scores, tops)]
    sums = [jnp.sum(e, axis=-1, keepdims=True) for e in exps]
    probs = [(e / t).astype(BF16) for e, t in zip(exps, sums)]
    for p, (qrows, krows, cols, _) in zip(probs, blocks):
        o_ref[qrows, cols] = jnp.dot(p, v_ref[krows, cols], preferred_element_type=F32).astype(BF16)


def _attention(qn, qr, kn, kr, v, o_prev=None):
    sample = o_prev is not None
    if sample:
        q_row0, tq, tk, n_seq = N_PROMPT, SEQ, PAST_LEN + DEC_SEQ, 1
        grid = (DEC_BATCH, C_HEADS // 2, DEC_SEQ // tq)
    else:
        n_seq = 4
        q_row0, tq, tk = 0, n_seq * SEQ, n_seq * SEQ
        grid = (BATCH // n_seq, C_HEADS // 2, 1)
    nq = grid[2]
    qspec = lambda width: pl.BlockSpec((tq, width), lambda b, p, i: (q_row0 // tq + b * nq + i, p))
    kspec = lambda width: pl.BlockSpec((tk, width), lambda b, p, i: (b, p))
    in_specs = [qspec(2 * QK_NOPE), qspec(LANES), kspec(2 * QK_NOPE), kspec(LANES), kspec(2 * C_DV)]
    args = [qn, qr, kn, kr, v]
    if sample:
        in_specs.append(pl.BlockSpec(memory_space=pl.ANY))
        args.append(o_prev)
    return pl.pallas_call(
        functools.partial(_attn_kernel, n_seq=n_seq),
        grid=grid,
        in_specs=in_specs,
        out_specs=qspec(2 * C_DV),
        out_shape=jax.ShapeDtypeStruct((N_TOK, C_HEADS * C_DV), BF16),
        input_output_aliases={5: 0} if sample else {},
        compiler_params=_params(("parallel", "parallel", "parallel")),
        name="mla_attention_sample" if sample else "mla_attention_prompt",
    )(*args)


def _rope_tables():
    rows = DEC_SEQ // GRID_W
    row = jnp.repeat(jnp.arange(rows, dtype=F32), GRID_W)
    col = jnp.tile(jnp.arange(GRID_W, dtype=F32), rows)
    inv_freq = ROPE_BASE ** (-jnp.arange(0, ROPE_AXIS, 2, dtype=F32) / ROPE_AXIS)
    ar = row[:, None] * inv_freq
    ac = col[:, None] * inv_freq
    cos = jnp.concatenate([jnp.cos(ar), jnp.cos(ar), jnp.cos(ac), jnp.cos(ac)], axis=-1)
    sin = jnp.concatenate([-jnp.sin(ar), jnp.sin(ar), -jnp.sin(ac), jnp.sin(ac)], axis=-1)
    return jnp.tile(cos, (1, 2)), jnp.tile(sin, (1, 2))


def _layer_weights(l, gla_w_gate_up, mla_w_q_up, mla_w_kv_up, mla_q_norm, mla_k_norm):
    ext =jnp.zeros((2, LANES, B_HEADS * B_DK), F32)
    ext = ext.at[0, QK_ROPE:QK_ROPE + GLA_RANK].set(gla_w_gate_up[l, 0])
    ext = ext.at[1, QK_ROPE + GLA_RANK:QK_ROPE + 2 * GLA_RANK].set(gla_w_gate_up[l, 1])
    wq = mla_w_q_up[l].reshape(Q_LORA, C_HEADS, QK_DIM)
    wq = jnp.concatenate([wq[:, :, :QK_NOPE].reshape(Q_LORA, -1),
                          wq[:, :, QK_NOPE:].reshape(Q_LORA, -1)], axis=1).astype(BF16)
    wkv = mla_w_kv_up[l].reshape(KV_LORA, C_HEADS, QK_NOPE + C_DV)
    wkv = jnp.concatenate([wkv[:, :, :QK_NOPE].reshape(KV_LORA, -1),
                           wkv[:, :, QK_NOPE:].reshape(KV_LORA, -1)], axis=1).astype(BF16)
    split = lambda g: (g[:QK_NOPE].reshape(1, QK_NOPE), jnp.tile(g[QK_NOPE:], 2).reshape(1, LANES))
    return ext.astype(BF16), wq, wkv, split(mla_q_norm[l]), split(mla_k_norm[l])


def kernel(x_prompt, x_sample, cache_mla, state_hgrn, state_gla, c, c_ctx, norm1, w_mod, b_mod, w_in,
           hgrn_lb_logits, hgrn_out_norm, gla_w_gate_up, gla_b_gate, gla_out_norm, mla_q_a_norm,
           mla_w_q_up, mla_kv_a_norm, mla_w_kv_up, mla_q_norm, mla_k_norm, w_branch_a, w_branch_b,
           w_branch_c, w_out, norm2, w_mlp_in, w_mlp_out):
    ctx_rows, lat_rows = (0, N_PROMPT), (N_PROMPT, N_SAMPLE)
    x = (x_prompt.reshape(N_PROMPT, D_MODEL), x_sample.reshape(N_SAMPLE, D_MODEL))
    cond =jnp.concatenate([c_ctx[None], c, jnp.zeros((MOD_ROWS - 1 - DEC_BATCH, D_MODEL), F32)])
    mod = _modulation(cond, w_mod, b_mod).reshape(DEPTH * MOD_ROWS, 1, N_MOD * D_MODEL)

    cos_s, sin_s = _rope_tables()
    cos_q = jnp.concatenate([jnp.ones((N_PROMPT, LANES), F32), jnp.tile(cos_s, (DEC_BATCH, 1))])
    sin_q = jnp.concatenate([jnp.zeros((N_PROMPT, LANES), F32), jnp.tile(sin_s, (DEC_BATCH, 1))])

    wb_a, wb_b, wb_c = w_branch_a.astype(BF16), w_branch_b.astype(BF16), w_branch_c.astype(BF16)
    w_in_t = jnp.swapaxes(w_in, 1, 2)
    narrow0, gate0 = Z_MAIN, Z_MAIN + 2 * GLA_RANK + Q_LORA + KV_LORA + QK_ROPE
    caches = []
    new_hgrn = new_gla = None
    for l in range(DEPTH):
        wlr, wq, wkv, (qgn, qgr), (kgn, kgr) = _layer_weights(
            l, gla_w_gate_up, mla_w_q_up, mla_w_kv_up, mla_q_norm, mla_k_norm)
        ws = _cast_rows(w_in_t, l, narrow0, gate0 - narrow0, (gate0 - narrow0) // 3, "cast_narrow")
        ws = jnp.concatenate([ws[2 * GLA_RANK:], ws[:2 * GLA_RANK]])
        wg = _cast_rows(w_in_t, l, gate0, 3 * D_MODEL, 512, "cast_gates")

        if l == 0:
            h = _norm_mod(x[0], norm1[l], mod, l, 0)
            h = _norm_mod(x[1], norm1[l], mod, l, 0, row0=N_PROMPT, o_prev=h)
        else:
            h = _norm_mod(x, norm1[l], mod, l, 0)
        z = _matmul(h, w_in_t, F32, l, Z_MAIN, transposed=True, name="in_proj")
        qlat, ckvk, zlr = _in_small(h, ws, mla_q_a_norm[l], mla_kv_a_norm[l])

        o_a, new_hgrn = _hgrn(z, hgrn_lb_logits, hgrn_out_norm[l], new_hgrn, l)
        (o_a,) = _hgrn(z, hgrn_lb_logits, hgrn_out_norm[l], state_hgrn, l, o_prev=o_a)
        o_b, new_gla = _gla(z, zlr, wlr, gla_b_gate[l], gla_out_norm[l], new_gla, l)
        (o_b,) = _gla(z, zlr, wlr, gla_b_gate[l], gla_out_norm[l], state_gla, l, o_prev=o_b)

        caches.append(ckvk[:N_PROMPT].reshape(BATCH, SEQ, KV_LORA + QK_ROPE))
        qn, qr = _qup(qlat, wq, qgn, qgr, cos_q, sin_q)
        kn_p, kr_p, v_p = _kvup(ckvk, wkv, kgn, kgr)
        kn_s, kr_s, v_s = _kvup(ckvk, wkv, kgn, kgr, cache_mla, l, cos_s, sin_s)
        o_c = _attention(qn, qr, kn_p, kr_p, v_p)
        o_c = _attention(qn, qr, kn_s, kr_s, v_s, o_prev=o_c)

        merged = _merge(h, wg, o_a, wb_a, o_b, wb_b, o_c, wb_c, l)
        if l == 0:
            xo = _matmul_residual(merged, w_out, x[0], mod, l, 2, 0, "out_proj", rows=ctx_rows)
            x = _matmul_residual(merged, w_out, x[1], mod, l, 2, 0, "out_proj", rows=lat_rows,
                                 res_row0=0, o_prev=xo)
        else:
            x = _matmul_residual(merged, w_out, x, mod, l, 2, 0, "out_proj")

        h2 = _norm_mod(x, norm2[l], mod, l, 3)
        u = _matmul(h2, w_mlp_in, BF16, l, D_FF, relu2=True, name="mlp_in")
        n_kb = D_FF // D_MODEL
        for kb in range(n_kb):
            if l == DEPTH - 1 and kb == n_kb - 1:
                x = tuple(_matmul_residual(u, w_mlp_out, x, mod, l, 5, kb, "mlp_out", rows=rows,
                                           out_rows=rows[1], out_row0=0)
                          for rows in (ctx_rows, lat_rows))
            else:
                x = _matmul_residual(u, w_mlp_out, x, mod, l, 5, kb, "mlp_out")

    y_prompt = x[0].reshape(BATCH, SEQ, D_MODEL)
    y_sample = x[1].reshape(DEC_BATCH, DEC_SEQ, D_MODEL)
    return (y_prompt, y_sample, jnp.stack(caches, axis=1), new_hgrn, new_gla)
```

```python
import functools

import numpy as np
import jax
import jax.numpy as jnp
from jax import lax
from jax.experimental import pallas as pl
from jax.experimental.pallas import tpu as pltpu

F32 = jnp.float32
BF16 = jnp.bfloat16

D_MODEL = 4096
BATCH = 32
SEQ = 256
DEPTH = 2
DEC_BATCH = 2
DEC_SEQ = 1024
PAST_LEN = 512
GRID_W = 64
A_HEADS = 8
A_DK = 128
A_DV = 128
B_HEADS = 4
B_DK = 128
B_DV = 256
GLA_RANK = 16
GLA_GATE_NORM = 16.0
C_HEADS = 16
Q_LORA = 1024
KV_LORA = 512
QK_NOPE = 128
QK_ROPE = 64
C_DV = 128
QK_DIM = QK_NOPE + QK_ROPE
ROPE_AXIS = QK_ROPE // 2
ROPE_BASE = 10000.0
N_MOD = 6
D_FF = 4 * D_MODEL
EPS = 1e-6

N_PROMPT = BATCH * SEQ
N_SAMPLE = DEC_BATCH * DEC_SEQ
N_TOK = N_PROMPT + N_SAMPLE
MOD_ROWS = 8

Z_MAIN = 8192
ROW_ALIGN = 32

LANES = 128
MXU_COLS = 256
V7X_VMEM_BYTES = 64 * 1024 * 1024
VMEM_BIG = 60 * 1024 * 1024
VMEM_MID = 40 * 1024 * 1024

LOG2_E = 1.4426950408889634
CH = 32
SB = 8


def _params(sem, vmem=VMEM_MID):
    return pltpu.CompilerParams(dimension_semantics=sem, vmem_limit_bytes=vmem)


def _mod_row(i, tm):
    return jnp.where(i < N_PROMPT // tm, 0, 1 + (i - N_PROMPT // tm) // (DEC_SEQ // tm))


def _sigmoid(x):
    return 1.0 / (1.0 + jnp.exp(-x))


def _log1p_of_exp_neg(t):
    return jnp.log(1.0 + jnp.exp(-t))


def _log_sigmoid(x):
    return jnp.minimum(x, 0.0) - _log1p_of_exp_neg(jnp.abs(x))


def _mod_kernel(cond_ref, w_ref, b_ref, o_ref):
    c = cond_ref[...]
    s = (c * _sigmoid(c)).astype(BF16)
    o_ref[...] = jnp.dot(s, w_ref[...].astype(BF16), preferred_element_type=F32) + b_ref[...]


def _modulation(cond, w_mod, b_mod):
    tn = 512
    n = N_MOD * D_MODEL
    return pl.pallas_call(
        _mod_kernel,
        grid=(DEPTH, n // tn),
        in_specs=[
            pl.BlockSpec((MOD_ROWS, D_MODEL), lambda l, j: (0, 0)),
            pl.BlockSpec((None, D_MODEL, tn), lambda l, j: (l, 0, j)),
            pl.BlockSpec((None, 1, tn), lambda l, j: (l, 0, j)),
        ],
        out_specs=pl.BlockSpec((None, MOD_ROWS, tn), lambda l, j: (l, 0, j)),
        out_shape=jax.ShapeDtypeStruct((DEPTH, MOD_ROWS, n), F32),
        compiler_params=_params(("parallel", "parallel")),
        name="modulation",
    )(cond, w_mod, b_mod.reshape(DEPTH, 1, n))


def _norm_mod_kernel(x_ref, g_ref, sh_ref, sc_ref, *rest):
    o_ref = rest[-1]
    x = x_ref[...]
    ms = jnp.mean(x * x, axis=-1, keepdims=True)
    y = x * lax.rsqrt(ms + EPS) * g_ref[...]
    o_ref[...] = (y * (1.0 + sc_ref[...]) + sh_ref[...]).astype(BF16)


def _norm_mod(x, gain, mod, layer, shift_chunk, row0=0, o_prev=None):
    tm = 512
    base = layer * MOD_ROWS
    blk0 = row0 // tm
    mod_spec = lambda chunk: pl.BlockSpec(
        (None, 1, D_MODEL), lambda i: (base + _mod_row(blk0 + i, tm), 0, chunk))
    in_specs = [pl.BlockSpec((tm, D_MODEL), lambda i: (i, 0)),
                pl.BlockSpec((1, D_MODEL), lambda i: (0, 0)),
                mod_spec(shift_chunk), mod_spec(shift_chunk + 1)]
    args = [x, gain.reshape(1, D_MODEL), mod, mod]
    if o_prev is not None:
        in_specs.append(pl.BlockSpec(memory_space=pl.ANY))
        args.append(o_prev)
    return pl.pallas_call(
        _norm_mod_kernel,
        grid=(x.shape[0] // tm,),
        in_specs=in_specs,
        out_specs=pl.BlockSpec((tm, D_MODEL), lambda i: (blk0 + i, 0)),
        out_shape=jax.ShapeDtypeStruct((N_TOK, D_MODEL), BF16),
        input_output_aliases={} if o_prev is None else {4: 0},
        compiler_params=_params(("parallel",)),
        name="norm_mod",
    )(*args)


def _cast_weight(w_ref, wb_ref):
    @pl.when(pl.program_id(1) == 0)
    def _():
        wb_ref[...] = w_ref[...].astype(BF16)
    return wb_ref


TRANS_B = (((1,), (1,)), ((), ()))


def _mm_kernel(x_ref, w_ref, o_ref, wb_ref, *, relu2, transposed):
    w_ref = _cast_weight(w_ref, wb_ref)
    if transposed:
        acc = lax.dot_general(x_ref[...], w_ref[...], TRANS_B, preferred_element_type=F32)
    else:
        acc = jnp.dot(x_ref[...], w_ref[...], preferred_element_type=F32)
    if relu2:
        acc = jnp.square(jnp.maximum(acc, 0.0))
    o_ref[...] = acc.astype(o_ref.dtype)


def _matmul(x, w, out_dtype, layer, n_cols, relu2=False, transposed=False, name="matmul"):
    tm, tn = 512, 1024
    m, k = x.shape
    if transposed:
        w_spec = pl.BlockSpec((None, tn, k), lambda j, i: (layer, j, 0))
    else:
        w_spec = pl.BlockSpec((None, k, tn), lambda j, i: (layer, 0, j))
    return pl.pallas_call(
        functools.partial(_mm_kernel, relu2=relu2, transposed=transposed),
        grid=(n_cols // tn, m // tm),
        in_specs=[pl.BlockSpec((tm, k), lambda j, i: (i, 0)), w_spec],
        out_specs=pl.BlockSpec((tm, tn), lambda j, i: (i, j)),
        out_shape=jax.ShapeDtypeStruct((m, n_cols), out_dtype),
        scratch_shapes=[pltpu.VMEM((tn, k) if transposed else (k, tn), BF16)],
        compiler_params=_params(("parallel", "arbitrary"), VMEM_BIG),
        name=name,
    )(x, w)


def _cast_rows_kernel(w_ref, o_ref):
    o_ref[...] = w_ref[0].astype(BF16)


def _cast_rows(w, layer, row0, n_rows, block_rows, name):
    k = w.shape[2]
    assert row0 % ROW_ALIGN == 0 and block_rows % ROW_ALIGN == 0 and n_rows % block_rows == 0
    start = lambda r: pl.multiple_of(ROW_ALIGN * (row0 // ROW_ALIGN + (block_rows // ROW_ALIGN) * r),
                                     ROW_ALIGN)
    return pl.pallas_call(
        _cast_rows_kernel,
        grid=(n_rows // block_rows,),
        in_specs=[pl.BlockSpec((pl.Element(1), pl.Element(block_rows), pl.Element(k)),
                               lambda r: (layer, start(r), 0))],
        out_specs=pl.BlockSpec((block_rows, k), lambda r: (r, 0)),
        out_shape=jax.ShapeDtypeStruct((n_rows, k), BF16),
        compiler_params=_params(("parallel",)),
        name=name,
    )(w)


def _in_small_kernel(x_ref, w_ref, gq_ref, gkv_ref, qlat_ref, ckvk_ref, lr_ref):
    x = x_ref[...]
    tm = x.shape[0]
    n_kv, n_lr = KV_LORA + QK_ROPE, 2 * GLA_RANK
    cq = lax.dot_general(x, w_ref[:Q_LORA, :], TRANS_B, preferred_element_type=F32)
    rest = lax.dot_general(x, w_ref[Q_LORA:, :], TRANS_B, preferred_element_type=F32)
    qlat_ref[...] = (cq * lax.rsqrt(jnp.mean(cq * cq, axis=-1, keepdims=True) + EPS)
                     * gq_ref[...]).astype(BF16)
    ckv = rest[:, :KV_LORA]
    ckvk_ref[:, :KV_LORA] = (ckv * lax.rsqrt(jnp.mean(ckv * ckv, axis=-1, keepdims=True) + EPS)
                             * gkv_ref[...])
    ckvk_ref[:, KV_LORA:] = rest[:, KV_LORA:n_kv]
    lr_ref[:, :QK_ROPE + n_lr] = rest[:, KV_LORA:]
    lr_ref[:, QK_ROPE + n_lr:] = jnp.zeros((tm, LANES - QK_ROPE - n_lr), F32)


def _in_small(h, w_small, q_gain, kv_gain):
    tm = 512
    n_kv = KV_LORA + QK_ROPE
    row = lambda width: pl.BlockSpec((tm, width), lambda i: (i, 0))
    full = lambda a: pl.BlockSpec(a.shape, lambda i: (0, 0))
    q_gain, kv_gain = q_gain.reshape(1, Q_LORA), kv_gain.reshape(1, KV_LORA)
    return pl.pallas_call(
        _in_small_kernel,
        grid=(N_TOK // tm,),
        in_specs=[row(D_MODEL), full(w_small), full(q_gain), full(kv_gain)],
        out_specs=[row(Q_LORA), row(n_kv), row(LANES)],
        out_shape=[jax.ShapeDtypeStruct((N_TOK, Q_LORA), BF16),
                   jax.ShapeDtypeStruct((N_TOK, n_kv), F32),
                   jax.ShapeDtypeStruct((N_TOK, LANES), F32)],
        compiler_params=_params(("parallel",), VMEM_BIG),
        name="in_proj_small",
    )(h, w_small, q_gain, kv_gain)


def _mm_res_kernel(x_ref, w_ref, res_ref, g_ref, *rest):
    o_ref, wb_ref = rest[-2:]
    w_ref = _cast_weight(w_ref, wb_ref)
    x = x_ref[...]
    for c0 in range(0, o_ref.shape[1], MXU_COLS):
        cols = slice(c0, c0 + MXU_COLS)
        part = jnp.dot(x, w_ref[:, cols], preferred_element_type=F32)
        o_ref[:, cols] = res_ref[:, cols] + g_ref[:, cols] * part


def _matmul_residual(a, w, res, mod, layer, gate_chunk, k_block, name, rows=(0, N_TOK),
                     res_row0=None, out_rows=N_TOK, out_row0=None, o_prev=None):
    tm, tn, tk = 512, 1024, D_MODEL
    n = w.shape[2]
    base = layer * MOD_ROWS
    cpc = D_MODEL // tn
    blk0 = rows[0] // tm
    res_blk0 = blk0 if res_row0 is None else res_row0 // tm
    out_blk0 = blk0 if out_row0 is None else out_row0 // tm
    in_specs = [
        pl.BlockSpec((tm, tk), lambda j, i: (blk0 + i, k_block)),
        pl.BlockSpec((None, tk, tn), lambda j, i: (layer, k_block, j)),
        pl.BlockSpec((tm, tn), lambda j, i: (res_blk0 + i, j)),
        pl.BlockSpec((None, 1, tn),
                     lambda j, i: (base + _mod_row(blk0 + i, tm), 0, gate_chunk * cpc + j)),
    ]
    args = [a, w, res, mod]
    if o_prev is not None:
        in_specs.append(pl.BlockSpec(memory_space=pl.ANY))
        args.append(o_prev)
    return pl.pallas_call(
        _mm_res_kernel,
        grid=(n // tn, rows[1] // tm),
        in_specs=in_specs,
        out_specs=pl.BlockSpec((tm, tn), lambda j, i: (out_blk0 + i, j)),
        out_shape=jax.ShapeDtypeStruct((out_rows, n), F32),
        input_output_aliases={} if o_prev is None else {4: 0},
        scratch_shapes=[pltpu.VMEM((tk, tn), BF16)],
        compiler_params=_params(("parallel", "arbitrary"), VMEM_BIG),
        name=name,
    )(*args)


def _merge_kernel(h_ref, wga_ref, wgb_ref, wgc_ref, oa_ref, wa_ref, ob_ref, wb_ref, oc_ref, wc_ref,
                  o_ref):
    h = h_ref[...]
    branches = ((wga_ref, oa_ref[...], wa_ref), (wgb_ref, ob_ref[...], wb_ref),
                (wgc_ref, oc_ref[...], wc_ref))
    for c0 in range(0, o_ref.shape[1], MXU_COLS):
        cols = slice(c0, c0 + MXU_COLS)
        acc = None
        for wg_ref, o_b, w_ref in branches:
            g = lax.dot_general(h, wg_ref[cols, :], TRANS_B, preferred_element_type=F32)
            p = jnp.dot(o_b, w_ref[:, cols], preferred_element_type=F32)
            term = _sigmoid(g) * p
            acc = term if acc is None else acc + term
        o_ref[:, cols] = acc.astype(BF16)


def _merge(h, wg, o_a, w_a, o_b, w_b, o_c, w_c, layer):
    tm, tn = 512, 512
    nj = D_MODEL // tn
    act = lambda width: pl.BlockSpec((tm, width), lambda j, i: (i, 0))
    wcol = lambda rows: pl.BlockSpec((None, rows, tn), lambda j, i: (layer, 0, j))
    gate = lambda b: pl.BlockSpec((tn, D_MODEL), lambda j, i: (b * nj + j, 0))
    return pl.pallas_call(
        _merge_kernel,
        grid=(nj, N_TOK // tm),
        in_specs=[act(D_MODEL), gate(0), gate(1), gate(2),
                  act(o_a.shape[1]), wcol(w_a.shape[1]),
                  act(o_b.shape[1]), wcol(w_b.shape[1]),
                  act(o_c.shape[1]), wcol(w_c.shape[1])],
        out_specs=pl.BlockSpec((tm, tn), lambda j, i: (i, j)),
        out_shape=jax.ShapeDtypeStruct((N_TOK, D_MODEL), BF16),
        compiler_params=_params(("parallel", "parallel"), VMEM_BIG),
        name="merge",
    )(h, wg, wg, wg, o_a, w_a, o_b, w_b, o_c, w_c)


PAD = 8


def _shifted(ref, n_rows, d, reverse):
    return ref[pl.ds(PAD + d if reverse else PAD - d, n_rows), :]


def _seg_cumsum(g, rowc, reverse, c_scr):
    n_rows = g.shape[0]
    x = g
    for sh in (1, 2, 4, 8, 16):
        mask = rowc < CH - sh if reverse else rowc >= sh
        if sh < PAD:
            c_scr[pl.ds(PAD, n_rows), :] = x
            moved = _shifted(c_scr, n_rows, sh, reverse)
        else:
            moved = pltpu.roll(x, n_rows - sh if reverse else sh, 0)
        x = x + jnp.where(mask, moved, 0.0)
    return x


def _scan_dir(q, k, v, g, s0t, reverse, scratch, accumulate, want_state):
    o_scr, k_scr, c_scr, v_scr = scratch
    n_rows, dk = q.shape
    n = n_rows // CH
    rowc = lax.broadcasted_iota(jnp.int32, (n_rows, dk), 0) & (CH - 1)
    cum = _seg_cumsum(g * LOG2_E, rowc, reverse, c_scr)
    c_scr[pl.ds(PAD, n_rows), :] = cum
    k_scr[pl.ds(PAD, n_rows), :] = k

    def rows_of(r, count):
        return jnp.broadcast_to(cum[r:r + 1, :], (count, dk))

    def chunk_row(r):
        return jnp.concatenate([rows_of(c * CH + r, CH) for c in range(n)], axis=0)

    last_row = 0 if reverse else CH - 1
    last_b = chunk_row(last_row)
    qt = (q * jnp.exp2(cum)).astype(BF16)
    kh = (k * jnp.exp2(last_b - cum)).astype(BF16)
    vb = v.astype(BF16)

    rows = rowc & (SB - 1)
    o_diag = jnp.sum(q * k, axis=-1, keepdims=True) * v
    for d in range(1, SB):
        mask = rows < SB - d if reverse else rows >= d
        kd = _shifted(k_scr, n_rows, d, reverse)
        cd = _shifted(c_scr, n_rows, d, reverse)
        vd = _shifted(v_scr, n_rows, d, reverse)
        e = jnp.exp2(jnp.where(mask, cum - cd, -jnp.inf))
        o_diag = o_diag + jnp.sum(q * kd * e, axis=-1, keepdims=True) * vd
    if accumulate:
        o_scr[...] += o_diag
    else:
        o_scr[...] = o_diag

    sub = rowc // SB
    nsb = CH // SB
    if reverse:
        own = [rows_of(c * CH + SB * min(i + 1, nsb - 1), SB) for c in range(n) for i in range(nsb)]
        has_piv = sub < nsb - 1
    else:
        own = [rows_of(c * CH + SB * max(i, 1) - 1, SB) for c in range(n) for i in range(nsb)]
        has_piv = sub >= 1
    qe = q * jnp.exp2(jnp.where(has_piv, cum - jnp.concatenate(own, axis=0), -jnp.inf))
    qs, ks = [], []
    for pr in range(SB, CH, SB):
        if reverse:
            piv, qmask, kmask = chunk_row(pr), sub == pr // SB - 1, rowc >= pr
        else:
            piv, qmask, kmask = chunk_row(pr - 1), sub == pr // SB, rowc < pr
        qs.append(jnp.where(qmask, qe, 0.0).astype(BF16))
        ks.append((k * jnp.exp2(jnp.where(kmask, piv - cum, -jnp.inf))).astype(BF16))
    qcat = jnp.concatenate(qs, axis=-1)
    kcat = jnp.concatenate(ks, axis=-1)

    trans_a = (((0,), (0,)), ((), ()))
    order = list(range(n - 1, -1, -1) if reverse else range(n))
    rows = [slice(c * CH, (c + 1) * CH) for c in order]
    n_upd = n if want_state else n - 1
    uts = [lax.dot_general(vb[sl], kh[sl], trans_a, preferred_element_type=F32) for sl in rows[:n_upd]]
    atts = [lax.dot_general(qcat[sl], kcat[sl], TRANS_B, preferred_element_type=F32).astype(BF16)
            for sl in rows]
    o_off = [jnp.dot(a, vb[sl], preferred_element_type=F32) for a, sl in zip(atts, rows)]
    st = s0t
    states = []
    for idx, c in enumerate(order):
        states.append(st.astype(BF16))
        if idx < n_upd:
            r = c * CH + last_row
            st = st * jnp.exp2(cum[r:r + 1, :]) + uts[idx]
    for sl, s_in, off in zip(rows, states, o_off):
        o_scr[sl, :] += lax.dot_general(qt[sl], s_in, TRANS_B, preferred_element_type=F32) + off
    return st


def _finish_scan(o_scr, gain_ref, zg_ref, o_ref):
    o = o_scr[...]
    ms = jnp.mean(o * o, axis=-1, keepdims=True)
    zg = zg_ref[...]
    o_ref[...] = (o * lax.rsqrt(ms + EPS) * gain_ref[...] * (zg * _sigmoid(zg))).astype(BF16)


def _split_scan_refs(refs, n_in, sample, n_alias):
    pos = n_in
    s0_ref = None
    if sample:
        s0_ref = refs[pos]
        pos += 1
    pos += n_alias
    o_ref = refs[pos]
    st_ref = None if sample else refs[pos + 1]
    scratch = refs[pos + (1 if sample else 2):]
    return refs[:n_in], s0_ref, o_ref, st_ref, scratch


def _prep_scan_scratch(scratch, v):
    _, k_scr, c_scr, v_scr = scratch
    n_rows = v.shape[0]
    for ref in (k_scr, c_scr, v_scr):
        zero = jnp.zeros((PAD, ref.shape[1]), F32)
        ref[pl.ds(0, PAD), :] = zero
        ref[pl.ds(PAD + n_rows, PAD), :] = zero
    v_scr[pl.ds(PAD, n_rows), :] = v


def _scan_scratch(seq, dk, dv):
    return [pltpu.VMEM((seq, dv), F32), pltpu.VMEM((seq + 2 * PAD, dk), F32),
            pltpu.VMEM((seq + 2 * PAD, dk), F32), pltpu.VMEM((seq + 2 * PAD, dv), F32)]


def _scan_kernel(*refs, head_fn, lane_widths, sample, hps, n_alias):
    ins, s0_ref, o_ref, st_ref, scratch = _split_scan_refs(refs, len(lane_widths), sample, n_alias)
    per_head = len(scratch) // hps
    dv = o_ref.shape[1] // hps

    def lanes(ref, width, hh):
        if width is None:
            return ref
        return ref.at[(slice(None),) * (len(ref.shape) - 1) + (pl.ds(width * hh, width),)]

    for hh in range(hps):
        head_fn([lanes(r, w, hh) for r, w in zip(ins, lane_widths)],
                None if s0_ref is None else s0_ref.at[:, hh],
                lanes(o_ref, dv, hh),
                None if st_ref is None else st_ref.at[:, hh],
                scratch[per_head * hh:per_head * (hh + 1)])


def _hgrn_head(ins, s0_ref, o_ref, st_ref, scratch, *, layer):
    lbl_ref, q_ref, ff_ref, fb_ref, i_ref, zg_ref, gain_ref = ins
    has_state, want_state = s0_ref is not None, st_ref is not None

    logits = [lbl_ref[l] for l in range(DEPTH)]
    mx = functools.reduce(jnp.maximum, logits)
    ex = [jnp.exp(x - mx) for x in logits]
    tot = functools.reduce(lambda a, b: a + b, ex)
    probs = [e / tot for e in ex]
    cum_first = probs[0]
    cum_l = functools.reduce(lambda a, b: a + b, probs[:layer + 1])
    lb = cum_l - cum_first

    q = q_ref[...]
    v = i_ref[...]
    _prep_scan_scratch(scratch, v)
    for d, (z_ref, reverse) in enumerate(((ff_ref, False), (fb_ref, True))):
        lbd = lb[d:d + 1, :]
        zf = z_ref[...]
        a = jnp.log(lbd)
        b = jnp.log1p(-lbd) + _log_sigmoid(zf)
        delta = a - b
        log_f = jnp.where(delta != delta, a + b,
                          jnp.maximum(a, b) + _log1p_of_exp_neg(jnp.abs(delta)))
        one_minus_f = (1.0 - lbd) * _sigmoid(-zf)
        if has_state:
            s0t = s0_ref[d].T
        else:
            s0t = jnp.zeros((A_DV, A_DK), F32)
        st = _scan_dir(q, one_minus_f, v, log_f, s0t, reverse, scratch, d == 1, want_state)
        if want_state:
            st_ref[d] = st.T
    _finish_scan(scratch[0], gain_ref, zg_ref, o_ref)


def _scan_call(head_fn, name, inputs, heads, dk, dv, state, layer, o_prev):
    sample = o_prev is not None
    seq, nb = (DEC_SEQ, DEC_BATCH) if sample else (SEQ, BATCH)
    hps = 1 if sample else 4
    row0 = N_PROMPT // seq if sample else 0
    in_specs, args, lane_widths = [], [], []
    for arr, kind, *par in inputs:
        args.append(arr)
        if kind == "cols":
            col0, width = par
            blk = col0 // (width * hps)
            in_specs.append(pl.BlockSpec((seq, width * hps), lambda b, h, blk=blk: (row0 + b, blk + h)))
            lane_widths.append(width)
        elif kind == "shared_cols":
            col0, width = par
            in_specs.append(pl.BlockSpec((seq, width), lambda b, h, blk=col0 // width: (row0 + b, blk)))
            lane_widths.append(None)
        elif kind == "heads_last":
            (width,) = par
            lead = arr.shape[:-1]
            in_specs.append(pl.BlockSpec(lead + (width * hps,),
                                         lambda b, h, n=len(lead): (0,) * n + (h,)))
            lane_widths.append(width)
        else:
            in_specs.append(pl.BlockSpec(arr.shape, lambda b, h, n=arr.ndim: (0,) * n))
            lane_widths.append(None)
    state_spec = pl.BlockSpec((None, None, 2, hps, dk, dv), lambda b, h: (b, layer, 0, h, 0, 0))
    aliases = {}
    if sample:
        in_specs += [state_spec, pl.BlockSpec(memory_space=pl.ANY)]
        aliases = {len(args) + 1: 0}
        args += [state, o_prev]
    elif state is not None:
        in_specs.append(pl.BlockSpec(memory_space=pl.ANY))
        aliases = {len(args): 1}
        args.append(state)
    out_specs = [pl.BlockSpec((seq, dv * hps), lambda b, h: (row0 + b, h))]
    out_shape = [jax.ShapeDtypeStruct((N_TOK, heads * dv), BF16)]
    if not sample:
        out_specs.append(state_spec)
        out_shape.append(jax.ShapeDtypeStruct((nb, DEPTH, 2, heads, dk, dv), F32))
    return pl.pallas_call(
        functools.partial(_scan_kernel, head_fn=head_fn, lane_widths=tuple(lane_widths),
                          sample=sample, hps=hps, n_alias=len(aliases)),
        grid=(nb, heads // hps),
        in_specs=in_specs,
        out_specs=out_specs,
        out_shape=out_shape,
        input_output_aliases=aliases,
        scratch_shapes=_scan_scratch(seq, dk, dv) * hps,
        compiler_params=_params(("parallel", "parallel")),
        name=name + ("_sample" if sample else "_prompt"),
    )(*args)


def _hgrn(z, lb_logits, out_gain, state, layer, o_prev=None):
    width = A_HEADS * A_DK
    inputs = [(lb_logits, "heads_last", A_DK)]
    inputs += [(z, "cols", g * width, A_DK) for g in range(5)]
    inputs += [(out_gain.reshape(1, A_DV), "whole")]
    return _scan_call(functools.partial(_hgrn_head, layer=layer), "hgrn", inputs,
                      A_HEADS, A_DK, A_DV, state, layer, o_prev)


def _gla_head(ins, s0_ref, o_ref, st_ref, scratch):
    q_ref, k_ref, v_ref, zr_ref, lr_ref, wg_ref, bg_ref, gain_ref = ins
    has_state, want_state = s0_ref is not None, st_ref is not None

    q = q_ref[...] * (B_DK ** -0.5)
    k = k_ref[...]
    v = v_ref[...]
    _prep_scan_scratch(scratch, v)
    lr = lr_ref[...].astype(BF16)
    for d, reverse in enumerate((False, True)):
        logit = jnp.dot(lr, wg_ref[d], preferred_element_type=F32) + bg_ref[d]
        g = _log_sigmoid(logit) / GLA_GATE_NORM
        if has_state:
            s0t = s0_ref[d].T
        else:
            s0t = jnp.zeros((B_DV, B_DK), F32)
        st = _scan_dir(q, k, v, g, s0t, reverse, scratch, d == 1, want_state)
        if want_state:
            st_ref[d] = st.T
    _finish_scan(scratch[0], gain_ref, zr_ref, o_ref)


def _gla(z, zlr, w_gate_ext, b_gate, out_gain, state, layer, o_prev=None):
    inputs = [(z, "cols", 5120, B_DK), (z, "cols", 5632, B_DK), (z, "cols", 6144, B_DV),
              (z, "cols", 7168, B_DV), (zlr, "shared_cols", 0, LANES),
              (w_gate_ext, "heads_last", B_DK), (b_gate.reshape(2, 1, B_HEADS * B_DK), "heads_last", B_DK),
              (out_gain.reshape(1, B_DV), "whole")]
    return _scan_call(_gla_head, "gla", inputs, B_HEADS, B_DK, B_DV, state, layer, o_prev)


def _swap_halves(t, lane):
    width = t.shape[-1]
    return jnp.where((lane & 31) < 16, pltpu.roll(t, width - 16, 1), pltpu.roll(t, 16, 1))


def _qup_kernel(x_ref, w_ref, gn_ref, gr_ref, c_ref, s_ref, qn_ref, qr_ref):
    x = x_ref[...]
    nope_w = C_HEADS * QK_NOPE
    tm = x.shape[0]
    lane = lax.broadcasted_iota(jnp.int32, (tm, LANES), 1)
    low = lane < QK_ROPE
    gn = gn_ref[...]
    gr = gr_ref[...]
    cos = c_ref[...]
    sin = s_ref[...]
    strip = lambda c0: jnp.dot(x, w_ref[:, c0:c0 + MXU_COLS], preferred_element_type=F32)
    for p in range(C_HEADS // 2):
        acc = strip(MXU_COLS * p)
        if p % 2 == 0:
            rope2 = strip(nope_w + LANES * p)
        r2 = rope2[:, LANES * (p % 2):LANES * (p % 2 + 1)]
        sq = r2 * r2
        ss_rope = (jnp.sum(jnp.where(low, sq, 0.0), axis=-1, keepdims=True),
                   jnp.sum(jnp.where(low, 0.0, sq), axis=-1, keepdims=True))
        inv = []
        for hh in range(2):
            h = 2 * p + hh
            nh = acc[:, QK_NOPE * hh: QK_NOPE * (hh + 1)]
            ssn = jnp.sum(nh * nh, axis=-1, keepdims=True)
            r = lax.rsqrt((ssn + ss_rope[hh]) / QK_DIM + EPS)
            qn_ref[:, QK_NOPE * h: QK_NOPE * (h + 1)] = (nh * r * gn).astype(BF16)
            inv.append(r)
        t = r2 * jnp.where(low, inv[0], inv[1]) * gr
        qr_ref[:, LANES * p: LANES * (p + 1)] = (t * cos + _swap_halves(t, lane) * sin).astype(BF16)


def _qup(qlat, w, gn, gr, cos, sin):
    tm = 512
    n_rows = qlat.shape[0]
    row = lambda width: pl.BlockSpec((tm, width), lambda i: (i, 0))
    full = lambda a: pl.BlockSpec(a.shape, lambda i: (0, 0))
    return pl.pallas_call(
        _qup_kernel,
        grid=(n_rows // tm,),
        in_specs=[row(Q_LORA), full(w), full(gn), full(gr), row(LANES), row(LANES)],
        out_specs=[row(C_HEADS * QK_NOPE), row(C_HEADS * QK_ROPE)],
        out_shape=[jax.ShapeDtypeStruct((n_rows, C_HEADS * QK_NOPE), BF16),
                   jax.ShapeDtypeStruct((n_rows, C_HEADS * QK_ROPE), BF16)],
        compiler_params=_params(("parallel",)),
        name="mla_q_up",
    )(qlat, w, gn, gr, cos, sin)


def _kvup_kernel(*refs, sample):
    if sample:
        cache_ref, x_ref, w_ref, gn_ref, gr_ref, c_ref, s_ref, kn_ref, kr_ref, v_ref = refs
        from_cache = pl.program_id(1) == 0
        x = jnp.where(from_cache, cache_ref[...], x_ref[...])
    else:
        x_ref, w_ref, gn_ref, gr_ref, kn_ref, kr_ref, v_ref = refs
        x = x_ref[...]
    ckv = x[:, :KV_LORA].astype(BF16)
    nope_w = C_HEADS * QK_NOPE
    tm = x.shape[0]
    lane = lax.broadcasted_iota(jnp.int32, (tm, LANES), 1)
    low = lane < QK_ROPE
    kr = x[:, KV_LORA:]
    ss_rope = jnp.sum(kr * kr, axis=-1, keepdims=True)
    rot = jnp.concatenate([kr, kr], axis=-1) * gr_ref[...]
    if sample:
        turned = rot * c_ref[...] + _swap_halves(rot, lane) * s_ref[...]
        rot = jnp.where(from_cache, rot, turned)
    gn = gn_ref[...]
    for p in range(C_HEADS // 2):
        pair = slice(2 * QK_NOPE * p, 2 * QK_NOPE * (p + 1))
        acc = jnp.dot(ckv, w_ref[:, pair], preferred_element_type=F32)
        v_ref[:, pair] = jnp.dot(ckv, w_ref[:, nope_w + pair.start:nope_w + pair.stop],
                                 preferred_element_type=F32).astype(BF16)
        inv = []
        for hh in range(2):
            h = 2 * p + hh
            nh = acc[:, QK_NOPE * hh: QK_NOPE * (hh + 1)]
            ssn = jnp.sum(nh * nh, axis=-1, keepdims=True)
            r = lax.rsqrt((ssn + ss_rope) / QK_DIM + EPS)
            kn_ref[:, QK_NOPE * h: QK_NOPE * (h + 1)] = (nh * r * gn).astype(BF16)
            inv.append(r)
        kr_ref[:, LANES * p: LANES * (p + 1)] = (rot * jnp.where(low, inv[0], inv[1])).astype(BF16)


def _kvup(ckvk, w, gn, gr, cache=None, layer=None, cos=None, sin=None):
    tm = PAST_LEN
    sample = cache is not None
    width = KV_LORA + QK_ROPE
    if sample:
        per_seq = 1 + DEC_SEQ // tm
        grid = (DEC_BATCH, per_seq)
        row_blk = lambda b, t: b * per_seq + t
        new_blk = lambda b, t: N_PROMPT // tm + b * (per_seq - 1) + jnp.maximum(t - 1, 0)
        full = lambda a: pl.BlockSpec(a.shape, lambda b, t: (0, 0))
        table = pl.BlockSpec((tm, LANES), lambda b, t: (jnp.maximum(t - 1, 0), 0))
        in_specs = [pl.BlockSpec((None, None, tm, width), lambda b, t: (b, layer, 0, 0)),
                    pl.BlockSpec((tm, width), lambda b, t: (new_blk(b, t), 0)),
                    full(w), full(gn), full(gr), table, table]
        args = (cache, ckvk, w, gn, gr, cos, sin)
        row = lambda wd: pl.BlockSpec((tm, wd), lambda b, t: (row_blk(b, t), 0))
        n_rows = DEC_BATCH * per_seq * tm
    else:
        grid = (N_PROMPT // tm,)
        full = lambda a: pl.BlockSpec(a.shape, lambda i: (0, 0))
        in_specs = [pl.BlockSpec((tm, width), lambda i: (i, 0)), full(w), full(gn), full(gr)]
        args = (ckvk, w, gn, gr)
        row = lambda wd: pl.BlockSpec((tm, wd), lambda i: (i, 0))
        n_rows = N_PROMPT
    return pl.pallas_call(
        functools.partial(_kvup_kernel, sample=sample),
        grid=grid,
        in_specs=in_specs,
        out_specs=[row(C_HEADS * QK_NOPE), row(C_HEADS * QK_ROPE), row(C_HEADS * C_DV)],
        out_shape=[jax.ShapeDtypeStruct((n_rows, C_HEADS * QK_NOPE), BF16),
                   jax.ShapeDtypeStruct((n_rows, C_HEADS * QK_ROPE), BF16),
                   jax.ShapeDtypeStruct((n_rows, C_HEADS * C_DV), BF16)],
        compiler_params=_params(("parallel",) * len(grid)),
        name="mla_kv_up_sample" if sample else "mla_kv_up_prompt",
    )(*args)


def _attn_kernel(qn_ref, qr_ref, kn_ref, kr_ref, v_ref, *rest, n_seq):
    o_ref = rest[-1]
    lq = qr_ref.shape[0] // n_seq
    lk = kr_ref.shape[0] // n_seq
    lane = lax.broadcasted_iota(jnp.int32, (lq, LANES), 1)
    blocks = [(slice(s * lq, (s + 1) * lq), slice(s * lk, (s + 1) * lk),
               slice(QK_NOPE * hh, QK_NOPE * (hh + 1)), hh)
              for s in range(n_seq) for hh in range(2)]
    scores = []
    for qrows, krows, cols, hh in blocks:
        qr = qr_ref[qrows, :]
        own = (lane < QK_ROPE) if hh == 0 else (lane >= QK_ROPE)
        qrm = jnp.where(own, qr, jnp.zeros_like(qr))
        q_cat = jnp.concatenate([qn_ref[qrows, cols], qrm], axis=-1)
        k_cat = jnp.concatenate([kn_ref[krows, cols], kr_ref[krows, :]], axis=-1)
        s = lax.dot_general(q_cat, k_cat, TRANS_B, preferred_element_type=F32)
        scores.append(s * (QK_DIM ** -0.5 * LOG2_E))
    tops = [jnp.max(s, axis=-1, keepdims=True) for s in scores]
    exps = [jnp.exp2(s - m) for s, m in zip(scores, tops)]
    sums = [jnp.sum(e, axis=-1, keepdims=True) for e in exps]
    probs = [(e / t).astype(BF16) for e, t in zip(exps, sums)]
    for p, (qrows, krows, cols, _) in zip(probs, blocks):
        o_ref[qrows, cols] = jnp.dot(p, v_ref[krows, cols], preferred_element_type=F32).astype(BF16)


def _attention(qn, qr, kn, kr, v, o_prev=None):
    sample = o_prev is not None
    if sample:
        q_row0, tq, tk, n_seq = N_PROMPT, SEQ, PAST_LEN + DEC_SEQ, 1
        grid = (DEC_BATCH, C_HEADS // 2, DEC_SEQ // tq)
    else:
        n_seq = 4
        q_row0, tq, tk = 0, n_seq * SEQ, n_seq * SEQ
        grid = (BATCH // n_seq, C_HEADS // 2, 1)
    nq = grid[2]
    qspec = lambda width: pl.BlockSpec((tq, width), lambda b, p, i: (q_row0 // tq + b * nq + i, p))
    kspec = lambda width: pl.BlockSpec((tk, width), lambda b, p, i: (b, p))
    in_specs = [qspec(2 * QK_NOPE), qspec(LANES), kspec(2 * QK_NOPE), kspec(LANES), kspec(2 * C_DV)]
    args = [qn, qr, kn, kr, v]
    if sample:
        in_specs.append(pl.BlockSpec(memory_space=pl.ANY))
        args.append(o_prev)
    return pl.pallas_call(
        functools.partial(_attn_kernel, n_seq=n_seq),
        grid=grid,
        in_specs=in_specs,
        out_specs=qspec(2 * C_DV),
        out_shape=jax.ShapeDtypeStruct((N_TOK, C_HEADS * C_DV), BF16),
        input_output_aliases={5: 0} if sample else {},
        compiler_params=_params(("parallel", "parallel", "parallel")),
        name="mla_attention_sample" if sample else "mla_attention_prompt",
    )(*args)


def _rope_tables():
    rows = DEC_SEQ // GRID_W
    row = jnp.repeat(jnp.arange(rows, dtype=F32), GRID_W)
    col = jnp.tile(jnp.arange(GRID_W, dtype=F32), rows)
    inv_freq = ROPE_BASE ** (-jnp.arange(0, ROPE_AXIS, 2, dtype=F32) / ROPE_AXIS)
    ar = row[:, None] * inv_freq
    ac = col[:, None] * inv_freq
    cos = jnp.concatenate([jnp.cos(ar), jnp.cos(ar), jnp.cos(ac), jnp.cos(ac)], axis=-1)
    sin = jnp.concatenate([-jnp.sin(ar), jnp.sin(ar), -jnp.sin(ac), jnp.sin(ac)], axis=-1)
    return jnp.tile(cos, (1, 2)), jnp.tile(sin, (1, 2))


def _layer_weights(l, gla_w_gate_up, mla_w_q_up, mla_w_kv_up, mla_q_norm, mla_k_norm):
    ext =jnp.zeros((2, LANES, B_HEADS * B_DK), F32)
    ext = ext.at[0, QK_ROPE:QK_ROPE + GLA_RANK].set(gla_w_gate_up[l, 0])
    ext = ext.at[1, QK_ROPE + GLA_RANK:QK_ROPE + 2 * GLA_RANK].set(gla_w_gate_up[l, 1])
    wq = mla_w_q_up[l].reshape(Q_LORA, C_HEADS, QK_DIM)
    wq = jnp.concatenate([wq[:, :, :QK_NOPE].reshape(Q_LORA, -1),
                          wq[:, :, QK_NOPE:].reshape(Q_LORA, -1)], axis=1).astype(BF16)
    wkv = mla_w_kv_up[l].reshape(KV_LORA, C_HEADS, QK_NOPE + C_DV)
    wkv = jnp.concatenate([wkv[:, :, :QK_NOPE].reshape(KV_LORA, -1),
                           wkv[:, :, QK_NOPE:].reshape(KV_LORA, -1)], axis=1).astype(BF16)
    split = lambda g: (g[:QK_NOPE].reshape(1, QK_NOPE), jnp.tile(g[QK_NOPE:], 2).reshape(1, LANES))
    return ext.astype(BF16), wq, wkv, split(mla_q_norm[l]), split(mla_k_norm[l])


def kernel(x_prompt, x_sample, cache_mla, state_hgrn, state_gla, c, c_ctx, norm1, w_mod, b_mod, w_in,
           hgrn_lb_logits, hgrn_out_norm, gla_w_gate_up, gla_b_gate, gla_out_norm, mla_q_a_norm,
           mla_w_q_up, mla_kv_a_norm, mla_w_kv_up, mla_q_norm, mla_k_norm, w_branch_a, w_branch_b,
           w_branch_c, w_out, norm2, w_mlp_in, w_mlp_out):
    ctx_rows, lat_rows = (0, N_PROMPT), (N_PROMPT, N_SAMPLE)
    x = (x_prompt.reshape(N_PROMPT, D_MODEL), x_sample.reshape(N_SAMPLE, D_MODEL))
    cond =jnp.concatenate([c_ctx[None], c, jnp.zeros((MOD_ROWS - 1 - DEC_BATCH, D_MODEL), F32)])
    mod = _modulation(cond, w_mod, b_mod).reshape(DEPTH * MOD_ROWS, 1, N_MOD * D_MODEL)

    cos_s, sin_s = _rope_tables()
    cos_q = jnp.concatenate([jnp.ones((N_PROMPT, LANES), F32), jnp.tile(cos_s, (DEC_BATCH, 1))])
    sin_q = jnp.concatenate([jnp.zeros((N_PROMPT, LANES), F32), jnp.tile(sin_s, (DEC_BATCH, 1))])

    wb_a, wb_b, wb_c = w_branch_a.astype(BF16), w_branch_b.astype(BF16), w_branch_c.astype(BF16)
    w_in_t = jnp.swapaxes(w_in, 1, 2)
    narrow0, gate0 = Z_MAIN, Z_MAIN + 2 * GLA_RANK + Q_LORA + KV_LORA + QK_ROPE
    caches = []
    new_hgrn = new_gla = None
    for l in range(DEPTH):
        wlr, wq, wkv, (qgn, qgr), (kgn, kgr) = _layer_weights(
            l, gla_w_gate_up, mla_w_q_up, mla_w_kv_up, mla_q_norm, mla_k_norm)
        ws = _cast_rows(w_in_t, l, narrow0, gate0 - narrow0, (gate0 - narrow0) // 3, "cast_narrow")
        ws = jnp.concatenate([ws[2 * GLA_RANK:], ws[:2 * GLA_RANK]])
        wg = _cast_rows(w_in_t, l, gate0, 3 * D_MODEL, 512, "cast_gates")

        if l == 0:
            h = _norm_mod(x[0], norm1[l], mod, l, 0)
            h = _norm_mod(x[1], norm1[l], mod, l, 0, row0=N_PROMPT, o_prev=h)
        else:
            h = _norm_mod(x, norm1[l], mod, l, 0)
        z = _matmul(h, w_in_t, F32, l, Z_MAIN, transposed=True, name="in_proj")
        qlat, ckvk, zlr = _in_small(h, ws, mla_q_a_norm[l], mla_kv_a_norm[l])

        o_a, new_hgrn = _hgrn(z, hgrn_lb_logits, hgrn_out_norm[l], new_hgrn, l)
        (o_a,) = _hgrn(z, hgrn_lb_logits, hgrn_out_norm[l], state_hgrn, l, o_prev=o_a)
        o_b, new_gla = _gla(z, zlr, wlr, gla_b_gate[l], gla_out_norm[l], new_gla, l)
        (o_b,) = _gla(z, zlr, wlr, gla_b_gate[l], gla_out_norm[l], state_gla, l, o_prev=o_b)

        caches.append(ckvk[:N_PROMPT].reshape(BATCH, SEQ, KV_LORA + QK_ROPE))
        qn, qr = _qup(qlat, wq, qgn, qgr, cos_q, sin_q)
        kn_p, kr_p, v_p = _kvup(ckvk, wkv, kgn, kgr)
        kn_s, kr_s, v_s = _kvup(ckvk, wkv, kgn, kgr, cache_mla, l, cos_s, sin_s)
        o_c = _attention(qn, qr, kn_p, kr_p, v_p)
        o_c = _attention(qn, qr, kn_s, kr_s, v_s, o_prev=o_c)

        merged = _merge(h, wg, o_a, wb_a, o_b, wb_b, o_c, wb_c, l)
        if l == 0:
            xo = _matmul_residual(merged, w_out, x[0], mod, l, 2, 0, "out_proj", rows=ctx_rows)
            x = _matmul_residual(merged, w_out, x[1], mod, l, 2, 0, "out_proj", rows=lat_rows,
                                 res_row0=0, o_prev=xo)
        else:
            x = _matmul_residual(merged, w_out, x, mod, l, 2, 0, "out_proj")

        h2 = _norm_mod(x, norm2[l], mod, l, 3)
        u = _matmul(h2, w_mlp_in, BF16, l, D_FF, relu2=True, name="mlp_in")
        n_kb = D_FF // D_MODEL
        for kb in range(n_kb):
            if l == DEPTH - 1 and kb == n_kb - 1:
                x = tuple(_matmul_residual(u, w_mlp_out, x, mod, l, 5, kb, "mlp_out", rows=rows,
                                           out_rows=rows[1], out_row0=0)
                          for rows in (ctx_rows, lat_rows))
            else:
                x = _matmul_residual(u, w_mlp_out, x, mod, l, 5, kb, "mlp_out")

    y_prompt = x[0].reshape(BATCH, SEQ, D_MODEL)
    y_sample = x[1].reshape(DEC_BATCH, DEC_SEQ, D_MODEL)
    return (y_prompt, y_sample, jnp.stack(caches, axis=1), new_hgrn, new_gla)
```

```python
import functools

import jax
import jax.numpy as jnp
from jax import lax
from jax.experimental import pallas as pl
from jax.experimental.pallas import tpu as pltpu

F32 = jnp.float32
BF16 = jnp.bfloat16

D_MODEL = 4096
BATCH = 32
SEQ = 256
DEPTH = 2
DEC_BATCH = 2
DEC_SEQ = 1024
PAST_LEN = 512
GRID_W = 64
A_HEADS = 8
A_DK = 128
A_DV = 128
B_HEADS = 4
B_DK = 128
B_DV = 256
GLA_RANK = 16
GLA_GATE_NORM = 16.0
C_HEADS = 16
Q_LORA = 1024
KV_LORA = 512
QK_NOPE = 128
QK_ROPE = 64
C_DV = 128
QK_DIM = QK_NOPE + QK_ROPE
ROPE_AXIS = QK_ROPE // 2
ROPE_BASE = 10000.0
N_MOD = 6
D_FF = 4 * D_MODEL
EPS = 1e-6

N_PROMPT = BATCH * SEQ
N_SAMPLE = DEC_BATCH * DEC_SEQ
N_TOK = N_PROMPT + N_SAMPLE
MOD_ROWS = 8

Z_MAIN = 8192
ROW_ALIGN = 32

LANES = 128
MXU_COLS = 256
V7X_VMEM_BYTES = 64 * 1024 * 1024
MIB = 1024 * 1024
VMEM_BIG = V7X_VMEM_BYTES - 4 * MIB
VMEM_MID = 40 * MIB

LOG2_E = 1.4426950408889634
CH = 32
SB = 8


def _params(sem, vmem=VMEM_MID):
    return pltpu.CompilerParams(dimension_semantics=sem, vmem_limit_bytes=vmem)


def _mod_row(i, tm):
    return jnp.where(i < N_PROMPT // tm, 0, 1 + (i - N_PROMPT // tm) // (DEC_SEQ // tm))


def _sigmoid(x):
    return 1.0 / (1.0 + jnp.exp(-x))


def _log1p_of_exp_neg(t):
    return jnp.log(1.0 + jnp.exp(-t))


def _log_sigmoid(x):
    return jnp.minimum(x, 0.0) - _log1p_of_exp_neg(jnp.abs(x))


def _mod_kernel(cond_ref, w_ref, b_ref, o_ref):
    c = cond_ref[...]
    s = (c * _sigmoid(c)).astype(BF16)
    o_ref[...] = jnp.dot(s, w_ref[...].astype(BF16), preferred_element_type=F32) + b_ref[...]


def _modulation(cond, w_mod, b_mod):
    tn = 512
    n = N_MOD * D_MODEL
    return pl.pallas_call(
        _mod_kernel,
        grid=(DEPTH, n // tn),
        in_specs=[
            pl.BlockSpec((MOD_ROWS, D_MODEL), lambda l, j: (0, 0)),
            pl.BlockSpec((None, D_MODEL, tn), lambda l, j: (l, 0, j)),
            pl.BlockSpec((None, 1, tn), lambda l, j: (l, 0, j)),
        ],
        out_specs=pl.BlockSpec((None, MOD_ROWS, tn), lambda l, j: (l, 0, j)),
        out_shape=jax.ShapeDtypeStruct((DEPTH, MOD_ROWS, n), F32),
        compiler_params=_params(("parallel", "parallel")),
        name="modulation",
    )(cond, w_mod, b_mod.reshape(DEPTH, 1, n))


def _norm_mod_kernel(x_ref, g_ref, sh_ref, sc_ref, *rest):
    o_ref = rest[-1]
    x = x_ref[...]
    ms = jnp.mean(x * x, axis=-1, keepdims=True)
    y = x * lax.rsqrt(ms + EPS) * g_ref[...]
    o_ref[...] = (y * (1.0 + sc_ref[...]) + sh_ref[...]).astype(BF16)


def _norm_mod(x, gain, mod, layer, shift_chunk, row0=0, o_prev=None):
    tm = 512
    base = layer * MOD_ROWS
    blk0 = row0 // tm
    mod_spec = lambda chunk: pl.BlockSpec(
        (None, 1, D_MODEL), lambda i: (base + _mod_row(blk0 + i, tm), 0, chunk))
    in_specs = [pl.BlockSpec((tm, D_MODEL), lambda i: (i, 0)),
                pl.BlockSpec((1, D_MODEL), lambda i: (0, 0)),
                mod_spec(shift_chunk), mod_spec(shift_chunk + 1)]
    args = [x, gain.reshape(1, D_MODEL), mod, mod]
    if o_prev is not None:
        in_specs.append(pl.BlockSpec(memory_space=pl.ANY))
        args.append(o_prev)
    return pl.pallas_call(
        _norm_mod_kernel,
        grid=(x.shape[0] // tm,),
        in_specs=in_specs,
        out_specs=pl.BlockSpec((tm, D_MODEL), lambda i: (blk0 + i, 0)),
        out_shape=jax.ShapeDtypeStruct((N_TOK, D_MODEL), BF16),
        input_output_aliases={} if o_prev is None else {4: 0},
        compiler_params=_params(("parallel",)),
        name="norm_mod",
    )(*args)


def _cast_weight(w_ref, wb_ref):
    @pl.when(pl.program_id(1) == 0)
    def _():
        wb_ref[...] = w_ref[...].astype(BF16)
    return wb_ref


TRANS_B = (((1,), (1,)), ((), ()))


def _mm_kernel(x_ref, w_ref, o_ref, wb_ref, *, relu2, transposed):
    w_ref = _cast_weight(w_ref, wb_ref)
    if transposed:
        acc = lax.dot_general(x_ref[...], w_ref[...], TRANS_B, preferred_element_type=F32)
    else:
        acc = jnp.dot(x_ref[...], w_ref[...], preferred_element_type=F32)
    if relu2:
        acc = jnp.square(jnp.maximum(acc, 0.0))
    o_ref[...] = acc.astype(o_ref.dtype)


def _matmul(x, w, out_dtype, layer, n_cols, relu2=False, transposed=False, name="matmul"):
    tm, tn = 512, 1024
    m, k = x.shape
    if transposed:
        w_spec = pl.BlockSpec((None, tn, k), lambda j, i: (layer, j, 0))
    else:
        w_spec = pl.BlockSpec((None, k, tn), lambda j, i: (layer, 0, j))
    return pl.pallas_call(
        functools.partial(_mm_kernel, relu2=relu2, transposed=transposed),
        grid=(n_cols // tn, m // tm),
        in_specs=[pl.BlockSpec((tm, k), lambda j, i: (i, 0)), w_spec],
        out_specs=pl.BlockSpec((tm, tn), lambda j, i: (i, j)),
        out_shape=jax.ShapeDtypeStruct((m, n_cols), out_dtype),
        scratch_shapes=[pltpu.VMEM((tn, k) if transposed else (k, tn), BF16)],
        compiler_params=_params(("parallel", "arbitrary"), VMEM_BIG),
        name=name,
    )(x, w)


def _cast_rows_kernel(w_ref, o_ref):
    o_ref[...] = w_ref[0].astype(BF16)


def _cast_rows(w, layer, row0, n_rows, block_rows, name):
    k = w.shape[2]
    assert row0 % ROW_ALIGN == 0 and block_rows % ROW_ALIGN == 0 and n_rows % block_rows == 0
    start = lambda r: pl.multiple_of(ROW_ALIGN * (row0 // ROW_ALIGN + (block_rows // ROW_ALIGN) * r),
                                     ROW_ALIGN)
    return pl.pallas_call(
        _cast_rows_kernel,
        grid=(n_rows // block_rows,),
        in_specs=[pl.BlockSpec((pl.Element(1), pl.Element(block_rows), pl.Element(k)),
                               lambda r: (layer, start(r), 0))],
        out_specs=pl.BlockSpec((block_rows, k), lambda r: (r, 0)),
        out_shape=jax.ShapeDtypeStruct((n_rows, k), BF16),
        compiler_params=_params(("parallel",)),
        name=name,
    )(w)


def _in_small_kernel(x_ref, w_ref, gq_ref, gkv_ref, qlat_ref, ckvk_ref, lr_ref):
    x = x_ref[...]
    tm = x.shape[0]
    n_kv, n_lr = KV_LORA + QK_ROPE, 2 * GLA_RANK
    cq = lax.dot_general(x, w_ref[:Q_LORA, :], TRANS_B, preferred_element_type=F32)
    rest = lax.dot_general(x, w_ref[Q_LORA:, :], TRANS_B, preferred_element_type=F32)
    qlat_ref[...] = (cq * lax.rsqrt(jnp.mean(cq * cq, axis=-1, keepdims=True) + EPS)
                     * gq_ref[...]).astype(BF16)
    ckv = rest[:, :KV_LORA]
    ckvk_ref[:, :KV_LORA] = (ckv * lax.rsqrt(jnp.mean(ckv * ckv, axis=-1, keepdims=True) + EPS)
                             * gkv_ref[...])
    ckvk_ref[:, KV_LORA:] = rest[:, KV_LORA:n_kv]
    lr_ref[:, :QK_ROPE + n_lr] = rest[:, KV_LORA:]
    lr_ref[:, QK_ROPE + n_lr:] = jnp.zeros((tm, LANES - QK_ROPE - n_lr), F32)


def _in_small(h, w_small, q_gain, kv_gain):
    tm = 512
    n_kv = KV_LORA + QK_ROPE
    row = lambda width: pl.BlockSpec((tm, width), lambda i: (i, 0))
    full = lambda a: pl.BlockSpec(a.shape, lambda i: (0, 0))
    q_gain, kv_gain = q_gain.reshape(1, Q_LORA), kv_gain.reshape(1, KV_LORA)
    return pl.pallas_call(
        _in_small_kernel,
        grid=(N_TOK // tm,),
        in_specs=[row(D_MODEL), full(w_small), full(q_gain), full(kv_gain)],
        out_specs=[row(Q_LORA), row(n_kv), row(LANES)],
        out_shape=[jax.ShapeDtypeStruct((N_TOK, Q_LORA), BF16),
                   jax.ShapeDtypeStruct((N_TOK, n_kv), F32),
                   jax.ShapeDtypeStruct((N_TOK, LANES), F32)],
        compiler_params=_params(("parallel",), VMEM_BIG),
        name="in_proj_small",
    )(h, w_small, q_gain, kv_gain)


def _mm_res_kernel(x_ref, w_ref, res_ref, g_ref, *rest):
    o_ref, wb_ref = rest[-2:]
    w_ref = _cast_weight(w_ref, wb_ref)
    x = x_ref[...]
    for c0 in range(0, o_ref.shape[1], MXU_COLS):
        cols = slice(c0, c0 + MXU_COLS)
        part = jnp.dot(x, w_ref[:, cols], preferred_element_type=F32)
        o_ref[:, cols] = res_ref[:, cols] + g_ref[:, cols] * part


def _matmul_residual(a, w, res, mod, layer, gate_chunk, k_block, name, rows=(0, N_TOK),
                     res_row0=None, out_rows=N_TOK, out_row0=None, o_prev=None):
    tm, tn, tk = 512, 1024, D_MODEL
    n = w.shape[2]
    base = layer * MOD_ROWS
    cpc = D_MODEL // tn
    blk0 = rows[0] // tm
    res_blk0 = blk0 if res_row0 is None else res_row0 // tm
    out_blk0 = blk0 if out_row0 is None else out_row0 // tm
    in_specs = [
        pl.BlockSpec((tm, tk), lambda j, i: (blk0 + i, k_block)),
        pl.BlockSpec((None, tk, tn), lambda j, i: (layer, k_block, j)),
        pl.BlockSpec((tm, tn), lambda j, i: (res_blk0 + i, j)),
        pl.BlockSpec((None, 1, tn),
                     lambda j, i: (base + _mod_row(blk0 + i, tm), 0, gate_chunk * cpc + j)),
    ]
    args = [a, w, res, mod]
    if o_prev is not None:
        in_specs.append(pl.BlockSpec(memory_space=pl.ANY))
        args.append(o_prev)
    return pl.pallas_call(
        _mm_res_kernel,
        grid=(n // tn, rows[1] // tm),
        in_specs=in_specs,
        out_specs=pl.BlockSpec((tm, tn), lambda j, i: (out_blk0 + i, j)),
        out_shape=jax.ShapeDtypeStruct((out_rows, n), F32),
        input_output_aliases={} if o_prev is None else {4: 0},
        scratch_shapes=[pltpu.VMEM((tk, tn), BF16)],
        compiler_params=_params(("parallel", "arbitrary"), VMEM_BIG),
        name=name,
    )(*args)


def _merge_kernel(h_ref, wga_ref, wgb_ref, wgc_ref, oa_ref, wa_ref, ob_ref, wb_ref, oc_ref, wc_ref,
                  o_ref):
    h = h_ref[...]
    branches = ((wga_ref, oa_ref[...], wa_ref), (wgb_ref, ob_ref[...], wb_ref),
                (wgc_ref, oc_ref[...], wc_ref))
    for c0 in range(0, o_ref.shape[1], MXU_COLS):
        cols = slice(c0, c0 + MXU_COLS)
        acc = None
        for wg_ref, o_b, w_ref in branches:
            g = lax.dot_general(h, wg_ref[cols, :], TRANS_B, preferred_element_type=F32)
            p = jnp.dot(o_b, w_ref[:, cols], preferred_element_type=F32)
            term = _sigmoid(g) * p
            acc = term if acc is None else acc + term
        o_ref[:, cols] = acc.astype(BF16)


def _merge(h, wg, o_a, w_a, o_b, w_b, o_c, w_c, layer):
    tm, tn = 512, 512
    nj = D_MODEL // tn
    act = lambda width: pl.BlockSpec((tm, width), lambda j, i: (i, 0))
    wcol = lambda rows: pl.BlockSpec((None, rows, tn), lambda j, i: (layer, 0, j))
    gate = lambda b: pl.BlockSpec((tn, D_MODEL), lambda j, i: (b * nj + j, 0))
    return pl.pallas_call(
        _merge_kernel,
        grid=(nj, N_TOK // tm),
        in_specs=[act(D_MODEL), gate(0), gate(1), gate(2),
                  act(o_a.shape[1]), wcol(w_a.shape[1]),
                  act(o_b.shape[1]), wcol(w_b.shape[1]),
                  act(o_c.shape[1]), wcol(w_c.shape[1])],
        out_specs=pl.BlockSpec((tm, tn), lambda j, i: (i, j)),
        out_shape=jax.ShapeDtypeStruct((N_TOK, D_MODEL), BF16),
        compiler_params=_params(("parallel", "parallel"), VMEM_BIG),
        name="merge",
    )(h, wg, wg, wg, o_a, w_a, o_b, w_b, o_c, w_c)


PAD = 8


def _shifted(ref, n_rows, d, reverse):
    return ref[pl.ds(PAD + d if reverse else PAD - d, n_rows), :]


def _seg_cumsum(g, rowc, reverse, c_scr):
    n_rows = g.shape[0]
    x = g
    for sh in (1, 2, 4, 8, 16):
        mask = rowc < CH - sh if reverse else rowc >= sh
        if sh < PAD:
            c_scr[pl.ds(PAD, n_rows), :] = x
            moved = _shifted(c_scr, n_rows, sh, reverse)
        else:
            moved = pltpu.roll(x, n_rows - sh if reverse else sh, 0)
        x = x + jnp.where(mask, moved, 0.0)
    return x


def _scan_dir(q, k, v, g, s0t, reverse, scratch, accumulate, want_state):
    o_scr, k_scr, c_scr, v_scr = scratch
    n_rows, dk = q.shape
    n = n_rows // CH
    rowc = lax.broadcasted_iota(jnp.int32, (n_rows, dk), 0) & (CH - 1)
    cum = _seg_cumsum(g * LOG2_E, rowc, reverse, c_scr)
    c_scr[pl.ds(PAD, n_rows), :] = cum
    k_scr[pl.ds(PAD, n_rows), :] = k

    def rows_of(r, count):
        return jnp.broadcast_to(cum[r:r + 1, :], (count, dk))

    def chunk_row(r):
        return jnp.concatenate([rows_of(c * CH + r, CH) for c in range(n)], axis=0)

    last_row = 0 if reverse else CH - 1
    last_b = chunk_row(last_row)
    qt = (q * jnp.exp2(cum)).astype(BF16)
    kh = (k * jnp.exp2(last_b - cum)).astype(BF16)
    vb = v.astype(BF16)

    rows = rowc & (SB - 1)
    o_diag = jnp.sum(q * k, axis=-1, keepdims=True) * v
    for d in range(1, SB):
        mask = rows < SB - d if reverse else rows >= d
        kd = _shifted(k_scr, n_rows, d, reverse)
        cd = _shifted(c_scr, n_rows, d, reverse)
        vd = _shifted(v_scr, n_rows, d, reverse)
        e = jnp.exp2(jnp.where(mask, cum - cd, -jnp.inf))
        o_diag = o_diag + jnp.sum(q * kd * e, axis=-1, keepdims=True) * vd
    if accumulate:
        o_scr[...] += o_diag
    else:
        o_scr[...] = o_diag

    sub = rowc // SB
    nsb = CH // SB
    if reverse:
        own = [rows_of(c * CH + SB * min(i + 1, nsb - 1), SB) for c in range(n) for i in range(nsb)]
        has_piv = sub < nsb - 1
    else:
        own = [rows_of(c * CH + SB * max(i, 1) - 1, SB) for c in range(n) for i in range(nsb)]
        has_piv = sub >= 1
    qe = q * jnp.exp2(jnp.where(has_piv, cum - jnp.concatenate(own, axis=0), -jnp.inf))
    qs, ks = [], []
    for pr in range(SB, CH, SB):
        if reverse:
            piv, qmask, kmask = chunk_row(pr), sub == pr // SB - 1, rowc >= pr
        else:
            piv, qmask, kmask = chunk_row(pr - 1), sub == pr // SB, rowc < pr
        qs.append(jnp.where(qmask, qe, 0.0).astype(BF16))
        ks.append((k * jnp.exp2(jnp.where(kmask, piv - cum, -jnp.inf))).astype(BF16))
    qcat = jnp.concatenate(qs, axis=-1)
    kcat = jnp.concatenate(ks, axis=-1)

    trans_a = (((0,), (0,)), ((), ()))
    order = list(range(n - 1, -1, -1) if reverse else range(n))
    rows = [slice(c * CH, (c + 1) * CH) for c in order]
    n_upd = n if want_state else n - 1
    uts = [lax.dot_general(vb[sl], kh[sl], trans_a, preferred_element_type=F32) for sl in rows[:n_upd]]
    atts = [lax.dot_general(qcat[sl], kcat[sl], TRANS_B, preferred_element_type=F32).astype(BF16)
            for sl in rows]
    o_off = [jnp.dot(a, vb[sl], preferred_element_type=F32) for a, sl in zip(atts, rows)]
    st = s0t
    states = []
    for idx, c in enumerate(order):
        states.append(st.astype(BF16))
        if idx < n_upd:
            r = c * CH + last_row
            st = st * jnp.exp2(cum[r:r + 1, :]) + uts[idx]
    for sl, s_in, off in zip(rows, states, o_off):
        o_scr[sl, :] += lax.dot_general(qt[sl], s_in, TRANS_B, preferred_element_type=F32) + off
    return st


def _finish_scan(o_scr, gain_ref, zg_ref, o_ref):
    o = o_scr[...]
    ms = jnp.mean(o * o, axis=-1, keepdims=True)
    zg = zg_ref[...]
    o_ref[...] = (o * lax.rsqrt(ms + EPS) * gain_ref[...] * (zg * _sigmoid(zg))).astype(BF16)


def _split_scan_refs(refs, n_in, sample, n_alias):
    pos = n_in
    s0_ref = None
    if sample:
        s0_ref = refs[pos]
        pos += 1
    pos += n_alias
    o_ref = refs[pos]
    st_ref = None if sample else refs[pos + 1]
    scratch = refs[pos + (1 if sample else 2):]
    return refs[:n_in], s0_ref, o_ref, st_ref, scratch


def _prep_scan_scratch(scratch, v):
    _, k_scr, c_scr, v_scr = scratch
    n_rows = v.shape[0]
    for ref in (k_scr, c_scr, v_scr):
        zero = jnp.zeros((PAD, ref.shape[1]), F32)
        ref[pl.ds(0, PAD), :] = zero
        ref[pl.ds(PAD + n_rows, PAD), :] = zero
    v_scr[pl.ds(PAD, n_rows), :] = v


def _scan_scratch(seq, dk, dv):
    return [pltpu.VMEM((seq, dv), F32), pltpu.VMEM((seq + 2 * PAD, dk), F32),
            pltpu.VMEM((seq + 2 * PAD, dk), F32), pltpu.VMEM((seq + 2 * PAD, dv), F32)]


def _scan_kernel(*refs, head_fn, lane_widths, sample, hps, n_alias):
    ins, s0_ref, o_ref, st_ref, scratch = _split_scan_refs(refs, len(lane_widths), sample, n_alias)
    per_head = len(scratch) // hps
    dv = o_ref.shape[1] // hps

    def lanes(ref, width, hh):
        if width is None:
            return ref
        return ref.at[(slice(None),) * (len(ref.shape) - 1) + (pl.ds(width * hh, width),)]

    for hh in range(hps):
        head_fn([lanes(r, w, hh) for r, w in zip(ins, lane_widths)],
                None if s0_ref is None else s0_ref.at[:, hh],
                lanes(o_ref, dv, hh),
                None if st_ref is None else st_ref.at[:, hh],
                scratch[per_head * hh:per_head * (hh + 1)])


def _hgrn_head(ins, s0_ref, o_ref, st_ref, scratch, *, layer):
    lbl_ref, q_ref, ff_ref, fb_ref, i_ref, zg_ref, gain_ref = ins
    has_state, want_state = s0_ref is not None, st_ref is not None

    logits = [lbl_ref[l] for l in range(DEPTH)]
    mx = functools.reduce(jnp.maximum, logits)
    ex = [jnp.exp(x - mx) for x in logits]
    tot = functools.reduce(lambda a, b: a + b, ex)
    probs = [e / tot for e in ex]
    cum_first = probs[0]
    cum_l = functools.reduce(lambda a, b: a + b, probs[:layer + 1])
    lb = cum_l - cum_first

    q = q_ref[...]
    v = i_ref[...]
    _prep_scan_scratch(scratch, v)
    for d, (z_ref, reverse) in enumerate(((ff_ref, False), (fb_ref, True))):
        lbd = lb[d:d + 1, :]
        zf = z_ref[...]
        a = jnp.log(lbd)
        b = jnp.log1p(-lbd) + _log_sigmoid(zf)
        delta = a - b
        log_f = jnp.where(delta != delta, a + b,
                          jnp.maximum(a, b) + _log1p_of_exp_neg(jnp.abs(delta)))
        one_minus_f = (1.0 - lbd) * _sigmoid(-zf)
        if has_state:
            s0t = s0_ref[d].T
        else:
            s0t = jnp.zeros((A_DV, A_DK), F32)
        st = _scan_dir(q, one_minus_f, v, log_f, s0t, reverse, scratch, d == 1, want_state)
        if want_state:
            st_ref[d] = st.T
    _finish_scan(scratch[0], gain_ref, zg_ref, o_ref)


def _scan_call(head_fn, name, inputs, heads, dk, dv, state, layer, o_prev):
    sample = o_prev is not None
    seq, nb = (DEC_SEQ, DEC_BATCH) if sample else (SEQ, BATCH)
    hps = 1 if sample else 4
    row0 = N_PROMPT // seq if sample else 0
    in_specs, args, lane_widths = [], [], []
    for arr, kind, *par in inputs:
        args.append(arr)
        if kind == "cols":
            col0, width = par
            blk = col0 // (width * hps)
            in_specs.append(pl.BlockSpec((seq, width * hps), lambda b, h, blk=blk: (row0 + b, blk + h)))
            lane_widths.append(width)
        elif kind == "shared_cols":
            col0, width = par
            in_specs.append(pl.BlockSpec((seq, width), lambda b, h, blk=col0 // width: (row0 + b, blk)))
            lane_widths.append(None)
        elif kind == "heads_last":
            (width,) = par
            lead = arr.shape[:-1]
            in_specs.append(pl.BlockSpec(lead + (width * hps,),
                                         lambda b, h, n=len(lead): (0,) * n + (h,)))
            lane_widths.append(width)
        else:
            in_specs.append(pl.BlockSpec(arr.shape, lambda b, h, n=arr.ndim: (0,) * n))
            lane_widths.append(None)
    state_spec = pl.BlockSpec((None, None, 2, hps, dk, dv), lambda b, h: (b, layer, 0, h, 0, 0))
    aliases = {}
    if sample:
        in_specs += [state_spec, pl.BlockSpec(memory_space=pl.ANY)]
        aliases = {len(args) + 1: 0}
        args += [state, o_prev]
    elif state is not None:
        in_specs.append(pl.BlockSpec(memory_space=pl.ANY))
        aliases = {len(args): 1}
        args.append(state)
    out_specs = [pl.BlockSpec((seq, dv * hps), lambda b, h: (row0 + b, h))]
    out_shape = [jax.ShapeDtypeStruct((N_TOK, heads * dv), BF16)]
    if not sample:
        out_specs.append(state_spec)
        out_shape.append(jax.ShapeDtypeStruct((nb, DEPTH, 2, heads, dk, dv), F32))
    return pl.pallas_call(
        functools.partial(_scan_kernel, head_fn=head_fn, lane_widths=tuple(lane_widths),
                          sample=sample, hps=hps, n_alias=len(aliases)),
        grid=(nb, heads // hps),
        in_specs=in_specs,
        out_specs=out_specs,
        out_shape=out_shape,
        input_output_aliases=aliases,
        scratch_shapes=_scan_scratch(seq, dk, dv) * hps,
        compiler_params=_params(("parallel", "parallel")),
        name=name + ("_sample" if sample else "_prompt"),
    )(*args)


def _hgrn(z, lb_logits, out_gain, state, layer, o_prev=None):
    width = A_HEADS * A_DK
    inputs = [(lb_logits, "heads_last", A_DK)]
    inputs += [(z, "cols", g * width, A_DK) for g in range(5)]
    inputs += [(out_gain.reshape(1, A_DV), "whole")]
    return _scan_call(functools.partial(_hgrn_head, layer=layer), "hgrn", inputs,
                      A_HEADS, A_DK, A_DV, state, layer, o_prev)


def _gla_head(ins, s0_ref, o_ref, st_ref, scratch):
    q_ref, k_ref, v_ref, zr_ref, lr_ref, wg_ref, bg_ref, gain_ref = ins
    has_state, want_state = s0_ref is not None, st_ref is not None

    q = q_ref[...] * (B_DK ** -0.5)
    k = k_ref[...]
    v = v_ref[...]
    _prep_scan_scratch(scratch, v)
    lr = lr_ref[...].astype(BF16)
    for d, reverse in enumerate((False, True)):
        logit = jnp.dot(lr, wg_ref[d], preferred_element_type=F32) + bg_ref[d]
        g = _log_sigmoid(logit) / GLA_GATE_NORM
        if has_state:
            s0t = s0_ref[d].T
        else:
            s0t = jnp.zeros((B_DV, B_DK), F32)
        st = _scan_dir(q, k, v, g, s0t, reverse, scratch, d == 1, want_state)
        if want_state:
            st_ref[d] = st.T
    _finish_scan(scratch[0], gain_ref, zr_ref, o_ref)


def _gla(z, zlr, w_gate_ext, b_gate, out_gain, state, layer, o_prev=None):
    inputs = [(z, "cols", 5120, B_DK), (z, "cols", 5632, B_DK), (z, "cols", 6144, B_DV),
              (z, "cols", 7168, B_DV), (zlr, "shared_cols", 0, LANES),
              (w_gate_ext, "heads_last", B_DK), (b_gate.reshape(2, 1, B_HEADS * B_DK), "heads_last", B_DK),
              (out_gain.reshape(1, B_DV), "whole")]
    return _scan_call(_gla_head, "gla", inputs, B_HEADS, B_DK, B_DV, state, layer, o_prev)


def _swap_halves(t, lane):
    width = t.shape[-1]
    return jnp.where((lane & 31) < 16, pltpu.roll(t, width - 16, 1), pltpu.roll(t, 16, 1))


def _qup_kernel(x_ref, w_ref, gn_ref, gr_ref, c_ref, s_ref, qn_ref, qr_ref):
    x = x_ref[...]
    nope_w = C_HEADS * QK_NOPE
    tm = x.shape[0]
    lane = lax.broadcasted_iota(jnp.int32, (tm, LANES), 1)
    low = lane < QK_ROPE
    gn = gn_ref[...]
    gr = gr_ref[...]
    cos = c_ref[...]
    sin = s_ref[...]
    strip = lambda c0: jnp.dot(x, w_ref[:, c0:c0 + MXU_COLS], preferred_element_type=F32)
    for p in range(C_HEADS // 2):
        acc = strip(MXU_COLS * p)
        if p % 2 == 0:
            rope2 = strip(nope_w + LANES * p)
        r2 = rope2[:, LANES * (p % 2):LANES * (p % 2 + 1)]
        sq = r2 * r2
        ss_rope = (jnp.sum(jnp.where(low, sq, 0.0), axis=-1, keepdims=True),
                   jnp.sum(jnp.where(low, 0.0, sq), axis=-1, keepdims=True))
        inv = []
        for hh in range(2):
            h = 2 * p + hh
            nh = acc[:, QK_NOPE * hh: QK_NOPE * (hh + 1)]
            ssn = jnp.sum(nh * nh, axis=-1, keepdims=True)
            r = lax.rsqrt((ssn + ss_rope[hh]) / QK_DIM + EPS)
            qn_ref[:, QK_NOPE * h: QK_NOPE * (h + 1)] = (nh * r * gn).astype(BF16)
            inv.append(r)
        t = r2 * jnp.where(low, inv[0], inv[1]) * gr
        qr_ref[:, LANES * p: LANES * (p + 1)] = (t * cos + _swap_halves(t, lane) * sin).astype(BF16)


def _qup(qlat, w, gn, gr, cos, sin):
    tm = 512
    n_rows = qlat.shape[0]
    row = lambda width: pl.BlockSpec((tm, width), lambda i: (i, 0))
    full = lambda a: pl.BlockSpec(a.shape, lambda i: (0, 0))
    return pl.pallas_call(
        _qup_kernel,
        grid=(n_rows // tm,),
        in_specs=[row(Q_LORA), full(w), full(gn), full(gr), row(LANES), row(LANES)],
        out_specs=[row(C_HEADS * QK_NOPE), row(C_HEADS * QK_ROPE)],
        out_shape=[jax.ShapeDtypeStruct((n_rows, C_HEADS * QK_NOPE), BF16),
                   jax.ShapeDtypeStruct((n_rows, C_HEADS * QK_ROPE), BF16)],
        compiler_params=_params(("parallel",)),
        name="mla_q_up",
    )(qlat, w, gn, gr, cos, sin)


def _kvup_kernel(*refs, sample):
    if sample:
        cache_ref, x_ref, w_ref, gn_ref, gr_ref, c_ref, s_ref, kn_ref, kr_ref, v_ref = refs
        from_cache = pl.program_id(1) == 0
        x = jnp.where(from_cache, cache_ref[...], x_ref[...])
    else:
        x_ref, w_ref, gn_ref, gr_ref, kn_ref, kr_ref, v_ref = refs
        x = x_ref[...]
    ckv = x[:, :KV_LORA].astype(BF16)
    nope_w = C_HEADS * QK_NOPE
    tm = x.shape[0]
    lane = lax.broadcasted_iota(jnp.int32, (tm, LANES), 1)
    low = lane < QK_ROPE
    kr = x[:, KV_LORA:]
    ss_rope = jnp.sum(kr * kr, axis=-1, keepdims=True)
    rot = jnp.concatenate([kr, kr], axis=-1) * gr_ref[...]
    if sample:
        turned = rot * c_ref[...] + _swap_halves(rot, lane) * s_ref[...]
        rot = jnp.where(from_cache, rot, turned)
    gn = gn_ref[...]
    for p in range(C_HEADS // 2):
        pair = slice(2 * QK_NOPE * p, 2 * QK_NOPE * (p + 1))
        acc = jnp.dot(ckv, w_ref[:, pair], preferred_element_type=F32)
        v_ref[:, pair] = jnp.dot(ckv, w_ref[:, nope_w + pair.start:nope_w + pair.stop],
                                 preferred_element_type=F32).astype(BF16)
        inv = []
        for hh in range(2):
            h = 2 * p + hh
            nh = acc[:, QK_NOPE * hh: QK_NOPE * (hh + 1)]
            ssn = jnp.sum(nh * nh, axis=-1, keepdims=True)
            r = lax.rsqrt((ssn + ss_rope) / QK_DIM + EPS)
            kn_ref[:, QK_NOPE * h: QK_NOPE * (h + 1)] = (nh * r * gn).astype(BF16)
            inv.append(r)
        kr_ref[:, LANES * p: LANES * (p + 1)] = (rot * jnp.where(low, inv[0], inv[1])).astype(BF16)


def _kvup(ckvk, w, gn, gr, cache=None, layer=None, cos=None, sin=None):
    tm = PAST_LEN
    sample = cache is not None
    width = KV_LORA + QK_ROPE
    if sample:
        per_seq = 1 + DEC_SEQ // tm
        grid = (DEC_BATCH, per_seq)
        row_blk = lambda b, t: b * per_seq + t
        new_blk = lambda b, t: N_PROMPT // tm + b * (per_seq - 1) + jnp.maximum(t - 1, 0)
        full = lambda a: pl.BlockSpec(a.shape, lambda b, t: (0, 0))
        table = pl.BlockSpec((tm, LANES), lambda b, t: (jnp.maximum(t - 1, 0), 0))
        in_specs = [pl.BlockSpec((None, None, tm, width), lambda b, t: (b, layer, 0, 0)),
                    pl.BlockSpec((tm, width), lambda b, t: (new_blk(b, t), 0)),
                    full(w), full(gn), full(gr), table, table]
        args = (cache, ckvk, w, gn, gr, cos, sin)
        row = lambda wd: pl.BlockSpec((tm, wd), lambda b, t: (row_blk(b, t), 0))
        n_rows = DEC_BATCH * per_seq * tm
    else:
        grid = (N_PROMPT // tm,)
        full = lambda a: pl.BlockSpec(a.shape, lambda i: (0, 0))
        in_specs = [pl.BlockSpec((tm, width), lambda i: (i, 0)), full(w), full(gn), full(gr)]
        args = (ckvk, w, gn, gr)
        row = lambda wd: pl.BlockSpec((tm, wd), lambda i: (i, 0))
        n_rows = N_PROMPT
    return pl.pallas_call(
        functools.partial(_kvup_kernel, sample=sample),
        grid=grid,
        in_specs=in_specs,
        out_specs=[row(C_HEADS * QK_NOPE), row(C_HEADS * QK_ROPE), row(C_HEADS * C_DV)],
        out_shape=[jax.ShapeDtypeStruct((n_rows, C_HEADS * QK_NOPE), BF16),
                   jax.ShapeDtypeStruct((n_rows, C_HEADS * QK_ROPE), BF16),
                   jax.ShapeDtypeStruct((n_rows, C_HEADS * C_DV), BF16)],
        compiler_params=_params(("parallel",) * len(grid)),
        name="mla_kv_up_sample" if sample else "mla_kv_up_prompt",
    )(*args)


def _attn_kernel(qn_ref, qr_ref, kn_ref, kr_ref, v_ref, *rest, n_seq):
    o_ref = rest[-1]
    lq = qr_ref.shape[0] // n_seq
    lk = kr_ref.shape[0] // n_seq
    lane = lax.broadcasted_iota(jnp.int32, (lq, LANES), 1)
    blocks = [(slice(s * lq, (s + 1) * lq), slice(s * lk, (s + 1) * lk),
               slice(QK_NOPE * hh, QK_NOPE * (hh + 1)), hh)
              for s in range(n_seq) for hh in range(2)]
    scores = []
    for qrows, krows, cols, hh in blocks:
        qr = qr_ref[qrows, :]
        own = (lane < QK_ROPE) if hh == 0 else (lane >= QK_ROPE)
        qrm = jnp.where(own, qr, jnp.zeros_like(qr))
        q_cat = jnp.concatenate([qn_ref[qrows, cols], qrm], axis=-1)
        k_cat = jnp.concatenate([kn_ref[krows, cols], kr_ref[krows, :]], axis=-1)
        s = lax.dot_general(q_cat, k_cat, TRANS_B, preferred_element_type=F32)
        scores.append(s * (QK_DIM ** -0.5 * LOG2_E))
    tops = [jnp.max(s, axis=-1, keepdims=True) for s in scores]
    exps = [jnp.exp2(s - m) for s, m in zip(scores, tops)]
    sums = [jnp.sum(e, axis=-1, keepdims=True) for e in exps]
    probs = [(e / t).astype(BF16) for e, t in zip(exps, sums)]
    for p, (qrows, krows, cols, _) in zip(probs, blocks):
        o_ref[qrows, cols] = jnp.dot(p, v_ref[krows, cols], preferred_element_type=F32).astype(BF16)


def _attention(qn, qr, kn, kr, v, o_prev=None):
    sample = o_prev is not None
    if sample:
        q_row0, tq, tk, n_seq = N_PROMPT, SEQ, PAST_LEN + DEC_SEQ, 1
        grid = (DEC_BATCH, C_HEADS // 2, DEC_SEQ // tq)
    else:
        n_seq = 4
        q_row0, tq, tk = 0, n_seq * SEQ, n_seq * SEQ
        grid = (BATCH // n_seq, C_HEADS // 2, 1)
    nq = grid[2]
    qspec = lambda width: pl.BlockSpec((tq, width), lambda b, p, i: (q_row0 // tq + b * nq + i, p))
    kspec = lambda width: pl.BlockSpec((tk, width), lambda b, p, i: (b, p))
    in_specs = [qspec(2 * QK_NOPE), qspec(LANES), kspec(2 * QK_NOPE), kspec(LANES), kspec(2 * C_DV)]
    args = [qn, qr, kn, kr, v]
    if sample:
        in_specs.append(pl.BlockSpec(memory_space=pl.ANY))
        args.append(o_prev)
    return pl.pallas_call(
        functools.partial(_attn_kernel, n_seq=n_seq),
        grid=grid,
        in_specs=in_specs,
        out_specs=qspec(2 * C_DV),
        out_shape=jax.ShapeDtypeStruct((N_TOK, C_HEADS * C_DV), BF16),
        input_output_aliases={5: 0} if sample else {},
        compiler_params=_params(("parallel", "parallel", "parallel")),
        name="mla_attention_sample" if sample else "mla_attention_prompt",
    )(*args)


def _rope_tables():
    rows = DEC_SEQ // GRID_W
    row = jnp.repeat(jnp.arange(rows, dtype=F32), GRID_W)
    col = jnp.tile(jnp.arange(GRID_W, dtype=F32), rows)
    inv_freq = ROPE_BASE ** (-jnp.arange(0, ROPE_AXIS, 2, dtype=F32) / ROPE_AXIS)
    ar = row[:, None] * inv_freq
    ac = col[:, None] * inv_freq
    cos = jnp.concatenate([jnp.cos(ar), jnp.cos(ar), jnp.cos(ac), jnp.cos(ac)], axis=-1)
    sin = jnp.concatenate([-jnp.sin(ar), jnp.sin(ar), -jnp.sin(ac), jnp.sin(ac)], axis=-1)
    return jnp.tile(cos, (1, 2)), jnp.tile(sin, (1, 2))


def _layer_weights(l, gla_w_gate_up, mla_w_q_up, mla_w_kv_up, mla_q_norm, mla_k_norm):
    ext = jnp.zeros((2, LANES, B_HEADS * B_DK), F32)
    ext = ext.at[0, QK_ROPE:QK_ROPE + GLA_RANK].set(gla_w_gate_up[l, 0])
    ext = ext.at[1, QK_ROPE + GLA_RANK:QK_ROPE + 2 * GLA_RANK].set(gla_w_gate_up[l, 1])
    wq = mla_w_q_up[l].reshape(Q_LORA, C_HEADS, QK_DIM)
    wq = jnp.concatenate([wq[:, :, :QK_NOPE].reshape(Q_LORA, -1),
                          wq[:, :, QK_NOPE:].reshape(Q_LORA, -1)], axis=1).astype(BF16)
    wkv = mla_w_kv_up[l].reshape(KV_LORA, C_HEADS, QK_NOPE + C_DV)
    wkv = jnp.concatenate([wkv[:, :, :QK_NOPE].reshape(KV_LORA, -1),
                           wkv[:, :, QK_NOPE:].reshape(KV_LORA, -1)], axis=1).astype(BF16)
    split = lambda g: (g[:QK_NOPE].reshape(1, QK_NOPE), jnp.tile(g[QK_NOPE:], 2).reshape(1, LANES))
    return ext.astype(BF16), wq, wkv, split(mla_q_norm[l]), split(mla_k_norm[l])


def kernel(x_prompt, x_sample, cache_mla, state_hgrn, state_gla, c, c_ctx, norm1, w_mod, b_mod, w_in,
           hgrn_lb_logits, hgrn_out_norm, gla_w_gate_up, gla_b_gate, gla_out_norm, mla_q_a_norm,
           mla_w_q_up, mla_kv_a_norm, mla_w_kv_up, mla_q_norm, mla_k_norm, w_branch_a, w_branch_b,
           w_branch_c, w_out, norm2, w_mlp_in, w_mlp_out):
    ctx_rows, lat_rows = (0, N_PROMPT), (N_PROMPT, N_SAMPLE)
    x = (x_prompt.reshape(N_PROMPT, D_MODEL), x_sample.reshape(N_SAMPLE, D_MODEL))
    cond = jnp.concatenate([c_ctx[None], c, jnp.zeros((MOD_ROWS - 1 - DEC_BATCH, D_MODEL), F32)])
    mod = _modulation(cond, w_mod, b_mod).reshape(DEPTH * MOD_ROWS, 1, N_MOD * D_MODEL)

    cos_s, sin_s = _rope_tables()
    cos_q = jnp.concatenate([jnp.ones((N_PROMPT, LANES), F32), jnp.tile(cos_s, (DEC_BATCH, 1))])
    sin_q = jnp.concatenate([jnp.zeros((N_PROMPT, LANES), F32), jnp.tile(sin_s, (DEC_BATCH, 1))])

    wb_a, wb_b, wb_c = w_branch_a.astype(BF16), w_branch_b.astype(BF16), w_branch_c.astype(BF16)
    w_in_t = jnp.swapaxes(w_in, 1, 2)
    narrow0, gate0 = Z_MAIN, Z_MAIN + 2 * GLA_RANK + Q_LORA + KV_LORA + QK_ROPE
    caches = []
    new_hgrn = new_gla = None
    for l in range(DEPTH):
        wlr, wq, wkv, (qgn, qgr), (kgn, kgr) = _layer_weights(
            l, gla_w_gate_up, mla_w_q_up, mla_w_kv_up, mla_q_norm, mla_k_norm)
        ws = _cast_rows(w_in_t, l, narrow0, gate0 - narrow0, (gate0 - narrow0) // 3, "cast_narrow")
        ws = jnp.concatenate([ws[2 * GLA_RANK:], ws[:2 * GLA_RANK]])
        wg = _cast_rows(w_in_t, l, gate0, 3 * D_MODEL, 512, "cast_gates")

        if l == 0:
            h = _norm_mod(x[0], norm1[l], mod, l, 0)
            h = _norm_mod(x[1], norm1[l], mod, l, 0, row0=N_PROMPT, o_prev=h)
        else:
            h = _norm_mod(x, norm1[l], mod, l, 0)
        z = _matmul(h, w_in_t, F32, l, Z_MAIN, transposed=True, name="in_proj")
        qlat, ckvk, zlr = _in_small(h, ws, mla_q_a_norm[l], mla_kv_a_norm[l])

        o_a, new_hgrn = _hgrn(z, hgrn_lb_logits, hgrn_out_norm[l], new_hgrn, l)
        (o_a,) = _hgrn(z, hgrn_lb_logits, hgrn_out_norm[l], state_hgrn, l, o_prev=o_a)
        o_b, new_gla = _gla(z, zlr, wlr, gla_b_gate[l], gla_out_norm[l], new_gla, l)
        (o_b,) = _gla(z, zlr, wlr, gla_b_gate[l], gla_out_norm[l], state_gla, l, o_prev=o_b)

        caches.append(ckvk[:N_PROMPT].reshape(BATCH, SEQ, KV_LORA + QK_ROPE))
        qn, qr = _qup(qlat, wq, qgn, qgr, cos_q, sin_q)
        kn_p, kr_p, v_p = _kvup(ckvk, wkv, kgn, kgr)
        kn_s, kr_s, v_s = _kvup(ckvk, wkv, kgn, kgr, cache_mla, l, cos_s, sin_s)
        o_c = _attention(qn, qr, kn_p, kr_p, v_p)
        o_c = _attention(qn, qr, kn_s, kr_s, v_s, o_prev=o_c)

        merged = _merge(h, wg, o_a, wb_a, o_b, wb_b, o_c, wb_c, l)
        if l == 0:
            xo = _matmul_residual(merged, w_out, x[0], mod, l, 2, 0, "out_proj", rows=ctx_rows)
            x = _matmul_residual(merged, w_out, x[1], mod, l, 2, 0, "out_proj", rows=lat_rows,
                                 res_row0=0, o_prev=xo)
        else:
            x = _matmul_residual(merged, w_out, x, mod, l, 2, 0, "out_proj")

        h2 = _norm_mod(x, norm2[l], mod, l, 3)
        u = _matmul(h2, w_mlp_in, BF16, l, D_FF, relu2=True, name="mlp_in")
        n_kb = D_FF // D_MODEL
        for kb in range(n_kb):
            if l == DEPTH - 1 and kb == n_kb - 1:
                x = tuple(_matmul_residual(u, w_mlp_out, x, mod, l, 5, kb, "mlp_out", rows=rows,
                                           out_rows=rows[1], out_row0=0)
                          for rows in (ctx_rows, lat_rows))
            else:
                x = _matmul_residual(u, w_mlp_out, x, mod, l, 5, kb, "mlp_out")

    y_prompt = x[0].reshape(BATCH, SEQ, D_MODEL)
    y_sample = x[1].reshape(DEC_BATCH, DEC_SEQ, D_MODEL)
    return (y_prompt, y_sample, jnp.stack(caches, axis=1), new_hgrn, new_gla)
```
